```python
import jax
import jax.numpy as jnp
from jax import lax
import numpy as np

D_MODEL = 2048
BATCH = 16
SEQ = 2048
DEPTH = 4

GRID_W = 64
CTX_LEN = 256
EPS = 1e-6
ROPE_BASE = 10000.0

A_HEAD_DIM = 128
A_HEADS = D_MODEL // 256
A_KV_HEADS = A_HEADS // 4
WINDOW = 128
BLOCK = 128

R_HEADS = D_MODEL // 256
R_DK = 128
R_DV = 256
R_CHUNK = 128

D_FF = 256 * ((8 * D_MODEL + 3 * 256 - 1) // (3 * 256))
N_EXPERTS = 8
TOP_K = 2
D_EXPERT = D_FF // 2
N_DENSE = (DEPTH + 1) // 2
N_MOE = DEPTH // 2

A_Q = A_HEADS * A_HEAD_DIM
A_KV = A_KV_HEADS * A_HEAD_DIM
R_QK = R_HEADS * R_DK
R_V = R_HEADS * R_DV
CTX_SPLIT = (A_KV, A_KV, R_QK, R_V)
Q_SPLIT = (A_Q, R_QK, R_V, D_MODEL, D_MODEL)
CTX_COLS = 2 * A_KV + R_QK + R_V
IN_COLS = CTX_COLS + A_Q + R_QK + R_V + 2 * D_MODEL

kernel_name = 'hybrid_dit_window_gqa_retention_moe'


def split_cols(p, sizes):
    cuts = [int(v) for v in np.cumsum(sizes)[:-1]]
    return jnp.split(p, cuts, axis=-1)


def rmsnorm(x, g):
    xf = x.astype(jnp.float32)
    y = xf * lax.rsqrt(jnp.mean(xf * xf, axis=-1, keepdims=True) + EPS)
    return (y * g.astype(jnp.float32)).astype(x.dtype)


def modulate(h, shift, scale):
    return h * (1 + scale) + shift


def axial_angles(t):
    rows = t // GRID_W
    row = jnp.repeat(jnp.arange(rows, dtype=jnp.float32), GRID_W)
    col = jnp.tile(jnp.arange(GRID_W, dtype=jnp.float32), rows)
    nf = A_HEAD_DIM // 4
    inv = jnp.power(ROPE_BASE, -jnp.arange(nf, dtype=jnp.float32) / nf)
    return jnp.stack([row[:, None] * inv, col[:, None] * inv], axis=1)


def retention_angles(t):
    nf = R_DK // 2
    inv = jnp.power(ROPE_BASE, -jnp.arange(nf, dtype=jnp.float32) / nf)
    return (jnp.arange(t, dtype=jnp.float32)[:, None] * inv)[:, None, :]


def rotate(x, ang):
    b, t, h, dh = x.shape
    g, f = ang.shape[1], ang.shape[2]
    xr = x.reshape(b, t, h, g, 2, f)
    cos = jnp.cos(ang)[None, :, None].astype(x.dtype)
    sin = jnp.sin(ang)[None, :, None].astype(x.dtype)
    x1, x2 = xr[..., 0, :], xr[..., 1, :]
    return jnp.stack([x1 * cos - x2 * sin, x2 * cos + x1 * sin], axis=-2).reshape(b, t, h, dh)


def window_attention(q, k, v, k_ctx, v_ctx, sink):
    b, t, ha, dh = q.shape
    kvh = k.shape[2]
    g = ha // kvh
    nb = t // BLOCK
    qb = (q * dh ** -0.5).reshape(b, nb, BLOCK, kvh, g, dh)
    pad = ((0, 0), (BLOCK, BLOCK), (0, 0), (0, 0))
    kp = jnp.pad(k, pad).reshape(b, nb + 2, BLOCK, kvh, dh)
    vp = jnp.pad(v, pad).reshape(b, nb + 2, BLOCK, kvh, dh)
    kb = jnp.concatenate([kp[:, 0:nb], kp[:, 1:nb + 1], kp[:, 2:nb + 2]], axis=2)
    vb = jnp.concatenate([vp[:, 0:nb], vp[:, 1:nb + 1], vp[:, 2:nb + 2]], axis=2)
    s_loc = jnp.einsum('bnqkgd,bnskd->bnkgqs', qb, kb).astype(jnp.float32)
    s_ctx = jnp.einsum('bnqkgd,bckd->bnkgqc', qb, k_ctx).astype(jnp.float32)
    qi = jnp.arange(BLOCK)[:, None]
    kj = jnp.arange(3 * BLOCK)[None, :]
    rel = kj - qi
    s_pos = jnp.arange(nb)[:, None, None] * BLOCK - BLOCK + kj[None]
    valid = (rel >= BLOCK - WINDOW) & (rel <= BLOCK + WINDOW) & (s_pos >= 0) & (s_pos < t)
    s_loc = jnp.where(valid[None, :, None, None], s_loc, jnp.finfo(jnp.float32).min)
    sink_l = jnp.broadcast_to(sink.astype(jnp.float32).reshape(kvh, g)[None, None, :, :, None, None],
                              (b, nb, kvh, g, BLOCK, 1))
    p = jax.nn.softmax(jnp.concatenate([s_loc, s_ctx, sink_l], axis=-1), axis=-1).astype(v.dtype)
    n_loc = 3 * BLOCK
    n_ctx = k_ctx.shape[1]
    o = (jnp.einsum('bnkgqs,bnskd->bnqkgd', p[..., :n_loc], vb)
         + jnp.einsum('bnkgqc,bckd->bnqkgd', p[..., n_loc:n_loc + n_ctx], v_ctx))
    return o.reshape(b, t, ha * dh)


def context_attention(q, k, v, sink):
    b, l, ha, dh = q.shape
    kvh = k.shape[2]
    g = ha // kvh
    qg = (q * dh ** -0.5).reshape(b, l, kvh, g, dh)
    s = jnp.einsum('blkgd,bckd->bkglc', qg, k).astype(jnp.float32)
    sink_b = jnp.broadcast_to(sink.astype(jnp.float32).reshape(kvh, g)[None, :, :, None, None], (b, kvh, g, l, 1))
    p = jax.nn.softmax(jnp.concatenate([s, sink_b], axis=-1), axis=-1)[..., :-1].astype(v.dtype)
    return jnp.einsum('bkglc,bckd->blkgd', p, v).reshape(b, l, ha * dh)


def retention_scan(q, k, v, log_gamma, state0):
    b, l, h, dk = q.shape
    dv = v.shape[-1]
    n = l // R_CHUNK
    qc = q.reshape(b, n, R_CHUNK, h, dk).astype(jnp.float32)
    kc = k.reshape(b, n, R_CHUNK, h, dk).astype(jnp.float32)
    vc = v.reshape(b, n, R_CHUNK, h, dv).astype(jnp.float32)
    pos = jnp.arange(R_CHUNK, dtype=jnp.float32)
    diff = pos[:, None] - pos[None, :]
    decay = jnp.where(diff >= 0, jnp.exp(log_gamma[:, None, None] * jnp.maximum(diff, 0.0)), 0.0)
    scores = jnp.einsum('bnihd,bnjhd->bnhij', qc, kc) * decay
    inner = jnp.einsum('bnhij,bnjhe->bnihe', scores, vc)
    zeta = jnp.exp(log_gamma[None, :] * (R_CHUNK - 1 - pos)[:, None])
    u = jnp.einsum('bnjhd,bnjhe->nbhde', kc * zeta[:, :, None], vc)
    chunk_decay = jnp.exp(log_gamma * R_CHUNK)[None, :, None, None]

    def step(r, u_i):
        return chunk_decay * r + u_i, r

    _, r_prev = lax.scan(step, state0, u)
    xi = jnp.exp(log_gamma[None, :] * (pos + 1)[:, None])
    cross = jnp.einsum('bnihd,nbhde->bnihe', qc * xi[:, :, None], r_prev)
    return (inner + cross).reshape(b, l, h, dv).astype(v.dtype)


def context_state(k, v, log_gamma, reverse):
    l = k.shape[1]
    pos = jnp.arange(l, dtype=jnp.float32)
    expo = pos if reverse else (l - 1 - pos)
    w = jnp.exp(log_gamma[None, :] * expo[:, None])
    return jnp.einsum('blhd,blhe->bhde', k.astype(jnp.float32) * w[None, :, :, None], v.astype(jnp.float32))


def bidir_retention(q, k, v, lg_f, lg_b, s_f, s_b):
    fwd = retention_scan(q, k, v, lg_f, s_f)
    bwd = retention_scan(q[:, ::-1], k[:, ::-1], v[:, ::-1], lg_b, s_b)[:, ::-1]
    return fwd + bwd


def retention_out(o, g_r, gn):
    b, t, h, dv = o.shape
    of = o.astype(jnp.float32)
    mu = jnp.mean(of, axis=-1, keepdims=True)
    var = jnp.mean(jnp.square(of - mu), axis=-1, keepdims=True)
    on = ((of - mu) * lax.rsqrt(var + EPS)).reshape(b, t, h * dv) * gn.astype(jnp.float32)
    return jax.nn.silu(g_r) * on.astype(g_r.dtype)


def merge_branches(y_a, y_r, z_a, z_r, w_br_a, w_br_r, w_out):
    m = jax.nn.sigmoid(z_a) * (y_a @ w_br_a) + jax.nn.sigmoid(z_r) * (y_r @ w_br_r)
    return m @ w_out


def token_mixers(h, hc, w_in, sink, lg_f, lg_b, gn, w_br_a, w_br_r, w_out, ang_a, ang_r, with_ctx_out):
    b, t, _ = h.shape
    l = hc.shape[1]
    k_a, v_a, k_r, v_r, q_a, q_r, g_r, z_a, z_r = split_cols(h @ w_in, CTX_SPLIT + Q_SPLIT)
    q_a = rotate(q_a.reshape(b, t, A_HEADS, A_HEAD_DIM), ang_a)
    k_a = rotate(k_a.reshape(b, t, A_KV_HEADS, A_HEAD_DIM), ang_a)
    v_a = v_a.reshape(b, t, A_KV_HEADS, A_HEAD_DIM)
    q_r = rotate(q_r.reshape(b, t, R_HEADS, R_DK), ang_r)
    k_r = rotate(k_r.reshape(b, t, R_HEADS, R_DK), ang_r) * R_DK ** -0.5
    v_r = v_r.reshape(b, t, R_HEADS, R_DV)

    pc = hc @ (w_in if with_ctx_out else w_in[:, :CTX_COLS])
    kc_a, vc_a, kc_r, vc_r = split_cols(pc[..., :CTX_COLS], CTX_SPLIT)
    kc_a = kc_a.reshape(b, l, A_KV_HEADS, A_HEAD_DIM)
    vc_a = vc_a.reshape(b, l, A_KV_HEADS, A_HEAD_DIM)
    kc_r = kc_r.reshape(b, l, R_HEADS, R_DK) * R_DK ** -0.5
    vc_r = vc_r.reshape(b, l, R_HEADS, R_DV)
    s_f = context_state(kc_r, vc_r, lg_f, reverse=False)
    s_b = context_state(kc_r, vc_r, lg_b, reverse=True)

    y_a = window_attention(q_a, k_a, v_a, kc_a, vc_a, sink)
    y_r = retention_out(bidir_retention(q_r, k_r, v_r, lg_f, lg_b, s_f, s_b), g_r, gn)
    y = merge_branches(y_a, y_r, z_a, z_r, w_br_a, w_br_r, w_out)
    if not with_ctx_out:
        return y, None

    qc_a, qc_r, gc_r, zc_a, zc_r = split_cols(pc[..., CTX_COLS:], Q_SPLIT)
    yc_a = context_attention(qc_a.reshape(b, l, A_HEADS, A_HEAD_DIM), kc_a, vc_a, sink)
    zero = jnp.zeros((b, R_HEADS, R_DK, R_DV), jnp.float32)
    yc_r = retention_out(bidir_retention(qc_r.reshape(b, l, R_HEADS, R_DK), kc_r, vc_r, lg_f, lg_b, zero, zero),
                         gc_r, gn)
    yc = merge_branches(yc_a, yc_r, zc_a, zc_r, w_br_a, w_br_r, w_out)
    return y, yc


def swiglu(h, wg, wu, wd):
    return (jax.nn.silu(h @ wg) * (h @ wu)) @ wd


def moe_swiglu(h, w_router, wg, wu, wd):
    logits = (h @ w_router).astype(jnp.float32)
    top_val, top_idx = lax.top_k(logits, TOP_K)
    top_w = jax.nn.softmax(top_val, axis=-1)
    combine = jnp.einsum('btk,btke->bte', top_w,
                         jax.nn.one_hot(top_idx, N_EXPERTS, dtype=jnp.float32)).astype(h.dtype)
    out = jnp.zeros_like(h)
    for e in range(N_EXPERTS):
        out = out + combine[..., e:e + 1] * swiglu(h, wg[e], wu[e], wd[e])
    return out


def channel_mixer(layer, h, w_gate_d, w_up_d, w_down_d, w_router, w_gate_e, w_up_e, w_down_e):
    i = layer // 2
    if layer % 2 == 0:
        return swiglu(h, w_gate_d[i], w_up_d[i], w_down_d[i])
    return moe_swiglu(h, w_router[i], w_gate_e[i], w_up_e[i], w_down_e[i])


def setup_inputs(seed: int = 0) -> dict:
    key = jax.random.key(seed)
    ks = jax.random.split(key, 24)
    f32 = jnp.float32
    d = D_MODEL

    def nrm(k, shape, scale):
        return jax.random.normal(k, shape, f32) * scale

    head_scales = 5.0 + jnp.arange(R_HEADS, dtype=f32)
    return {
        'x': nrm(ks[0], (BATCH, SEQ, d), 1.0),
        'c': nrm(ks[1], (BATCH, d), 1.0),
        'ctx': nrm(ks[2], (BATCH, CTX_LEN, d), 1.0),
        'c_ctx': nrm(ks[3], (d,), 1.0),
        'w_ada': nrm(ks[4], (DEPTH, d, 6 * d), 0.5 * d ** -0.5),
        'b_ada': nrm(ks[5], (DEPTH, 6 * d), 0.02),
        'g_mix': 1.0 + nrm(ks[6], (DEPTH, d), 0.05),
        'g_ffn': 1.0 + nrm(ks[7], (DEPTH, d), 0.05),
        'w_in': nrm(ks[8], (DEPTH, d, IN_COLS), d ** -0.5),
        'attn_sink': nrm(ks[9], (DEPTH, A_HEADS), 0.5),
        'ret_a_fwd': head_scales + nrm(ks[10], (DEPTH, R_HEADS), 0.1),
        'ret_a_bwd': head_scales + nrm(ks[11], (DEPTH, R_HEADS), 0.1),
        'ret_gn': 1.0 + nrm(ks[12], (DEPTH, R_V), 0.05),
        'w_br_attn': nrm(ks[13], (DEPTH, A_Q, d), A_Q ** -0.5),
        'w_br_ret': nrm(ks[14], (DEPTH, R_V, d), R_V ** -0.5),
        'w_out': nrm(ks[15], (DEPTH, d, d), d ** -0.5),
        'w_gate_d': nrm(ks[16], (N_DENSE, d, D_FF), d ** -0.5),
        'w_up_d': nrm(ks[17], (N_DENSE, d, D_FF), d ** -0.5),
        'w_down_d': nrm(ks[18], (N_DENSE, D_FF, d), D_FF ** -0.5),
        'w_router': nrm(ks[19], (N_MOE, d, N_EXPERTS), d ** -0.5),
        'w_gate_e': nrm(ks[20], (N_MOE, N_EXPERTS, d, D_EXPERT), d ** -0.5),
        'w_up_e': nrm(ks[21], (N_MOE, N_EXPERTS, d, D_EXPERT), d ** -0.5),
        'w_down_e': nrm(ks[22], (N_MOE, N_EXPERTS, D_EXPERT, d), D_EXPERT ** -0.5),
        'g_final': 1.0 + nrm(ks[23], (d,), 0.05),
    }


def reference(x, c, ctx, c_ctx, w_ada, b_ada, g_mix, g_ffn, w_in, attn_sink,
              ret_a_fwd, ret_a_bwd, ret_gn, w_br_attn, w_br_ret, w_out,
              w_gate_d, w_up_d, w_down_d, w_router, w_gate_e, w_up_e, w_down_e, g_final):
    t = x.shape[1]
    ang_a = axial_angles(t)
    ang_r = retention_angles(t)
    silu_c = jax.nn.silu(c)
    silu_cc = jax.nn.silu(c_ctx)
    xc = ctx
    for layer in range(DEPTH):
        last = layer == DEPTH - 1
        mod = silu_c @ w_ada[layer] + b_ada[layer]
        mod_c = silu_cc @ w_ada[layer] + b_ada[layer]
        sh_m, sc_m, gt_m, sh_f, sc_f, gt_f = [m[:, None] for m in jnp.split(mod, 6, axis=-1)]
        csh_m, csc_m, cgt_m, csh_f, csc_f, cgt_f = jnp.split(mod_c, 6, axis=-1)
        lg_f = jnp.log1p(-jnp.exp2(-ret_a_fwd[layer].astype(jnp.float32)))
        lg_b = jnp.log1p(-jnp.exp2(-ret_a_bwd[layer].astype(jnp.float32)))

        h = modulate(rmsnorm(x, g_mix[layer]), sh_m, sc_m)
        hc = modulate(rmsnorm(xc, g_mix[layer]), csh_m, csc_m)
        y, yc = token_mixers(h, hc, w_in[layer], attn_sink[layer], lg_f, lg_b, ret_gn[layer],
                             w_br_attn[layer], w_br_ret[layer], w_out[layer], ang_a, ang_r, not last)
        x = x + gt_m * y
        hf = modulate(rmsnorm(x, g_ffn[layer]), sh_f, sc_f)
        x = x + gt_f * channel_mixer(layer, hf, w_gate_d, w_up_d, w_down_d, w_router, w_gate_e, w_up_e, w_down_e)
        if not last:
            xc = xc + cgt_m * yc
            hcf = modulate(rmsnorm(xc, g_ffn[layer]), csh_f, csc_f)
            xc = xc + cgt_f * channel_mixer(layer, hcf, w_gate_d, w_up_d, w_down_d,
                                            w_router, w_gate_e, w_up_e, w_down_e)
    return rmsnorm(x, g_final)
```

```python
import functools

import numpy as np
import jax
import jax.numpy as jnp
from jax import lax
from jax.experimental import pallas as pl
from jax.experimental.pallas import tpu as pltpu

F32 = jnp.float32
BF16 = jnp.bfloat16

EPS = 1e-6
ROPE_BASE = 10000.0
GRID_W = 64
HEAD_DIM = 128
R_DV = 256
BLOCK = 128
Q_PER_KV = 4
N_EXPERTS = 8
LANES = 128

V7X_VMEM_BYTES = 64 * 1024 * 1024
VMEM_LIMIT = V7X_VMEM_BYTES - 8 * 1024 * 1024
ROW_TILE = 1024
NORM_ROW_TILE = 256


def _cparams(n_axes):
    return pltpu.CompilerParams(dimension_semantics=("arbitrary",) * n_axes,
                                vmem_limit_bytes=VMEM_LIMIT)


def _dot(a, b):
    return jnp.dot(a, b, preferred_element_type=F32)


def _dot_nt(a, b):
    return lax.dot_general(a, b, (((1,), (1,)), ((), ())), preferred_element_type=F32)


def _dot_tn(a, b):
    return lax.dot_general(a, b, (((0,), (0,)), ((), ())), preferred_element_type=F32)


def _silu(x):
    return x * jax.nn.sigmoid(x)


def _ada_kernel(c_ref, w_ref, b_ref, o_ref):
    s = _silu(c_ref[...]).astype(BF16)
    o_ref[...] = _dot(s, w_ref[...].astype(BF16)) + b_ref[...]


def _ada_call(c_all, w_ada, b_ada):
    depth, d, n6 = w_ada.shape
    r = c_all.shape[0]
    bn = 1024
    return pl.pallas_call(
        _ada_kernel,
        grid=(depth, n6 // bn),
        in_specs=[pl.BlockSpec((r, d), lambda l, j: (0, 0)),
                  pl.BlockSpec((None, d, bn), lambda l, j: (l, 0, j)),
                  pl.BlockSpec((None, 1, bn), lambda l, j: (l, 0, j))],
        out_specs=pl.BlockSpec((None, r, bn), lambda l, j: (l, 0, j)),
        out_shape=jax.ShapeDtypeStruct((depth, r, n6), F32),
        compiler_params=_cparams(2),
        name="ada_mod",
    )(c_all, w_ada, b_ada.reshape(depth, 1, n6))


def _norm_mod_kernel(x_ref, g_ref, sh_ref, sc_ref, o_ref):
    x = x_ref[...]
    y = x * lax.rsqrt(jnp.mean(x * x, axis=-1, keepdims=True) + EPS) * g_ref[...]
    o_ref[...] = (y * (1.0 + sc_ref[...]) + sh_ref[...]).astype(o_ref.dtype)


def _norm_kernel(x_ref, g_ref, o_ref):
    x = x_ref[...]
    o_ref[...] = x * lax.rsqrt(jnp.mean(x * x, axis=-1, keepdims=True) + EPS) * g_ref[...]


def _norm_mod_call(cfg, x_all, g, mod_l, shift_chunk, n_rows):
    d, bm = cfg.d, cfg.norm_bm
    bidx = cfg.bidx_for(bm)
    return pl.pallas_call(
        _norm_mod_kernel,
        grid=(n_rows // bm,),
        in_specs=[pl.BlockSpec((bm, d), lambda i: (i, 0)),
                  pl.BlockSpec((1, d), lambda i: (0, 0)),
                  pl.BlockSpec((None, 1, d), lambda i: (bidx(i), 0, shift_chunk)),
                  pl.BlockSpec((None, 1, d), lambda i: (bidx(i), 0, shift_chunk + 1))],
        out_specs=pl.BlockSpec((bm, d), lambda i: (i, 0)),
        out_shape=jax.ShapeDtypeStruct((cfg.ntok, d), BF16),
        compiler_params=_cparams(1),
        name="norm_mod",
    )(x_all, g.reshape(1, d), mod_l, mod_l)


def _final_norm_call(cfg, x_all, g):
    d, bm = cfg.d, cfg.norm_bm
    n_rows = cfg.b * cfg.t
    return pl.pallas_call(
        _norm_kernel,
        grid=(n_rows // bm,),
        in_specs=[pl.BlockSpec((bm, d), lambda i: (i, 0)),
                  pl.BlockSpec((1, d), lambda i: (0, 0))],
        out_specs=pl.BlockSpec((bm, d), lambda i: (i, 0)),
        out_shape=jax.ShapeDtypeStruct((n_rows, d), F32),
        compiler_params=_cparams(1),
        name="final_norm",
    )(x_all, g.reshape(1, d))


def _mm_kernel(a_ref, w_ref, o_ref):
    o_ref[...] = _dot(a_ref[...], w_ref[...]).astype(o_ref.dtype)


def _in_proj_call(cfg, h, w):
    d, bm = cfg.d, cfg.bm
    n = w.shape[1]
    bn = 512
    return pl.pallas_call(
        _mm_kernel,
        grid=(cfg.ntok // bm, n // bn),
        in_specs=[pl.BlockSpec((bm, d), lambda i, j: (i, 0)),
                  pl.BlockSpec((d, bn), lambda i, j: (0, j))],
        out_specs=pl.BlockSpec((bm, bn), lambda i, j: (i, j)),
        out_shape=jax.ShapeDtypeStruct((cfg.ntok, n), BF16),
        compiler_params=_cparams(2),
        name="in_proj",
    )(h, w)


def _merge_kernel(ya_ref, yr_ref, wa_ref, wr_ref, za_ref, zr_ref, o_ref):
    a = _dot(ya_ref[...], wa_ref[...])
    r = _dot(yr_ref[...], wr_ref[...])
    za = za_ref[...].astype(F32)
    zr = zr_ref[...].astype(F32)
    o_ref[...] = (jax.nn.sigmoid(za) * a + jax.nn.sigmoid(zr) * r).astype(o_ref.dtype)


def _merge_call(cfg, y_a, y_r, w_a, w_r, p, n_rows):
    d, bm = cfg.d, cfg.bm
    bn = 512
    za_blk, zr_blk = cfg.off_za // bn, cfg.off_zr // bn
    return pl.pallas_call(
        _merge_kernel,
        grid=(n_rows // bm, d // bn),
        in_specs=[pl.BlockSpec((bm, cfg.a_q), lambda i, j: (i, 0)),
                  pl.BlockSpec((bm, cfg.r_v), lambda i, j: (i, 0)),
                  pl.BlockSpec((cfg.a_q, bn), lambda i, j: (0, j)),
                  pl.BlockSpec((cfg.r_v, bn), lambda i, j: (0, j)),
                  pl.BlockSpec((bm, bn), lambda i, j: (i, za_blk + j)),
                  pl.BlockSpec((bm, bn), lambda i, j: (i, zr_blk + j))],
        out_specs=pl.BlockSpec((bm, bn), lambda i, j: (i, j)),
        out_shape=jax.ShapeDtypeStruct((cfg.ntok, d), BF16),
        compiler_params=_cparams(2),
        name="merge",
    )(y_a, y_r, w_a, w_r, p, p)


def _resid_kernel(a_ref, w_ref, x_ref, gt_ref, o_ref):
    o_ref[...] = x_ref[...] + gt_ref[...] * _dot(a_ref[...], w_ref[...])


def _resid_comb_kernel(a_ref, w_ref, x_ref, gt_ref, comb_ref, o_ref, *, expert):
    c = comb_ref[:, expert:expert + 1]
    o_ref[...] = x_ref[...] + (gt_ref[...] * c) * _dot(a_ref[...], w_ref[...])


def _resid_call(cfg, a, w, x_all, mod_l, gate_chunk, n_rows, bn, comb=None, expert=None):
    d, bm = cfg.d, cfg.bm
    k = a.shape[1]
    bidx = cfg.bidx_for(bm)
    gblk = gate_chunk * (d // bn)
    if w.ndim == 3:
        w_spec = pl.BlockSpec((None, k, bn), lambda i, j: (expert, 0, j))
    else:
        w_spec = pl.BlockSpec((k, bn), lambda i, j: (0, j))
    in_specs = [pl.BlockSpec((bm, k), lambda i, j: (i, 0)),
                w_spec,
                pl.BlockSpec((bm, bn), lambda i, j: (i, j)),
                pl.BlockSpec((None, 1, bn), lambda i, j: (bidx(i), 0, gblk + j))]
    args = [a, w, x_all, mod_l]
    kern = _resid_kernel
    if comb is not None:
        in_specs.append(pl.BlockSpec((bm, LANES), lambda i, j: (i, 0)))
        args.append(comb)
        kern = functools.partial(_resid_comb_kernel, expert=expert)
    return pl.pallas_call(
        kern,
        grid=(n_rows // bm, d // bn),
        in_specs=in_specs,
        out_specs=pl.BlockSpec((bm, bn), lambda i, j: (i, j)),
        out_shape=jax.ShapeDtypeStruct((cfg.ntok, d), F32),
        input_output_aliases={2: 0},
        compiler_params=_cparams(2),
        name="resid_proj",
    )(*args)


def _glu_kernel(a_ref, wg_ref, wu_ref, o_ref):
    a = a_ref[...]
    g = _dot(a, wg_ref[...])
    u = _dot(a, wu_ref[...])
    o_ref[...] = (_silu(g) * u).astype(o_ref.dtype)


def _glu_call(cfg, h, wg, wu, n_rows, expert=None):
    d, bm = cfg.d, cfg.bm
    n = wg.shape[-1]
    bn = 512
    if wg.ndim == 3:
        w_spec = pl.BlockSpec((None, d, bn), lambda i, j: (expert, 0, j))
    else:
        w_spec = pl.BlockSpec((d, bn), lambda i, j: (0, j))
    return pl.pallas_call(
        _glu_kernel,
        grid=(n_rows // bm, pl.cdiv(n, bn)),
        in_specs=[pl.BlockSpec((bm, d), lambda i, j: (i, 0)), w_spec, w_spec],
        out_specs=pl.BlockSpec((bm, bn), lambda i, j: (i, j)),
        out_shape=jax.ShapeDtypeStruct((cfg.ntok, n), BF16),
        compiler_params=_cparams(2),
        name="glu_up",
    )(h, wg, wu)


def _router_kernel(h_ref, w_ref, comb_ref):
    logits = _dot(h_ref[...], w_ref[...])
    lane = lax.broadcasted_iota(jnp.int32, logits.shape, 1).astype(F32)
    neg = -jnp.inf
    lg = jnp.where(lane < N_EXPERTS, logits, neg)
    m1 = jnp.max(lg, axis=-1, keepdims=True)
    i1 = jnp.min(jnp.where(lg == m1, lane, float(LANES)), axis=-1, keepdims=True)
    lg2 = jnp.where(lane == i1, neg, lg)
    m2 = jnp.max(lg2, axis=-1, keepdims=True)
    i2 = jnp.min(jnp.where(lg2 == m2, lane, float(LANES)), axis=-1, keepdims=True)
    e = jnp.exp(m2 - m1)
    w1 = 1.0 / (1.0 + e)
    w2 = e / (1.0 + e)
    comb_ref[...] = jnp.where(lane == i1, w1, 0.0) + jnp.where(lane == i2, w2, 0.0)


def _router_call(cfg, h, w_router_pad, n_rows):
    d, bm = cfg.d, cfg.bm
    return pl.pallas_call(
        _router_kernel,
        grid=(n_rows // bm,),
        in_specs=[pl.BlockSpec((bm, d), lambda i: (i, 0)),
                  pl.BlockSpec((d, LANES), lambda i: (0, 0))],
        out_specs=pl.BlockSpec((bm, LANES), lambda i: (i, 0)),
        out_shape=jax.ShapeDtypeStruct((cfg.ntok, LANES), F32),
        compiler_params=_cparams(1),
        name="router",
    )(h, w_router_pad)


def _softmax_pv(s, sink_col, v):
    m = jnp.maximum(jnp.max(s, axis=-1, keepdims=True), sink_col)
    p = jnp.exp(s - m)
    denom = jnp.sum(p, axis=-1, keepdims=True) + jnp.exp(sink_col - m)
    return _dot(p.astype(BF16), v) / denom


def _sink_column(sink_ref, group, rows):
    r = lax.broadcasted_iota(jnp.int32, (Q_PER_KV * rows, 1), 0)
    col = jnp.full((Q_PER_KV * rows, 1), sink_ref[group * Q_PER_KV], F32)
    for j in range(1, Q_PER_KV):
        col = jnp.where(r >= j * rows, sink_ref[group * Q_PER_KV + j], col)
    return col


def _attn_lat_kernel(sink_ref, *refs, kvh, t):
    q_refs = refs[:kvh]
    kp_ref, kc_ref, kn_ref, vp_ref, vc_ref, vn_ref, kx_ref, vx_ref, cos_ref, sin_ref, o_ref = refs[kvh:]
    n = pl.program_id(1)
    nb = t // BLOCK
    lane = lax.broadcasted_iota(jnp.int32, (BLOCK, HEAD_DIM), 1)
    first_half = (lane & (HEAD_DIM // 4)) == 0

    def rot(x, blk):
        start = pl.multiple_of(blk * BLOCK, BLOCK)
        c = cos_ref[pl.ds(start, BLOCK), :]
        s = sin_ref[pl.ds(start, BLOCK), :]
        partner = jnp.where(first_half, pltpu.roll(x, HEAD_DIM - HEAD_DIM // 4, 1),
                            pltpu.roll(x, HEAD_DIM // 4, 1))
        return x * c + partner * s

    blk_p = jnp.maximum(n - 1, 0)
    blk_n = jnp.minimum(n + 1, nb - 1)
    n_loc = 3 * BLOCK
    n_keys = n_loc + kx_ref.shape[0]
    rows = Q_PER_KV * BLOCK
    qi = lax.broadcasted_iota(jnp.int32, (rows, n_keys), 0) & (BLOCK - 1)
    kj = lax.broadcasted_iota(jnp.int32, (rows, n_keys), 1)
    rel = kj - qi
    s_pos = n * BLOCK - BLOCK + kj
    valid = (kj >= n_loc) | ((rel >= 0) & (rel <= 2 * BLOCK) & (s_pos >= 0) & (s_pos < t))
    scale = HEAD_DIM ** -0.5

    for g in range(kvh):
        hs = slice(g * HEAD_DIM, (g + 1) * HEAD_DIM)
        q = jnp.concatenate(
            [rot(q_refs[g][:, j * HEAD_DIM:(j + 1) * HEAD_DIM].astype(F32), n) * scale
             for j in range(Q_PER_KV)], axis=0).astype(BF16)
        k = jnp.concatenate(
            [rot(kp_ref[:, hs].astype(F32), blk_p).astype(BF16),
             rot(kc_ref[:, hs].astype(F32), n).astype(BF16),
             rot(kn_ref[:, hs].astype(F32), blk_n).astype(BF16),
             kx_ref[:, hs]], axis=0)
        v = jnp.concatenate([vp_ref[:, hs], vc_ref[:, hs], vn_ref[:, hs], vx_ref[:, hs]], axis=0)
        s = jnp.where(valid, _dot_nt(q, k), jnp.finfo(F32).min)
        o = _softmax_pv(s, _sink_column(sink_ref, g, BLOCK), v)
        for j in range(Q_PER_KV):
            h = g * Q_PER_KV + j
            o_ref[:, h * HEAD_DIM:(h + 1) * HEAD_DIM] = o[j * BLOCK:(j + 1) * BLOCK].astype(o_ref.dtype)


def _attn_lat_call(cfg, p, sink, cos_a, sin_a):
    b, t, l = cfg.b, cfg.t, cfg.l
    nb = t // BLOCK
    kvh = cfg.kvh
    gw = Q_PER_KV * HEAD_DIM
    kvw = kvh * HEAD_DIM
    q_blk = cfg.off_qa // gw
    k_blk, v_blk = cfg.off_ka // kvw, cfg.off_va // kvw
    ctx_row0 = (b * t) // l

    def kv_spec(col_blk, shift):
        def imap(bi, n):
            return (bi * nb + jnp.clip(n + shift, 0, nb - 1), col_blk)
        return pl.BlockSpec((BLOCK, kvw), imap)

    in_specs = [pl.BlockSpec(memory_space=pltpu.SMEM)]
    in_specs += [pl.BlockSpec((BLOCK, gw), functools.partial(lambda bi, n, g: (bi * nb + n, q_blk + g), g=g))
                 for g in range(kvh)]
    in_specs += [kv_spec(k_blk, -1), kv_spec(k_blk, 0), kv_spec(k_blk, 1),
                 kv_spec(v_blk, -1), kv_spec(v_blk, 0), kv_spec(v_blk, 1),
                 pl.BlockSpec((l, kvw), lambda bi, n: (ctx_row0 + bi, k_blk)),
                 pl.BlockSpec((l, kvw), lambda bi, n: (ctx_row0 + bi, v_blk)),
                 pl.BlockSpec((t, HEAD_DIM), lambda bi, n: (0, 0)),
                 pl.BlockSpec((t, HEAD_DIM), lambda bi, n: (0, 0))]
    return pl.pallas_call(
        functools.partial(_attn_lat_kernel, kvh=kvh, t=t),
        grid=(b, nb),
        in_specs=in_specs,
        out_specs=pl.BlockSpec((BLOCK, cfg.a_q), lambda bi, n: (bi * nb + n, 0)),
        out_shape=jax.ShapeDtypeStruct((cfg.ntok, cfg.a_q), BF16),
        compiler_params=_cparams(2),
        name="attn_latent",
    )(sink, *([p] * (kvh + 8)), cos_a, sin_a)


def _attn_ctx_kernel(sink_ref, *refs, kvh):
    q_refs = refs[:kvh]
    kx_ref, vx_ref, _, o_ref = refs[kvh:]
    l = kx_ref.shape[0]
    scale = HEAD_DIM ** -0.5
    for g in range(kvh):
        hs = slice(g * HEAD_DIM, (g + 1) * HEAD_DIM)
        q = jnp.concatenate(
            [(q_refs[g][:, j * HEAD_DIM:(j + 1) * HEAD_DIM].astype(F32) * scale).astype(BF16)
             for j in range(Q_PER_KV)], axis=0)
        s = _dot_nt(q, kx_ref[:, hs])
        o = _softmax_pv(s, _sink_column(sink_ref, g, l), vx_ref[:, hs])
        for j in range(Q_PER_KV):
            h = g * Q_PER_KV + j
            o_ref[:, h * HEAD_DIM:(h + 1) * HEAD_DIM] = o[j * l:(j + 1) * l].astype(o_ref.dtype)


def _attn_ctx_call(cfg, p, sink, y_a):
    b, t, l = cfg.b, cfg.t, cfg.l
    kvh = cfg.kvh
    gw = Q_PER_KV * HEAD_DIM
    kvw = kvh * HEAD_DIM
    q_blk = cfg.off_qa // gw
    k_blk, v_blk = cfg.off_ka // kvw, cfg.off_va // kvw
    ctx_row0 = (b * t) // l
    in_specs = [pl.BlockSpec(memory_space=pltpu.SMEM)]
    in_specs += [pl.BlockSpec((l, gw), functools.partial(lambda bi, g: (ctx_row0 + bi, q_blk + g), g=g))
                 for g in range(kvh)]
    in_specs += [pl.BlockSpec((l, kvw), lambda bi: (ctx_row0 + bi, k_blk)),
                 pl.BlockSpec((l, kvw), lambda bi: (ctx_row0 + bi, v_blk)),
                 pl.BlockSpec(memory_space=pl.ANY)]
    return pl.pallas_call(
        functools.partial(_attn_ctx_kernel, kvh=kvh),
        grid=(b,),
        in_specs=in_specs,
        out_specs=pl.BlockSpec((l, cfg.a_q), lambda bi: (ctx_row0 + bi, 0)),
        out_shape=jax.ShapeDtypeStruct((cfg.ntok, cfg.a_q), BF16),
        input_output_aliases={kvh + 3: 0},
        compiler_params=_cparams(1),
        name="attn_ctx",
    )(sink, *([p] * (kvh + 2)), y_a)


def _retention_scan(q_of, k_of, v_ref, g_ref, gn, lg_f, lg_b, o_ref, acc_ref, n_chunks, s_f, s_b):
    c = BLOCK
    ii = lax.broadcasted_iota(jnp.int32, (c, c), 0).astype(F32)
    jj = lax.broadcasted_iota(jnp.int32, (c, c), 1).astype(F32)
    diff = ii - jj
    decay = jnp.where(diff > 0, jnp.exp(lg_f * jnp.maximum(diff, 0.0)),
                      jnp.where(diff < 0, jnp.exp(lg_b * jnp.maximum(-diff, 0.0)), 2.0))
    zeta_f = jnp.exp(lg_f * (c - 1 - ii))
    xi_f = jnp.exp(lg_f * (ii + 1))
    zeta_b = jnp.exp(lg_b * ii)
    xi_b = jnp.exp(lg_b * (c - ii))
    cd_f = jnp.exp(jnp.concatenate([lg_f, lg_f], axis=1) * c)
    cd_b = jnp.exp(jnp.concatenate([lg_b, lg_b], axis=1) * c)

    for i in reversed(range(n_chunks)):
        rows = slice(i * c, (i + 1) * c)
        q, k, v = q_of(i), k_of(i), v_ref[rows, :]
        acc_ref[rows, :] = _dot((q * xi_b).astype(BF16), s_b.astype(BF16))
        s_b = cd_b * s_b + _dot_tn((k * zeta_b).astype(BF16), v)

    for i in range(n_chunks):
        rows = slice(i * c, (i + 1) * c)
        q, k, v = q_of(i), k_of(i), v_ref[rows, :]
        scores = _dot_nt(q.astype(BF16), k.astype(BF16)) * decay
        o = (acc_ref[rows, :] + _dot(scores.astype(BF16), v)
             + _dot((q * xi_f).astype(BF16), s_f.astype(BF16)))
        s_f = cd_f * s_f + _dot_tn((k * zeta_f).astype(BF16), v)
        mu = jnp.mean(o, axis=-1, keepdims=True)
        dev = o - mu
        var = jnp.mean(dev * dev, axis=-1, keepdims=True)
        on = dev * lax.rsqrt(var + EPS) * gn
        gate = g_ref[rows, :].astype(F32)
        o_ref[rows, :] = (_silu(gate) * on).astype(o_ref.dtype)
    return s_f, s_b


def _log_decay(a_row):
    return jnp.log1p(-jnp.exp2(-a_row))


def _ret_ctx_kernel(q_ref, k_ref, v_ref, g_ref, gn_ref, af_ref, ab_ref, o_ref, sf_ref, sb_ref, acc_ref):
    scale = HEAD_DIM ** -0.5
    n_chunks = q_ref.shape[0] // BLOCK

    def q_of(i):
        return q_ref[i * BLOCK:(i + 1) * BLOCK, :].astype(F32)

    def k_of(i):
        return k_ref[i * BLOCK:(i + 1) * BLOCK, :].astype(F32) * scale

    zero = jnp.zeros((HEAD_DIM, R_DV), F32)
    s_f, s_b = _retention_scan(q_of, k_of, v_ref, g_ref, gn_ref[...], _log_decay(af_ref[...]),
                               _log_decay(ab_ref[...]), o_ref, acc_ref, n_chunks, zero, zero)
    sf_ref[...] = s_f
    sb_ref[...] = s_b


def _ret_specs(cfg, rows, row_blk):
    qb, kb = cfg.off_qr // HEAD_DIM, cfg.off_kr // HEAD_DIM
    vb, gb = cfg.off_vr // R_DV, cfg.off_gr // R_DV
    return [pl.BlockSpec((rows, HEAD_DIM), lambda bi, h: (row_blk(bi), qb + h)),
            pl.BlockSpec((rows, HEAD_DIM), lambda bi, h: (row_blk(bi), kb + h)),
            pl.BlockSpec((rows, R_DV), lambda bi, h: (row_blk(bi), vb + h)),
            pl.BlockSpec((rows, R_DV), lambda bi, h: (row_blk(bi), gb + h)),
            pl.BlockSpec((1, R_DV), lambda bi, h: (0, h)),
            pl.BlockSpec((None, 1, HEAD_DIM), lambda bi, h: (h, 0, 0)),
            pl.BlockSpec((None, 1, HEAD_DIM), lambda bi, h: (h, 0, 0))]


def _ret_ctx_call(cfg, p, gn, a_f, a_b):
    b, t, l, rh = cfg.b, cfg.t, cfg.l, cfg.rh
    ctx_row0 = (b * t) // l
    state_spec = pl.BlockSpec((None, None, HEAD_DIM, R_DV), lambda bi, h: (bi, h, 0, 0))
    state_shape = jax.ShapeDtypeStruct((b, rh, HEAD_DIM, R_DV), F32)
    return pl.pallas_call(
        _ret_ctx_kernel,
        grid=(b, rh),
        in_specs=_ret_specs(cfg, l, lambda bi: ctx_row0 + bi),
        out_specs=[pl.BlockSpec((l, R_DV), lambda bi, h: (ctx_row0 + bi, h)), state_spec, state_spec],
        out_shape=[jax.ShapeDtypeStruct((cfg.ntok, cfg.r_v), BF16), state_shape, state_shape],
        scratch_shapes=[pltpu.VMEM((l, R_DV), F32)],
        compiler_params=_cparams(2),
        name="retention_ctx",
    )(p, p, p, p, gn, a_f, a_b)


def _ret_lat_kernel(q_ref, k_ref, v_ref, g_ref, gn_ref, af_ref, ab_ref, cos_ref, sin_ref, sf_ref, sb_ref, _,
                    o_ref, acc_ref, qr_ref, kr_ref):
    scale = HEAD_DIM ** -0.5
    n_chunks = q_ref.shape[0] // BLOCK
    cos, sin = cos_ref[...], sin_ref[...]

    def rot(x):
        return x * cos + pltpu.roll(x, HEAD_DIM // 2, 1) * sin

    qr_ref[...] = rot(q_ref[...].astype(F32))
    kr_ref[...] = rot(k_ref[...].astype(F32)) * scale

    def q_of(i):
        return qr_ref[i * BLOCK:(i + 1) * BLOCK, :]

    def k_of(i):
        return kr_ref[i * BLOCK:(i + 1) * BLOCK, :]

    _retention_scan(q_of, k_of, v_ref, g_ref, gn_ref[...], _log_decay(af_ref[...]), _log_decay(ab_ref[...]),
                    o_ref, acc_ref, n_chunks, sf_ref[...], sb_ref[...])


def _ret_lat_call(cfg, p, gn, a_f, a_b, cos_r, sin_r, s_f, s_b, y_r):
    b, t, rh = cfg.b, cfg.t, cfg.rh
    state_spec = pl.BlockSpec((None, None, HEAD_DIM, R_DV), lambda bi, h: (bi, h, 0, 0))
    table_spec = pl.BlockSpec((t, HEAD_DIM), lambda bi, h: (0, 0))
    in_specs = _ret_specs(cfg, t, lambda bi: bi) + [table_spec, table_spec, state_spec, state_spec,
                                                    pl.BlockSpec(memory_space=pl.ANY)]
    return pl.pallas_call(
        _ret_lat_kernel,
        grid=(b, rh),
        in_specs=in_specs,
        out_specs=pl.BlockSpec((t, R_DV), lambda bi, h: (bi, h)),
        out_shape=jax.ShapeDtypeStruct((cfg.ntok, cfg.r_v), BF16),
        scratch_shapes=[pltpu.VMEM((t, R_DV), F32), pltpu.VMEM((t, HEAD_DIM), F32),
                        pltpu.VMEM((t, HEAD_DIM), F32)],
        input_output_aliases={11: 0},
        compiler_params=_cparams(2),
        name="retention_latent",
    )(p, p, p, p, gn, a_f, a_b, cos_r, sin_r, s_f, s_b, y_r)


def _rotary_tables(t):
    rows = t // GRID_W
    row = jnp.repeat(jnp.arange(rows, dtype=F32), GRID_W)
    col = jnp.tile(jnp.arange(GRID_W, dtype=F32), rows)
    nf = HEAD_DIM // 4
    inv = jnp.power(ROPE_BASE, -jnp.arange(nf, dtype=F32) / nf)
    ang_row, ang_col = row[:, None] * inv, col[:, None] * inv
    cos_a = jnp.concatenate([jnp.cos(ang_row)] * 2 + [jnp.cos(ang_col)] * 2, axis=1)
    sin_a = jnp.concatenate([-jnp.sin(ang_row), jnp.sin(ang_row), -jnp.sin(ang_col), jnp.sin(ang_col)], axis=1)
    nf = HEAD_DIM // 2
    inv = jnp.power(ROPE_BASE, -jnp.arange(nf, dtype=F32) / nf)
    ang = jnp.arange(t, dtype=F32)[:, None] * inv
    cos_r = jnp.concatenate([jnp.cos(ang)] * 2, axis=1)
    sin_r = jnp.concatenate([-jnp.sin(ang), jnp.sin(ang)], axis=1)
    return cos_a, sin_a, cos_r, sin_r


class _Config:
    def __init__(self, x, ctx, w_in, w_gate_d):
        self.b, self.t, self.d = x.shape
        self.l = ctx.shape[1]
        self.ntok = self.b * (self.t + self.l)
        self.ha = self.d // 256
        self.kvh = self.ha // Q_PER_KV
        self.rh = self.d // 256
        self.a_q = self.ha * HEAD_DIM
        self.a_kv = self.kvh * HEAD_DIM
        self.r_qk = self.rh * HEAD_DIM
        self.r_v = self.rh * R_DV
        self.off_ka = 0
        self.off_va = self.a_kv
        self.off_kr = 2 * self.a_kv
        self.off_vr = self.off_kr + self.r_qk
        self.off_qa = self.off_vr + self.r_v
        self.off_qr = self.off_qa + self.a_q
        self.off_gr = self.off_qr + self.r_qk
        self.off_za = self.off_gr + self.r_v
        self.off_zr = self.off_za + self.d
        assert w_in.shape[-1] == self.off_zr + self.d
        self.bm = min(ROW_TILE, self.t)
        self.norm_bm = min(NORM_ROW_TILE, self.t)
        assert self.t % self.bm == 0 and (self.b * self.l) % self.bm == 0
        assert self.t % GRID_W == 0 and self.t % BLOCK == 0 and self.l % BLOCK == 0

    def bidx_for(self, tile):
        assert self.t % tile == 0 and (self.b * self.l) % tile == 0
        tiles_per_batch = self.t // tile
        n_batch = self.b
        return lambda i: jnp.minimum(i // tiles_per_batch, n_batch)


def kernel(x, c, ctx, c_ctx, w_ada, b_ada, g_mix, g_ffn, w_in, attn_sink, ret_a_fwd, ret_a_bwd, ret_gn,
           w_br_attn, w_br_ret, w_out, w_gate_d, w_up_d, w_down_d, w_router, w_gate_e, w_up_e, w_down_e,
           g_final):
    cfg = _Config(x, ctx, w_in, w_gate_d)
    b, t, d, l = cfg.b, cfg.t, cfg.d, cfg.l
    depth = w_ada.shape[0]
    n_lat = b * t

    x_all = jnp.concatenate([x.reshape(n_lat, d), ctx.reshape(b * l, d)], axis=0)
    mod_rows = -(-(b + 1) // 8) * 8
    c_all = jnp.zeros((mod_rows, d), F32).at[:b].set(c).at[b].set(c_ctx)
    mod = _ada_call(c_all, w_ada, b_ada).reshape(depth, mod_rows, 1, 6 * d)

    cos_a, sin_a, cos_r, sin_r = _rotary_tables(t)
    w_in_h, w_br_attn_h, w_br_ret_h, w_out_h = (w.astype(BF16) for w in (w_in, w_br_attn, w_br_ret, w_out))
    w_gate_d_h, w_up_d_h, w_down_d_h = (w.astype(BF16) for w in (w_gate_d, w_up_d, w_down_d))
    w_gate_e_h, w_up_e_h, w_down_e_h = (w.astype(BF16) for w in (w_gate_e, w_up_e, w_down_e))
    w_router_h = jnp.pad(w_router, ((0, 0), (0, 0), (0, LANES - N_EXPERTS))).astype(BF16)
    a_f = jnp.broadcast_to(ret_a_fwd.astype(F32)[:, :, None, None], ret_a_fwd.shape + (1, HEAD_DIM))
    a_b = jnp.broadcast_to(ret_a_bwd.astype(F32)[:, :, None, None], ret_a_bwd.shape + (1, HEAD_DIM))

    for layer in range(depth):
        last = layer == depth - 1
        n_rows = n_lat if last else cfg.ntok
        mod_l = mod[layer]
        gn = ret_gn[layer].reshape(1, cfg.r_v)

        h = _norm_mod_call(cfg, x_all, g_mix[layer], mod_l, 0, cfg.ntok)
        p = _in_proj_call(cfg, h, w_in_h[layer])
        y_a = _attn_lat_call(cfg, p, attn_sink[layer], cos_a, sin_a)
        if not last:
            y_a = _attn_ctx_call(cfg, p, attn_sink[layer], y_a)
        y_r, s_f, s_b = _ret_ctx_call(cfg, p, gn, a_f[layer], a_b[layer])
        y_r = _ret_lat_call(cfg, p, gn, a_f[layer], a_b[layer], cos_r, sin_r, s_f, s_b, y_r)
        m = _merge_call(cfg, y_a, y_r, w_br_attn_h[layer], w_br_ret_h[layer], p, n_rows)
        x_all = _resid_call(cfg, m, w_out_h[layer], x_all, mod_l, 2, n_rows, bn=1024)

        hf = _norm_mod_call(cfg, x_all, g_ffn[layer], mod_l, 3, n_rows)
        i = layer // 2
        if layer % 2 == 0:
            act = _glu_call(cfg, hf, w_gate_d_h[i], w_up_d_h[i], n_rows)
            x_all = _resid_call(cfg, act, w_down_d_h[i], x_all, mod_l, 5, n_rows, bn=512)
        else:
            comb = _router_call(cfg, hf, w_router_h[i], n_rows)
            for e in range(N_EXPERTS):
                act = _glu_call(cfg, hf, w_gate_e_h[i], w_up_e_h[i], n_rows, expert=e)
                x_all = _resid_call(cfg, act, w_down_e_h[i], x_all, mod_l, 5, n_rows, bn=512,
                                    comb=comb, expert=e)

    return _final_norm_call(cfg, x_all, g_final).reshape(b, t, d)
```

```python
import functools

import numpy as np
import jax
import jax.numpy as jnp
from jax import lax
from jax.experimental import pallas as pl
from jax.experimental.pallas import tpu as pltpu

F32 = jnp.float32
BF16 = jnp.bfloat16

EPS = 1e-6
ROPE_BASE = 10000.0
GRID_W = 64
HEAD_DIM = 128
R_DV = 256
BLOCK = 128
Q_PER_KV = 4
N_EXPERTS = 8
LANES = 128

V7X_VMEM_BYTES = 64 * 1024 * 1024
VMEM_LIMIT = V7X_VMEM_BYTES - 8 * 1024 * 1024
ROW_TILE = 1024
NORM_ROW_TILE = 256
SORT_TILE = 512
DISPATCH_TILE = 512
COMBINE_TILE = 256
U32 = jnp.uint32


def _cparams(n_axes):
    return pltpu.CompilerParams(dimension_semantics=("arbitrary",) * n_axes,
                                vmem_limit_bytes=VMEM_LIMIT)


def _dot(a, b):
    return jnp.dot(a, b, preferred_element_type=F32)


def _dot_nt(a, b):
    return lax.dot_general(a, b, (((1,), (1,)), ((), ())), preferred_element_type=F32)


def _dot_tn(a, b):
    return lax.dot_general(a, b, (((0,), (0,)), ((), ())), preferred_element_type=F32)


def _silu(x):
    return x * jax.nn.sigmoid(x)


def _ada_kernel(c_ref, w_ref, b_ref, o_ref):
    s = _silu(c_ref[...]).astype(BF16)
    o_ref[...] = _dot(s, w_ref[...].astype(BF16)) + b_ref[...]


def _ada_call(c_all, w_ada, b_ada):
    depth, d, n6 = w_ada.shape
    r = c_all.shape[0]
    bn = 1024
    return pl.pallas_call(
        _ada_kernel,
        grid=(depth, n6 // bn),
        in_specs=[pl.BlockSpec((r, d), lambda l, j: (0, 0)),
                  pl.BlockSpec((None, d, bn), lambda l, j: (l, 0, j)),
                  pl.BlockSpec((None, 1, bn), lambda l, j: (l, 0, j))],
        out_specs=pl.BlockSpec((None, r, bn), lambda l, j: (l, 0, j)),
        out_shape=jax.ShapeDtypeStruct((depth, r, n6), F32),
        compiler_params=_cparams(2),
        name="ada_mod",
    )(c_all, w_ada, b_ada.reshape(depth, 1, n6))


def _norm_mod_kernel(x_ref, g_ref, sh_ref, sc_ref, o_ref):
    x = x_ref[...]
    y = x * lax.rsqrt(jnp.mean(x * x, axis=-1, keepdims=True) + EPS) * g_ref[...]
    o_ref[...] = (y * (1.0 + sc_ref[...]) + sh_ref[...]).astype(o_ref.dtype)


def _norm_kernel(x_ref, g_ref, o_ref):
    x = x_ref[...]
    o_ref[...] = x * lax.rsqrt(jnp.mean(x * x, axis=-1, keepdims=True) + EPS) * g_ref[...]


def _norm_mod_call(cfg, x_all, g, mod_l, shift_chunk, n_rows):
    d, bm = cfg.d, cfg.norm_bm
    bidx = cfg.bidx_for(bm)
    return pl.pallas_call(
        _norm_mod_kernel,
        grid=(n_rows // bm,),
        in_specs=[pl.BlockSpec((bm, d), lambda i: (i, 0)),
                  pl.BlockSpec((1, d), lambda i: (0, 0)),
                  pl.BlockSpec((None, 1, d), lambda i: (bidx(i), 0, shift_chunk)),
                  pl.BlockSpec((None, 1, d), lambda i: (bidx(i), 0, shift_chunk + 1))],
        out_specs=pl.BlockSpec((bm, d), lambda i: (i, 0)),
        out_shape=jax.ShapeDtypeStruct((n_rows, d), BF16),
        compiler_params=_cparams(1),
        name="norm_mod",
    )(x_all, g.reshape(1, d), mod_l, mod_l)


def _final_norm_call(cfg, x_all, g):
    d, bm = cfg.d, cfg.norm_bm
    n_rows = cfg.b * cfg.t
    return pl.pallas_call(
        _norm_kernel,
        grid=(n_rows // bm,),
        in_specs=[pl.BlockSpec((bm, d), lambda i: (i, 0)),
                  pl.BlockSpec((1, d), lambda i: (0, 0))],
        out_specs=pl.BlockSpec((bm, d), lambda i: (i, 0)),
        out_shape=jax.ShapeDtypeStruct((n_rows, d), F32),
        compiler_params=_cparams(1),
        name="final_norm",
    )(x_all, g.reshape(1, d))


def _mm_kernel(a_ref, w_ref, o_ref):
    o_ref[...] = _dot(a_ref[...], w_ref[...]).astype(o_ref.dtype)


def _in_proj_call(cfg, h, w, layer):
    d, bm = cfg.d, cfg.bm
    n = w.shape[-1]
    bn = 512
    return pl.pallas_call(
        _mm_kernel,
        grid=(cfg.ntok // bm, n // bn),
        in_specs=[pl.BlockSpec((bm, d), lambda i, j: (i, 0)),
                  pl.BlockSpec((None, d, bn), lambda i, j: (layer, 0, j))],
        out_specs=pl.BlockSpec((bm, bn), lambda i, j: (i, j)),
        out_shape=jax.ShapeDtypeStruct((cfg.ntok, n), BF16),
        compiler_params=_cparams(2),
        name="in_proj",
    )(h, w)


def _merge_kernel(ya_ref, yr_ref, yac_ref, yrc_ref, wa_ref, wr_ref, za_ref, zr_ref, o_ref, *, lat_tiles):
    def merge(ya, yr):
        a = _dot(ya[...], wa_ref[...])
        r = _dot(yr[...], wr_ref[...])
        za = za_ref[...].astype(F32)
        zr = zr_ref[...].astype(F32)
        o_ref[...] = (jax.nn.sigmoid(za) * a + jax.nn.sigmoid(zr) * r).astype(o_ref.dtype)

    i = pl.program_id(0)

    @pl.when(i < lat_tiles)
    def _():
        merge(ya_ref, yr_ref)

    @pl.when(i >= lat_tiles)
    def _():
        merge(yac_ref, yrc_ref)


def _merge_call(cfg, y_a, y_r, y_a_ctx, y_r_ctx, w_a, w_r, layer, p, n_rows):
    d, bm = cfg.d, cfg.bm
    bn = 512
    za_blk, zr_blk = cfg.off_za // bn, cfg.off_zr // bn
    lat_tiles = (cfg.b * cfg.t) // bm

    def lat(i, j):
        return (jnp.minimum(i, lat_tiles - 1), 0)

    def ctx(i, j):
        return (jnp.maximum(i - lat_tiles, 0), 0)

    return pl.pallas_call(
        functools.partial(_merge_kernel, lat_tiles=lat_tiles),
        grid=(n_rows // bm, d // bn),
        in_specs=[pl.BlockSpec((bm, cfg.a_q), lat),
                  pl.BlockSpec((bm, cfg.r_v), lat),
                  pl.BlockSpec((bm, cfg.a_q), ctx),
                  pl.BlockSpec((bm, cfg.r_v), ctx),
                  pl.BlockSpec((None, cfg.a_q, bn), lambda i, j: (layer, 0, j)),
                  pl.BlockSpec((None, cfg.r_v, bn), lambda i, j: (layer, 0, j)),
                  pl.BlockSpec((bm, bn), lambda i, j: (i, za_blk + j)),
                  pl.BlockSpec((bm, bn), lambda i, j: (i, zr_blk + j))],
        out_specs=pl.BlockSpec((bm, bn), lambda i, j: (i, j)),
        out_shape=jax.ShapeDtypeStruct((n_rows, d), BF16),
        compiler_params=_cparams(2),
        name="merge",
    )(y_a, y_r, y_a_ctx, y_r_ctx, w_a, w_r, p, p)


def _resid_kernel(a_ref, w_ref, x_ref, gt_ref, o_ref):
    o_ref[...] = x_ref[...] + gt_ref[...] * _dot(a_ref[...], w_ref[...])


def _resid_call(cfg, a, w, widx, x_all, mod_l, gate_chunk, n_rows, bn):
    d, bm = cfg.d, cfg.bm
    k = a.shape[1]
    bidx = cfg.bidx_for(bm)
    gblk = gate_chunk * (d // bn)
    return pl.pallas_call(
        _resid_kernel,
        grid=(n_rows // bm, d // bn),
        in_specs=[pl.BlockSpec((bm, k), lambda i, j: (i, 0)),
                  pl.BlockSpec((None, k, bn), lambda i, j: (widx, 0, j)),
                  pl.BlockSpec((bm, bn), lambda i, j: (i, j)),
                  pl.BlockSpec((None, 1, bn), lambda i, j: (bidx(i), 0, gblk + j))],
        out_specs=pl.BlockSpec((bm, bn), lambda i, j: (i, j)),
        out_shape=jax.ShapeDtypeStruct((cfg.ntok, d), F32),
        input_output_aliases={2: 0},
        compiler_params=_cparams(2),
        name="resid_proj",
    )(a, w, x_all, mod_l)


def _glu_kernel(a_ref, wg_ref, wu_ref, o_ref):
    a = a_ref[...]
    g = _dot(a, wg_ref[...])
    u = _dot(a, wu_ref[...])
    o_ref[...] = (_silu(g) * u).astype(o_ref.dtype)


def _glu_call(cfg, h, wg, wu, widx, n_rows):
    d, bm = cfg.d, cfg.bm
    n = wg.shape[-1]
    bn = 512
    w_spec = pl.BlockSpec((None, d, bn), lambda i, j: (widx, 0, j))
    return pl.pallas_call(
        _glu_kernel,
        grid=(n_rows // bm, n // bn),
        in_specs=[pl.BlockSpec((bm, d), lambda i, j: (i, 0)), w_spec, w_spec],
        out_specs=pl.BlockSpec((bm, bn), lambda i, j: (i, j)),
        out_shape=jax.ShapeDtypeStruct((n_rows, n), BF16),
        compiler_params=_cparams(2),
        name="glu_up",
    )(h, wg, wu)


_R_I1, _R_I2, _R_W1, _R_W2, _R_R1, _R_R2 = range(6)


def _pack_halves(h):
    half = h.shape[1] // 2
    lo = lax.bitcast_convert_type(h[:, :half].astype(F32), U32)
    hi = lax.bitcast_convert_type(h[:, half:].astype(F32), U32)
    return (hi & jnp.uint32(0xFFFF0000)) | lax.shift_right_logical(lo, jnp.uint32(16))


def _unpack_halves(packed):
    lo = lax.bitcast_convert_type(lax.shift_left(packed, jnp.uint32(16)), F32)
    hi = lax.bitcast_convert_type(packed & jnp.uint32(0xFFFF0000), F32)
    return lo.astype(BF16), hi.astype(BF16)


def _route_kernel(x_ref, g_ref, sh_ref, sc_ref, wr_ref, hp_ref, route_ref, cnt_ref, carry_ref):
    @pl.when(pl.program_id(0) == 0)
    def _():
        carry_ref[...] = jnp.zeros_like(carry_ref)

    x = x_ref[...]
    y = x * lax.rsqrt(jnp.mean(x * x, axis=-1, keepdims=True) + EPS) * g_ref[...]
    h = (y * (1.0 + sc_ref[...]) + sh_ref[...]).astype(BF16)
    hp_ref[...] = _pack_halves(h)

    logits = _dot(h, wr_ref[...])
    rows = logits.shape[0]
    lane = lax.broadcasted_iota(jnp.int32, logits.shape, 1).astype(F32)
    neg = -jnp.inf
    lg = jnp.where(lane < N_EXPERTS, logits, neg)
    m1 = jnp.max(lg, axis=-1, keepdims=True)
    i1 = jnp.min(jnp.where(lg == m1, lane, float(LANES)), axis=-1, keepdims=True)
    lg2 = jnp.where(lane == i1, neg, lg)
    m2 = jnp.max(lg2, axis=-1, keepdims=True)
    i2 = jnp.min(jnp.where(lg2 == m2, lane, float(LANES)), axis=-1, keepdims=True)
    e = jnp.exp(m2 - m1)
    w1 = 1.0 / (1.0 + e)
    w2 = e / (1.0 + e)

    chosen = jnp.where((lane == i1) | (lane == i2), 1.0, 0.0)
    rr = lax.broadcasted_iota(jnp.int32, (rows, rows), 0)
    cc = lax.broadcasted_iota(jnp.int32, (rows, rows), 1)
    earlier = jnp.where(cc < rr, 1.0, 0.0).astype(BF16)
    rank = _dot(earlier, chosen.astype(BF16)) + carry_ref[...]
    r1 = jnp.sum(jnp.where(lane == i1, rank, 0.0), axis=-1, keepdims=True)
    r2 = jnp.sum(jnp.where(lane == i2, rank, 0.0), axis=-1, keepdims=True)
    carry_ref[...] += jnp.sum(chosen, axis=0, keepdims=True)
    cnt_ref[...] = carry_ref[...]

    rec = jnp.zeros_like(logits)
    for slot, val in ((_R_I1, i1), (_R_I2, i2), (_R_W1, w1), (_R_W2, w2), (_R_R1, r1), (_R_R2, r2)):
        rec = jnp.where(lane == slot, val, rec)
    route_ref[...] = rec


def _route_call(cfg, x_all, g, mod_l, w_router, i_moe, n_rows):
    d, bm = cfg.d, cfg.norm_bm
    bidx = cfg.bidx_for(bm)
    return pl.pallas_call(
        _route_kernel,
        grid=(n_rows // bm,),
        in_specs=[pl.BlockSpec((bm, d), lambda i: (i, 0)),
                  pl.BlockSpec((1, d), lambda i: (0, 0)),
                  pl.BlockSpec((None, 1, d), lambda i: (bidx(i), 0, 3)),
                  pl.BlockSpec((None, 1, d), lambda i: (bidx(i), 0, 4)),
                  pl.BlockSpec((None, d, LANES), lambda i: (i_moe, 0, 0))],
        out_specs=[pl.BlockSpec((bm, d // 2), lambda i: (i, 0)),
                   pl.BlockSpec((bm, LANES), lambda i: (i, 0)),
                   pl.BlockSpec((1, LANES), lambda i: (0, 0))],
        out_shape=[jax.ShapeDtypeStruct((n_rows, d // 2), U32),
                   jax.ShapeDtypeStruct((n_rows, LANES), F32),
                   jax.ShapeDtypeStruct((1, LANES), F32)],
        scratch_shapes=[pltpu.VMEM((1, LANES), F32)],
        compiler_params=_cparams(1),
        name="route",
    )(x_all, g.reshape(1, d), mod_l, mod_l, w_router)


def _slot(base_ref, e_ref, r_ref, j):
    return base_ref[e_ref[j]] + r_ref[j]


def _dispatch_kernel(base_ref, e1_ref, e2_ref, r1_ref, r2_ref, src_ref, _, dst_ref, sem):
    tile = e1_ref.shape[0]
    tok0 = pl.program_id(0) * tile

    def issue(j, carry):
        src = src_ref.at[pl.ds(tok0 + j, 1)]
        pltpu.make_async_copy(src, dst_ref.at[pl.ds(_slot(base_ref, e1_ref, r1_ref, j), 1)], sem).start()
        pltpu.make_async_copy(src, dst_ref.at[pl.ds(_slot(base_ref, e2_ref, r2_ref, j), 1)], sem).start()
        return carry

    lax.fori_loop(0, tile, issue, 0, unroll=8)

    def drain(j, carry):
        pltpu.make_async_copy(src_ref.at[pl.ds(0, 1)], dst_ref.at[pl.ds(0, 1)], sem).wait()
        pltpu.make_async_copy(src_ref.at[pl.ds(0, 1)], dst_ref.at[pl.ds(0, 1)], sem).wait()
        return carry

    lax.fori_loop(0, tile, drain, 0, unroll=8)


def _dispatch_call(hp, base, e1, e2, r1, r2, n_slots):
    n_rows, half = hp.shape
    tile = DISPATCH_TILE
    smem_tile = pl.BlockSpec((tile,), lambda i: (i,), memory_space=pltpu.SMEM)
    return pl.pallas_call(
        _dispatch_kernel,
        grid=(n_rows // tile,),
        in_specs=[pl.BlockSpec(memory_space=pltpu.SMEM), smem_tile, smem_tile, smem_tile, smem_tile,
                  pl.BlockSpec(memory_space=pl.ANY), pl.BlockSpec(memory_space=pl.ANY)],
        out_specs=pl.BlockSpec(memory_space=pl.ANY),
        out_shape=jax.ShapeDtypeStruct((n_slots, half), U32),
        scratch_shapes=[pltpu.SemaphoreType.DMA(())],
        input_output_aliases={6: 0},
        compiler_params=_cparams(1),
        name="moe_dispatch",
    )(base, e1, e2, r1, r2, hp, jnp.zeros((n_slots, half), U32))


def _glu_sorted_kernel(te_ref, used_ref, a_ref, wg_ref, wu_ref, o_ref):
    @pl.when(pl.program_id(1) < used_ref[0])
    def _():
        lo, hi = _unpack_halves(a_ref[...])
        half = lo.shape[1]
        g = _dot(lo, wg_ref[:half, :]) + _dot(hi, wg_ref[half:, :])
        u = _dot(lo, wu_ref[:half, :]) + _dot(hi, wu_ref[half:, :])
        o_ref[...] = (_silu(g) * u).astype(o_ref.dtype)

    @pl.when(pl.program_id(1) >= used_ref[0])
    def _():
        o_ref[...] = jnp.zeros_like(o_ref)


def _glu_sorted_call(a_sorted, wg, wu, i_moe, tile_expert, n_used):
    n_slots, half = a_sorted.shape
    d = 2 * half
    de = wg.shape[-1]
    bm = SORT_TILE
    bn = de // 2
    w_spec = pl.BlockSpec((None, None, d, bn), lambda j, i, te, nu: (i_moe, te[i], 0, j))
    return pl.pallas_call(
        _glu_sorted_kernel,
        grid_spec=pltpu.PrefetchScalarGridSpec(
            num_scalar_prefetch=2,
            grid=(de // bn, n_slots // bm),
            in_specs=[pl.BlockSpec((bm, half), lambda j, i, te, nu: (i, 0)), w_spec, w_spec],
            out_specs=pl.BlockSpec((bm, bn), lambda j, i, te, nu: (i, j))),
        out_shape=jax.ShapeDtypeStruct((n_slots, de), BF16),
        compiler_params=_cparams(2),
        name="moe_glu",
    )(tile_expert, n_used, a_sorted, wg, wu)


def _down_sorted_kernel(te_ref, used_ref, a_ref, w_ref, o_ref):
    @pl.when(pl.program_id(1) < used_ref[0])
    def _():
        o_ref[...] = _dot(a_ref[...], w_ref[...])

    @pl.when(pl.program_id(1) >= used_ref[0])
    def _():
        o_ref[...] = jnp.zeros_like(o_ref)


def _down_sorted_call(act, wd, i_moe, tile_expert, n_used):
    n_slots, de = act.shape
    d = wd.shape[-1]
    bm = SORT_TILE
    bn = 1024
    return pl.pallas_call(
        _down_sorted_kernel,
        grid_spec=pltpu.PrefetchScalarGridSpec(
            num_scalar_prefetch=2,
            grid=(d // bn, n_slots // bm),
            in_specs=[pl.BlockSpec((bm, de), lambda j, i, te, nu: (i, 0)),
                      pl.BlockSpec((None, None, de, bn), lambda j, i, te, nu: (i_moe, te[i], 0, j))],
            out_specs=pl.BlockSpec((bm, bn), lambda j, i, te, nu: (i, j))),
        out_shape=jax.ShapeDtypeStruct((n_slots, d), F32),
        compiler_params=_cparams(2),
        name="moe_down",
    )(tile_expert, n_used, act, wd)


def _combine_kernel(base_ref, e1c, e2c, r1c, r2c, e1n, e2n, r1n, r2n, y_ref, route_ref, x_ref, gt_ref,
                    o_ref, buf_ref, sem):
    i = pl.program_id(0)
    n = pl.num_programs(0)
    bm = x_ref.shape[0]

    def gather(e1_ref, e2_ref, r1_ref, r2_ref, buf_slot):
        def issue(j, carry):
            pltpu.make_async_copy(y_ref.at[pl.ds(_slot(base_ref, e1_ref, r1_ref, j), 1)],
                                  buf_ref.at[buf_slot, pl.ds(j, 1)], sem.at[buf_slot]).start()
            pltpu.make_async_copy(y_ref.at[pl.ds(_slot(base_ref, e2_ref, r2_ref, j), 1)],
                                  buf_ref.at[buf_slot, pl.ds(bm + j, 1)], sem.at[buf_slot]).start()
            return carry
        lax.fori_loop(0, bm, issue, 0, unroll=8)

    @pl.when(i == 0)
    def _():
        gather(e1c, e2c, r1c, r2c, 0)

    @pl.when(i + 1 < n)
    def _():
        gather(e1n, e2n, r1n, r2n, (i + 1) % 2)

    cur = i % 2

    def drain(j, carry):
        pltpu.make_async_copy(y_ref.at[pl.ds(0, 1)], buf_ref.at[cur, pl.ds(0, 1)], sem.at[cur]).wait()
        pltpu.make_async_copy(y_ref.at[pl.ds(0, 1)], buf_ref.at[cur, pl.ds(0, 1)], sem.at[cur]).wait()
        return carry

    lax.fori_loop(0, bm, drain, 0, unroll=8)

    rec = route_ref[...]
    w1 = rec[:, _R_W1:_R_W1 + 1]
    w2 = rec[:, _R_W2:_R_W2 + 1]
    y = w1 * buf_ref[cur, :bm, :] + w2 * buf_ref[cur, bm:, :]
    o_ref[...] = x_ref[...] + gt_ref[...] * y


def _combine_call(cfg, y_sorted, route, base, e1, e2, r1, r2, x_all, mod_l, n_rows):
    d = cfg.d
    bm = COMBINE_TILE
    bidx = cfg.bidx_for(bm)
    n_tiles = n_rows // bm
    cur = pl.BlockSpec((bm,), lambda i: (i,), memory_space=pltpu.SMEM)
    nxt = pl.BlockSpec((bm,), lambda i: (jnp.minimum(i + 1, n_tiles - 1),), memory_space=pltpu.SMEM)
    return pl.pallas_call(
        _combine_kernel,
        grid=(n_tiles,),
        in_specs=[pl.BlockSpec(memory_space=pltpu.SMEM), cur, cur, cur, cur, nxt, nxt, nxt, nxt,
                  pl.BlockSpec(memory_space=pl.ANY),
                  pl.BlockSpec((bm, LANES), lambda i: (i, 0)),
                  pl.BlockSpec((bm, d), lambda i: (i, 0)),
                  pl.BlockSpec((None, 1, d), lambda i: (bidx(i), 0, 5))],
        out_specs=pl.BlockSpec((bm, d), lambda i: (i, 0)),
        out_shape=jax.ShapeDtypeStruct((cfg.ntok, d), F32),
        scratch_shapes=[pltpu.VMEM((2, 2 * bm, d), F32), pltpu.SemaphoreType.DMA((2,))],
        input_output_aliases={11: 0},
        compiler_params=_cparams(1),
        name="moe_combine",
    )(base, e1, e2, r1, r2, e1, e2, r1, r2, y_sorted, route, x_all, mod_l)


def _moe_ffn(cfg, x_all, g, mod_l, w_router, wg, wu, wd, i_moe, n_rows):
    bm = SORT_TILE
    hp, route, counts = _route_call(cfg, x_all, g, mod_l, w_router, i_moe, n_rows)

    cnt = counts[0, :N_EXPERTS].astype(jnp.int32)
    padded = (cnt + bm - 1) // bm * bm
    ends = jnp.cumsum(padded)
    base = ends - padded
    n_tiles = 2 * n_rows // bm + N_EXPERTS
    tile_start = jnp.arange(n_tiles, dtype=jnp.int32) * bm
    tile_expert = jnp.minimum(jnp.sum((tile_start[:, None] >= ends[None, :]).astype(jnp.int32), axis=1),
                              N_EXPERTS - 1)
    n_used = (ends[-1:] // bm).astype(jnp.int32)
    e1, e2, r1, r2 = (route[:, k].astype(jnp.int32) for k in (_R_I1, _R_I2, _R_R1, _R_R2))

    a_sorted = _dispatch_call(hp, base, e1, e2, r1, r2, n_tiles * bm)
    act = _glu_sorted_call(a_sorted, wg, wu, i_moe, tile_expert, n_used)
    y_sorted = _down_sorted_call(act, wd, i_moe, tile_expert, n_used)
    return _combine_call(cfg, y_sorted, route, base, e1, e2, r1, r2, x_all, mod_l, n_rows)


def _softmax_pv(s, sink_col, v):
    m = jnp.maximum(jnp.max(s, axis=-1, keepdims=True), sink_col)
    p = jnp.exp(s - m)
    denom = jnp.sum(p, axis=-1, keepdims=True) + jnp.exp(sink_col - m)
    return _dot(p.astype(BF16), v) / denom


def _sink_column(sink_ref, group, rows):
    r = lax.broadcasted_iota(jnp.int32, (Q_PER_KV * rows, 1), 0)
    col = jnp.full((Q_PER_KV * rows, 1), sink_ref[group * Q_PER_KV], F32)
    for j in range(1, Q_PER_KV):
        col = jnp.where(r >= j * rows, sink_ref[group * Q_PER_KV + j], col)
    return col


def _attn_lat_kernel(sink_ref, *refs, kvh, t):
    q_refs = refs[:kvh]
    kp_ref, kc_ref, kn_ref, vp_ref, vc_ref, vn_ref, kx_ref, vx_ref, cos_ref, sin_ref, o_ref = refs[kvh:]
    n = pl.program_id(1)
    nb = t // BLOCK
    lane = lax.broadcasted_iota(jnp.int32, (BLOCK, HEAD_DIM), 1)
    first_half = (lane & (HEAD_DIM // 4)) == 0

    def rot(x, blk):
        start = pl.multiple_of(blk * BLOCK, BLOCK)
        c = cos_ref[pl.ds(start, BLOCK), :]
        s = sin_ref[pl.ds(start, BLOCK), :]
        partner = jnp.where(first_half, pltpu.roll(x, HEAD_DIM - HEAD_DIM // 4, 1),
                            pltpu.roll(x, HEAD_DIM // 4, 1))
        return x * c + partner * s

    blk_p = jnp.maximum(n - 1, 0)
    blk_n = jnp.minimum(n + 1, nb - 1)
    n_loc = 3 * BLOCK
    n_keys = n_loc + kx_ref.shape[0]
    rows = Q_PER_KV * BLOCK
    qi = lax.broadcasted_iota(jnp.int32, (rows, n_keys), 0) & (BLOCK - 1)
    kj = lax.broadcasted_iota(jnp.int32, (rows, n_keys), 1)
    rel = kj - qi
    s_pos = n * BLOCK - BLOCK + kj
    valid = (kj >= n_loc) | ((rel >= 0) & (rel <= 2 * BLOCK) & (s_pos >= 0) & (s_pos < t))
    scale = HEAD_DIM ** -0.5

    for g in range(kvh):
        hs = slice(g * HEAD_DIM, (g + 1) * HEAD_DIM)
        q = jnp.concatenate(
            [rot(q_refs[g][:, j * HEAD_DIM:(j + 1) * HEAD_DIM].astype(F32), n) * scale
             for j in range(Q_PER_KV)], axis=0).astype(BF16)
        k = jnp.concatenate(
            [rot(kp_ref[:, hs].astype(F32), blk_p).astype(BF16),
             rot(kc_ref[:, hs].astype(F32), n).astype(BF16),
             rot(kn_ref[:, hs].astype(F32), blk_n).astype(BF16),
             kx_ref[:, hs]], axis=0)
        v = jnp.concatenate([vp_ref[:, hs], vc_ref[:, hs], vn_ref[:, hs], vx_ref[:, hs]], axis=0)
        s = jnp.where(valid, _dot_nt(q, k), jnp.finfo(F32).min)
        o = _softmax_pv(s, _sink_column(sink_ref, g, BLOCK), v)
        for j in range(Q_PER_KV):
            h = g * Q_PER_KV + j
            o_ref[:, h * HEAD_DIM:(h + 1) * HEAD_DIM] = o[j * BLOCK:(j + 1) * BLOCK].astype(o_ref.dtype)


def _attn_lat_call(cfg, p, sink, cos_a, sin_a):
    b, t, l = cfg.b, cfg.t, cfg.l
    nb = t // BLOCK
    kvh = cfg.kvh
    gw = Q_PER_KV * HEAD_DIM
    kvw = kvh * HEAD_DIM
    q_blk = cfg.off_qa // gw
    k_blk, v_blk = cfg.off_ka // kvw, cfg.off_va // kvw
    ctx_row0 = (b * t) // l

    def kv_spec(col_blk, shift):
        def imap(bi, n):
            return (bi * nb + jnp.clip(n + shift, 0, nb - 1), col_blk)
        return pl.BlockSpec((BLOCK, kvw), imap)

    in_specs = [pl.BlockSpec(memory_space=pltpu.SMEM)]
    in_specs += [pl.BlockSpec((BLOCK, gw), functools.partial(lambda bi, n, g: (bi * nb + n, q_blk + g), g=g))
                 for g in range(kvh)]
    in_specs += [kv_spec(k_blk, -1), kv_spec(k_blk, 0), kv_spec(k_blk, 1),
                 kv_spec(v_blk, -1), kv_spec(v_blk, 0), kv_spec(v_blk, 1),
                 pl.BlockSpec((l, kvw), lambda bi, n: (ctx_row0 + bi, k_blk)),
                 pl.BlockSpec((l, kvw), lambda bi, n: (ctx_row0 + bi, v_blk)),
                 pl.BlockSpec((t, HEAD_DIM), lambda bi, n: (0, 0)),
                 pl.BlockSpec((t, HEAD_DIM), lambda bi, n: (0, 0))]
    return pl.pallas_call(
        functools.partial(_attn_lat_kernel, kvh=kvh, t=t),
        grid=(b, nb),
        in_specs=in_specs,
        out_specs=pl.BlockSpec((BLOCK, cfg.a_q), lambda bi, n: (bi * nb + n, 0)),
        out_shape=jax.ShapeDtypeStruct((b * t, cfg.a_q), BF16),
        compiler_params=_cparams(2),
        name="attn_latent",
    )(sink, *([p] * (kvh + 8)), cos_a, sin_a)


def _attn_ctx_kernel(sink_ref, *refs, kvh):
    q_refs = refs[:kvh]
    kx_ref, vx_ref, o_ref = refs[kvh:]
    l = kx_ref.shape[0]
    scale = HEAD_DIM ** -0.5
    for g in range(kvh):
        hs = slice(g * HEAD_DIM, (g + 1) * HEAD_DIM)
        q = jnp.concatenate(
            [(q_refs[g][:, j * HEAD_DIM:(j + 1) * HEAD_DIM].astype(F32) * scale).astype(BF16)
             for j in range(Q_PER_KV)], axis=0)
        s = _dot_nt(q, kx_ref[:, hs])
        o = _softmax_pv(s, _sink_column(sink_ref, g, l), vx_ref[:, hs])
        for j in range(Q_PER_KV):
            h = g * Q_PER_KV + j
            o_ref[:, h * HEAD_DIM:(h + 1) * HEAD_DIM] = o[j * l:(j + 1) * l].astype(o_ref.dtype)


def _attn_ctx_call(cfg, p, sink):
    b, t, l = cfg.b, cfg.t, cfg.l
    kvh = cfg.kvh
    gw = Q_PER_KV * HEAD_DIM
    kvw = kvh * HEAD_DIM
    q_blk = cfg.off_qa // gw
    k_blk, v_blk = cfg.off_ka // kvw, cfg.off_va // kvw
    ctx_row0 = (b * t) // l
    in_specs = [pl.BlockSpec(memory_space=pltpu.SMEM)]
    in_specs += [pl.BlockSpec((l, gw), functools.partial(lambda bi, g: (ctx_row0 + bi, q_blk + g), g=g))
                 for g in range(kvh)]
    in_specs += [pl.BlockSpec((l, kvw), lambda bi: (ctx_row0 + bi, k_blk)),
                 pl.BlockSpec((l, kvw), lambda bi: (ctx_row0 + bi, v_blk))]
    return pl.pallas_call(
        functools.partial(_attn_ctx_kernel, kvh=kvh),
        grid=(b,),
        in_specs=in_specs,
        out_specs=pl.BlockSpec((l, cfg.a_q), lambda bi: (bi, 0)),
        out_shape=jax.ShapeDtypeStruct((b * l, cfg.a_q), BF16),
        compiler_params=_cparams(1),
        name="attn_ctx",
    )(sink, *([p] * (kvh + 2)))


def _retention_scan(q_of, k_of, v_ref, g_ref, gn, lg_f, lg_b, o_ref, acc_ref, n_chunks, s_f, s_b):
    c = BLOCK
    ii = lax.broadcasted_iota(jnp.int32, (c, c), 0).astype(F32)
    jj = lax.broadcasted_iota(jnp.int32, (c, c), 1).astype(F32)
    diff = ii - jj
    decay = jnp.where(diff > 0, jnp.exp(lg_f * jnp.maximum(diff, 0.0)),
                      jnp.where(diff < 0, jnp.exp(lg_b * jnp.maximum(-diff, 0.0)), 2.0))
    zeta_f = jnp.exp(lg_f * (c - 1 - ii))
    xi_f = jnp.exp(lg_f * (ii + 1))
    zeta_b = jnp.exp(lg_b * ii)
    xi_b = jnp.exp(lg_b * (c - ii))
    cd_f = jnp.exp(jnp.concatenate([lg_f, lg_f], axis=1) * c)
    cd_b = jnp.exp(jnp.concatenate([lg_b, lg_b], axis=1) * c)

    for i in reversed(range(n_chunks)):
        rows = slice(i * c, (i + 1) * c)
        q, k, v = q_of(i), k_of(i), v_ref[rows, :]
        acc_ref[rows, :] = _dot((q * xi_b).astype(BF16), s_b.astype(BF16))
        s_b = cd_b * s_b + _dot_tn((k * zeta_b).astype(BF16), v)

    for i in range(n_chunks):
        rows = slice(i * c, (i + 1) * c)
        q, k, v = q_of(i), k_of(i), v_ref[rows, :]
        scores = _dot_nt(q.astype(BF16), k.astype(BF16)) * decay
        o = (acc_ref[rows, :] + _dot(scores.astype(BF16), v)
             + _dot((q * xi_f).astype(BF16), s_f.astype(BF16)))
        s_f = cd_f * s_f + _dot_tn((k * zeta_f).astype(BF16), v)
        mu = jnp.mean(o, axis=-1, keepdims=True)
        dev = o - mu
        var = jnp.mean(dev * dev, axis=-1, keepdims=True)
        on = dev * lax.rsqrt(var + EPS) * gn
        gate = g_ref[rows, :].astype(F32)
        o_ref[rows, :] = (_silu(gate) * on).astype(o_ref.dtype)
    return s_f, s_b


def _log_decay(a_row):
    return jnp.log1p(-jnp.exp2(-a_row))


def _ret_ctx_kernel(q_ref, k_ref, v_ref, g_ref, gn_ref, af_ref, ab_ref, o_ref, sf_ref, sb_ref, acc_ref):
    scale = HEAD_DIM ** -0.5
    n_chunks = q_ref.shape[0] // BLOCK

    def q_of(i):
        return q_ref[i * BLOCK:(i + 1) * BLOCK, :].astype(F32)

    def k_of(i):
        return k_ref[i * BLOCK:(i + 1) * BLOCK, :].astype(F32) * scale

    zero = jnp.zeros((HEAD_DIM, R_DV), F32)
    s_f, s_b = _retention_scan(q_of, k_of, v_ref, g_ref, gn_ref[...], _log_decay(af_ref[...]),
                               _log_decay(ab_ref[...]), o_ref, acc_ref, n_chunks, zero, zero)
    sf_ref[...] = s_f
    sb_ref[...] = s_b


def _ret_specs(cfg, rows, row_blk):
    qb, kb = cfg.off_qr // HEAD_DIM, cfg.off_kr // HEAD_DIM
    vb, gb = cfg.off_vr // R_DV, cfg.off_gr // R_DV
    return [pl.BlockSpec((rows, HEAD_DIM), lambda bi, h: (row_blk(bi), qb + h)),
            pl.BlockSpec((rows, HEAD_DIM), lambda bi, h: (row_blk(bi), kb + h)),
            pl.BlockSpec((rows, R_DV), lambda bi, h: (row_blk(bi), vb + h)),
            pl.BlockSpec((rows, R_DV), lambda bi, h: (row_blk(bi), gb + h)),
            pl.BlockSpec((1, R_DV), lambda bi, h: (0, h)),
            pl.BlockSpec((None, 1, HEAD_DIM), lambda bi, h: (h, 0, 0)),
            pl.BlockSpec((None, 1, HEAD_DIM), lambda bi, h: (h, 0, 0))]


def _ret_ctx_call(cfg, p, gn, a_f, a_b):
    b, t, l, rh = cfg.b, cfg.t, cfg.l, cfg.rh
    ctx_row0 = (b * t) // l
    state_spec = pl.BlockSpec((None, None, HEAD_DIM, R_DV), lambda bi, h: (bi, h, 0, 0))
    state_shape = jax.ShapeDtypeStruct((b, rh, HEAD_DIM, R_DV), F32)
    return pl.pallas_call(
        _ret_ctx_kernel,
        grid=(b, rh),
        in_specs=_ret_specs(cfg, l, lambda bi: ctx_row0 + bi),
        out_specs=[pl.BlockSpec((l, R_DV), lambda bi, h: (bi, h)), state_spec, state_spec],
        out_shape=[jax.ShapeDtypeStruct((b * l, cfg.r_v), BF16), state_shape, state_shape],
        scratch_shapes=[pltpu.VMEM((l, R_DV), F32)],
        compiler_params=_cparams(2),
        name="retention_ctx",
    )(p, p, p, p, gn, a_f, a_b)


def _ret_lat_kernel(q_ref, k_ref, v_ref, g_ref, gn_ref, af_ref, ab_ref, cos_ref, sin_ref, sf_ref, sb_ref,
                    o_ref, acc_ref, qr_ref, kr_ref):
    scale = HEAD_DIM ** -0.5
    n_chunks = q_ref.shape[0] // BLOCK
    cos, sin = cos_ref[...], sin_ref[...]

    def rot(x):
        return x * cos + pltpu.roll(x, HEAD_DIM // 2, 1) * sin

    qr_ref[...] = rot(q_ref[...].astype(F32))
    kr_ref[...] = rot(k_ref[...].astype(F32)) * scale

    def q_of(i):
        return qr_ref[i * BLOCK:(i + 1) * BLOCK, :]

    def k_of(i):
        return kr_ref[i * BLOCK:(i + 1) * BLOCK, :]

    _retention_scan(q_of, k_of, v_ref, g_ref, gn_ref[...], _log_decay(af_ref[...]), _log_decay(ab_ref[...]),
                    o_ref, acc_ref, n_chunks, sf_ref[...], sb_ref[...])


def _ret_lat_call(cfg, p, gn, a_f, a_b, cos_r, sin_r, s_f, s_b):
    b, t, rh = cfg.b, cfg.t, cfg.rh
    state_spec = pl.BlockSpec((None, None, HEAD_DIM, R_DV), lambda bi, h: (bi, h, 0, 0))
    table_spec = pl.BlockSpec((t, HEAD_DIM), lambda bi, h: (0, 0))
    in_specs = _ret_specs(cfg, t, lambda bi: bi) + [table_spec, table_spec, state_spec, state_spec]
    return pl.pallas_call(
        _ret_lat_kernel,
        grid=(b, rh),
        in_specs=in_specs,
        out_specs=pl.BlockSpec((t, R_DV), lambda bi, h: (bi, h)),
        out_shape=jax.ShapeDtypeStruct((b * t, cfg.r_v), BF16),
        scratch_shapes=[pltpu.VMEM((t, R_DV), F32), pltpu.VMEM((t, HEAD_DIM), F32),
                        pltpu.VMEM((t, HEAD_DIM), F32)],
        compiler_params=_cparams(2),
        name="retention_latent",
    )(p, p, p, p, gn, a_f, a_b, cos_r, sin_r, s_f, s_b)


def _rotary_tables(t):
    rows = t // GRID_W
    row = jnp.repeat(jnp.arange(rows, dtype=F32), GRID_W)
    col = jnp.tile(jnp.arange(GRID_W, dtype=F32), rows)
    nf = HEAD_DIM // 4
    inv = jnp.power(ROPE_BASE, -jnp.arange(nf, dtype=F32) / nf)
    ang_row, ang_col = row[:, None] * inv, col[:, None] * inv
    cos_a = jnp.concatenate([jnp.cos(ang_row)] * 2 + [jnp.cos(ang_col)] * 2, axis=1)
    sin_a = jnp.concatenate([-jnp.sin(ang_row), jnp.sin(ang_row), -jnp.sin(ang_col), jnp.sin(ang_col)], axis=1)
    nf = HEAD_DIM // 2
    inv = jnp.power(ROPE_BASE, -jnp.arange(nf, dtype=F32) / nf)
    ang = jnp.arange(t, dtype=F32)[:, None] * inv
    cos_r = jnp.concatenate([jnp.cos(ang)] * 2, axis=1)
    sin_r = jnp.concatenate([-jnp.sin(ang), jnp.sin(ang)], axis=1)
    return cos_a, sin_a, cos_r, sin_r


class _Config:
    def __init__(self, x, ctx, w_in, w_gate_d):
        self.b, self.t, self.d = x.shape
        self.l = ctx.shape[1]
        self.ntok = self.b * (self.t + self.l)
        self.ha = self.d // 256
        self.kvh = self.ha // Q_PER_KV
        self.rh = self.d // 256
        self.a_q = self.ha * HEAD_DIM
        self.a_kv = self.kvh * HEAD_DIM
        self.r_qk = self.rh * HEAD_DIM
        self.r_v = self.rh * R_DV
        self.off_ka = 0
        self.off_va = self.a_kv
        self.off_kr = 2 * self.a_kv
        self.off_vr = self.off_kr + self.r_qk
        self.off_qa = self.off_vr + self.r_v
        self.off_qr = self.off_qa + self.a_q
        self.off_gr = self.off_qr + self.r_qk
        self.off_za = self.off_gr + self.r_v
        self.off_zr = self.off_za + self.d
        assert w_in.shape[-1] == self.off_zr + self.d
        self.bm = min(ROW_TILE, self.t)
        self.norm_bm = min(NORM_ROW_TILE, self.t)
        assert self.t % self.bm == 0 and (self.b * self.l) % self.bm == 0
        assert self.t % GRID_W == 0 and self.t % BLOCK == 0 and self.l % BLOCK == 0

    def bidx_for(self, tile):
        assert self.t % tile == 0 and (self.b * self.l) % tile == 0
        tiles_per_batch = self.t // tile
        n_batch = self.b
        return lambda i: jnp.minimum(i // tiles_per_batch, n_batch)


def kernel(x, c, ctx, c_ctx, w_ada, b_ada, g_mix, g_ffn, w_in, attn_sink, ret_a_fwd, ret_a_bwd, ret_gn,
           w_br_attn, w_br_ret, w_out, w_gate_d, w_up_d, w_down_d, w_router, w_gate_e, w_up_e, w_down_e,
           g_final):
    cfg = _Config(x, ctx, w_in, w_gate_d)
    b, t, d, l = cfg.b, cfg.t, cfg.d, cfg.l
    depth = w_ada.shape[0]
    n_lat = b * t

    x_all = jnp.concatenate([x.reshape(n_lat, d), ctx.reshape(b * l, d)], axis=0)
    mod_rows = -(-(b + 1) // 8) * 8
    c_all = jnp.zeros((mod_rows, d), F32).at[:b].set(c).at[b].set(c_ctx)
    mod = _ada_call(c_all, w_ada, b_ada).reshape(depth, mod_rows, 1, 6 * d)

    cos_a, sin_a, cos_r, sin_r = _rotary_tables(t)
    w_in_h, w_br_attn_h, w_br_ret_h, w_out_h = (w.astype(BF16) for w in (w_in, w_br_attn, w_br_ret, w_out))
    w_gate_d_h, w_up_d_h, w_down_d_h = (w.astype(BF16) for w in (w_gate_d, w_up_d, w_down_d))
    w_gate_e_h, w_up_e_h, w_down_e_h = (w.astype(BF16) for w in (w_gate_e, w_up_e, w_down_e))
    w_router_h = jnp.pad(w_router, ((0, 0), (0, 0), (0, LANES - N_EXPERTS))).astype(BF16)
    a_f = jnp.broadcast_to(ret_a_fwd.astype(F32)[:, :, None, None], ret_a_fwd.shape + (1, HEAD_DIM))
    a_b = jnp.broadcast_to(ret_a_bwd.astype(F32)[:, :, None, None], ret_a_bwd.shape + (1, HEAD_DIM))

    for layer in range(depth):
        last = layer == depth - 1
        n_rows = n_lat if last else cfg.ntok
        mod_l = mod[layer]
        gn = ret_gn[layer].reshape(1, cfg.r_v)

        h = _norm_mod_call(cfg, x_all, g_mix[layer], mod_l, 0, cfg.ntok)
        p = _in_proj_call(cfg, h, w_in_h, layer)
        y_a = _attn_lat_call(cfg, p, attn_sink[layer], cos_a, sin_a)
        y_r_ctx, s_f, s_b = _ret_ctx_call(cfg, p, gn, a_f[layer], a_b[layer])
        y_r = _ret_lat_call(cfg, p, gn, a_f[layer], a_b[layer], cos_r, sin_r, s_f, s_b)
        if last:
            y_a_ctx, y_r_ctx = y_a, y_r
        else:
            y_a_ctx = _attn_ctx_call(cfg, p, attn_sink[layer])
        m = _merge_call(cfg, y_a, y_r, y_a_ctx, y_r_ctx, w_br_attn_h, w_br_ret_h, layer, p, n_rows)
        x_all = _resid_call(cfg, m, w_out_h, layer, x_all, mod_l, 2, n_rows, bn=1024)

        i = layer // 2
        if layer % 2 == 0:
            hf = _norm_mod_call(cfg, x_all, g_ffn[layer], mod_l, 3, n_rows)
            act = _glu_call(cfg, hf, w_gate_d_h, w_up_d_h, i, n_rows)
            x_all = _resid_call(cfg, act, w_down_d_h, i, x_all, mod_l, 5, n_rows, bn=512)
        else:
            x_all = _moe_ffn(cfg, x_all, g_ffn[layer], mod_l, w_router_h, w_gate_e_h, w_up_e_h, w_down_e_h,
                             i, n_rows)

    return _final_norm_call(cfg, x_all, g_final).reshape(b, t, d)
```

```python
import functools

import numpy as np
import jax
import jax.numpy as jnp
from jax import lax
from jax.experimental import pallas as pl
from jax.experimental.pallas import tpu as pltpu

F32 = jnp.float32
BF16 = jnp.bfloat16

EPS = 1e-6
ROPE_BASE = 10000.0
GRID_W = 64
HEAD_DIM = 128
R_DV = 256
BLOCK = 128
Q_PER_KV = 4
N_EXPERTS = 8
LANES = 128

V7X_VMEM_BYTES = 64 * 1024 * 1024
VMEM_LIMIT = V7X_VMEM_BYTES - 8 * 1024 * 1024
ROW_TILE = 1024
NORM_ROW_TILE = 512
SORT_TILE = 512
DISPATCH_TILE = 512
COMBINE_TILE = 256
U32 = jnp.uint32


def _cparams(n_axes):
    return pltpu.CompilerParams(dimension_semantics=("arbitrary",) * n_axes,
                                vmem_limit_bytes=VMEM_LIMIT)


def _dot(a, b):
    return jnp.dot(a, b, preferred_element_type=F32)


def _dot_nt(a, b):
    return lax.dot_general(a, b, (((1,), (1,)), ((), ())), preferred_element_type=F32)


def _dot_tn(a, b):
    return lax.dot_general(a, b, (((0,), (0,)), ((), ())), preferred_element_type=F32)


def _silu(x):
    return x * jax.nn.sigmoid(x)


def _ada_kernel(c_ref, w_ref, b_ref, o_ref):
    s = _silu(c_ref[...]).astype(BF16)
    o_ref[...] = _dot(s, w_ref[...].astype(BF16)) + b_ref[...]


def _ada_call(c_all, w_ada, b_ada):
    depth, d, n6 = w_ada.shape
    r = c_all.shape[0]
    bn = 1024
    return pl.pallas_call(
        _ada_kernel,
        grid=(depth, n6 // bn),
        in_specs=[pl.BlockSpec((r, d), lambda l, j: (0, 0)),
                  pl.BlockSpec((None, d, bn), lambda l, j: (l, 0, j)),
                  pl.BlockSpec((None, 1, bn), lambda l, j: (l, 0, j))],
        out_specs=pl.BlockSpec((None, r, bn), lambda l, j: (l, 0, j)),
        out_shape=jax.ShapeDtypeStruct((depth, r, n6), F32),
        compiler_params=_cparams(2),
        name="ada_mod",
    )(c_all, w_ada, b_ada.reshape(depth, 1, n6))


def _norm_mod_kernel(x_ref, g_ref, sh_ref, sc_ref, o_ref):
    x = x_ref[...]
    y = x * lax.rsqrt(jnp.mean(x * x, axis=-1, keepdims=True) + EPS) * g_ref[...]
    o_ref[...] = (y * (1.0 + sc_ref[...]) + sh_ref[...]).astype(o_ref.dtype)


def _norm_kernel(x_ref, g_ref, o_ref):
    x = x_ref[...]
    o_ref[...] = x * lax.rsqrt(jnp.mean(x * x, axis=-1, keepdims=True) + EPS) * g_ref[...]


def _norm_mod_call(cfg, x_all, g, mod_l, shift_chunk, n_rows):
    d, bm = cfg.d, cfg.norm_bm
    bidx = cfg.bidx_for(bm)
    return pl.pallas_call(
        _norm_mod_kernel,
        grid=(n_rows // bm,),
        in_specs=[pl.BlockSpec((bm, d), lambda i: (i, 0)),
                  pl.BlockSpec((1, d), lambda i: (0, 0)),
                  pl.BlockSpec((None, 1, d), lambda i: (bidx(i), 0, shift_chunk)),
                  pl.BlockSpec((None, 1, d), lambda i: (bidx(i), 0, shift_chunk + 1))],
        out_specs=pl.BlockSpec((bm, d), lambda i: (i, 0)),
        out_shape=jax.ShapeDtypeStruct((n_rows, d), BF16),
        compiler_params=_cparams(1),
        name="norm_mod",
    )(x_all, g.reshape(1, d), mod_l, mod_l)


def _final_norm_call(cfg, x_all, g):
    d, bm = cfg.d, cfg.norm_bm
    n_rows = cfg.b * cfg.t
    return pl.pallas_call(
        _norm_kernel,
        grid=(n_rows // bm,),
        in_specs=[pl.BlockSpec((bm, d), lambda i: (i, 0)),
                  pl.BlockSpec((1, d), lambda i: (0, 0))],
        out_specs=pl.BlockSpec((bm, d), lambda i: (i, 0)),
        out_shape=jax.ShapeDtypeStruct((n_rows, d), F32),
        compiler_params=_cparams(1),
        name="final_norm",
    )(x_all, g.reshape(1, d))


def _mm_kernel(a_ref, w_ref, o_ref):
    o_ref[...] = _dot(a_ref[...], w_ref[...]).astype(o_ref.dtype)


def _wide_tile(cfg, n_rows):
    return 2 * cfg.bm if n_rows % (2 * cfg.bm) == 0 else cfg.bm


def _in_proj_call(cfg, h, w, layer):
    d, bm = cfg.d, _wide_tile(cfg, cfg.ntok)
    n = w.shape[-1]
    bn = 512
    return pl.pallas_call(
        _mm_kernel,
        grid=(cfg.ntok // bm, n // bn),
        in_specs=[pl.BlockSpec((bm, d), lambda i, j: (i, 0)),
                  pl.BlockSpec((None, d, bn), lambda i, j: (layer, 0, j))],
        out_specs=pl.BlockSpec((bm, bn), lambda i, j: (i, j)),
        out_shape=jax.ShapeDtypeStruct((cfg.ntok, n), BF16),
        compiler_params=_cparams(2),
        name="in_proj",
    )(h, w)


def _merge_kernel(ya_ref, yr_ref, yac_ref, yrc_ref, wa_ref, wr_ref, za_ref, zr_ref, o_ref, *, lat_tiles):
    def merge(ya, yr):
        a = _dot(ya[...], wa_ref[...])
        r = _dot(yr[...], wr_ref[...])
        za = za_ref[...].astype(F32)
        zr = zr_ref[...].astype(F32)
        o_ref[...] = (jax.nn.sigmoid(za) * a + jax.nn.sigmoid(zr) * r).astype(o_ref.dtype)

    i = pl.program_id(0)

    @pl.when(i < lat_tiles)
    def _():
        merge(ya_ref, yr_ref)

    @pl.when(i >= lat_tiles)
    def _():
        merge(yac_ref, yrc_ref)


def _merge_call(cfg, y_a, y_r, y_a_ctx, y_r_ctx, w_a, w_r, layer, p, n_rows):
    d, bm = cfg.d, cfg.bm
    bn = 512
    za_blk, zr_blk = cfg.off_za // bn, cfg.off_zr // bn
    lat_tiles = (cfg.b * cfg.t) // bm

    def lat(i, j):
        return (jnp.minimum(i, lat_tiles - 1), 0)

    def ctx(i, j):
        return (jnp.maximum(i - lat_tiles, 0), 0)

    return pl.pallas_call(
        functools.partial(_merge_kernel, lat_tiles=lat_tiles),
        grid=(n_rows // bm, d // bn),
        in_specs=[pl.BlockSpec((bm, cfg.a_q), lat),
                  pl.BlockSpec((bm, cfg.r_v), lat),
                  pl.BlockSpec((bm, cfg.a_q), ctx),
                  pl.BlockSpec((bm, cfg.r_v), ctx),
                  pl.BlockSpec((None, cfg.a_q, bn), lambda i, j: (layer, 0, j)),
                  pl.BlockSpec((None, cfg.r_v, bn), lambda i, j: (layer, 0, j)),
                  pl.BlockSpec((bm, bn), lambda i, j: (i, za_blk + j)),
                  pl.BlockSpec((bm, bn), lambda i, j: (i, zr_blk + j))],
        out_specs=pl.BlockSpec((bm, bn), lambda i, j: (i, j)),
        out_shape=jax.ShapeDtypeStruct((n_rows, d), BF16),
        compiler_params=_cparams(2),
        name="merge",
    )(y_a, y_r, y_a_ctx, y_r_ctx, w_a, w_r, p, p)


def _resid_kernel(a_ref, w_ref, x_ref, gt_ref, o_ref):
    o_ref[...] = x_ref[...] + gt_ref[...] * _dot(a_ref[...], w_ref[...])


def _resid_call(cfg, a, w, widx, x_all, mod_l, gate_chunk, n_rows, bn):
    d, bm = cfg.d, cfg.bm
    k = a.shape[1]
    bidx = cfg.bidx_for(bm)
    gblk = gate_chunk * (d // bn)
    return pl.pallas_call(
        _resid_kernel,
        grid=(n_rows // bm, d // bn),
        in_specs=[pl.BlockSpec((bm, k), lambda i, j: (i, 0)),
                  pl.BlockSpec((None, k, bn), lambda i, j: (widx, 0, j)),
                  pl.BlockSpec((bm, bn), lambda i, j: (i, j)),
                  pl.BlockSpec((None, 1, bn), lambda i, j: (bidx(i), 0, gblk + j))],
        out_specs=pl.BlockSpec((bm, bn), lambda i, j: (i, j)),
        out_shape=jax.ShapeDtypeStruct((cfg.ntok, d), F32),
        input_output_aliases={2: 0},
        compiler_params=_cparams(2),
        name="resid_proj",
    )(a, w, x_all, mod_l)


def _glu_kernel(a_ref, wg_ref, wu_ref, o_ref):
    a = a_ref[...]
    g = _dot(a, wg_ref[...])
    u = _dot(a, wu_ref[...])
    o_ref[...] = (_silu(g) * u).astype(o_ref.dtype)


def _glu_call(cfg, h, wg, wu, widx, n_rows):
    d, bm = cfg.d, _wide_tile(cfg, n_rows)
    n = wg.shape[-1]
    bn = 512
    w_spec = pl.BlockSpec((None, d, bn), lambda i, j: (widx, 0, j))
    return pl.pallas_call(
        _glu_kernel,
        grid=(n_rows // bm, n // bn),
        in_specs=[pl.BlockSpec((bm, d), lambda i, j: (i, 0)), w_spec, w_spec],
        out_specs=pl.BlockSpec((bm, bn), lambda i, j: (i, j)),
        out_shape=jax.ShapeDtypeStruct((n_rows, n), BF16),
        compiler_params=_cparams(2),
        name="glu_up",
    )(h, wg, wu)


_R_I1, _R_I2, _R_W1, _R_W2, _R_R1, _R_R2 = range(6)


def _pack_halves(h):
    half = h.shape[1] // 2
    lo = lax.bitcast_convert_type(h[:, :half].astype(F32), U32)
    hi = lax.bitcast_convert_type(h[:, half:].astype(F32), U32)
    return (hi & jnp.uint32(0xFFFF0000)) | lax.shift_right_logical(lo, jnp.uint32(16))


def _unpack_halves(packed):
    lo = lax.bitcast_convert_type(lax.shift_left(packed, jnp.uint32(16)), F32)
    hi = lax.bitcast_convert_type(packed & jnp.uint32(0xFFFF0000), F32)
    return lo.astype(BF16), hi.astype(BF16)


def _route_kernel(x_ref, g_ref, sh_ref, sc_ref, wr_ref, hp_ref, route_ref, cnt_ref, carry_ref):
    @pl.when(pl.program_id(0) == 0)
    def _():
        carry_ref[...] = jnp.zeros_like(carry_ref)

    x = x_ref[...]
    y = x * lax.rsqrt(jnp.mean(x * x, axis=-1, keepdims=True) + EPS) * g_ref[...]
    h = (y * (1.0 + sc_ref[...]) + sh_ref[...]).astype(BF16)
    hp_ref[...] = _pack_halves(h)

    logits = _dot(h, wr_ref[...])
    rows = logits.shape[0]
    lane = lax.broadcasted_iota(jnp.int32, logits.shape, 1).astype(F32)
    neg = -jnp.inf
    lg = jnp.where(lane < N_EXPERTS, logits, neg)
    m1 = jnp.max(lg, axis=-1, keepdims=True)
    i1 = jnp.min(jnp.where(lg == m1, lane, float(LANES)), axis=-1, keepdims=True)
    lg2 = jnp.where(lane == i1, neg, lg)
    m2 = jnp.max(lg2, axis=-1, keepdims=True)
    i2 = jnp.min(jnp.where(lg2 == m2, lane, float(LANES)), axis=-1, keepdims=True)
    e = jnp.exp(m2 - m1)
    w1 = 1.0 / (1.0 + e)
    w2 = e / (1.0 + e)

    chosen = jnp.where((lane == i1) | (lane == i2), 1.0, 0.0)
    rr = lax.broadcasted_iota(jnp.int32, (rows, rows), 0)
    cc = lax.broadcasted_iota(jnp.int32, (rows, rows), 1)
    earlier = jnp.where(cc < rr, 1.0, 0.0).astype(BF16)
    rank = _dot(earlier, chosen.astype(BF16)) + carry_ref[...]
    r1 = jnp.sum(jnp.where(lane == i1, rank, 0.0), axis=-1, keepdims=True)
    r2 = jnp.sum(jnp.where(lane == i2, rank, 0.0), axis=-1, keepdims=True)
    carry_ref[...] += jnp.sum(chosen, axis=0, keepdims=True)
    cnt_ref[...] = carry_ref[...]

    rec = jnp.zeros_like(logits)
    for slot, val in ((_R_I1, i1), (_R_I2, i2), (_R_W1, w1), (_R_W2, w2), (_R_R1, r1), (_R_R2, r2)):
        rec = jnp.where(lane == slot, val, rec)
    route_ref[...] = rec


def _route_call(cfg, x_all, g, mod_l, w_router, i_moe, n_rows):
    d, bm = cfg.d, cfg.norm_bm
    bidx = cfg.bidx_for(bm)
    return pl.pallas_call(
        _route_kernel,
        grid=(n_rows // bm,),
        in_specs=[pl.BlockSpec((bm, d), lambda i: (i, 0)),
                  pl.BlockSpec((1, d), lambda i: (0, 0)),
                  pl.BlockSpec((None, 1, d), lambda i: (bidx(i), 0, 3)),
                  pl.BlockSpec((None, 1, d), lambda i: (bidx(i), 0, 4)),
                  pl.BlockSpec((None, d, LANES), lambda i: (i_moe, 0, 0))],
        out_specs=[pl.BlockSpec((bm, d // 2), lambda i: (i, 0)),
                   pl.BlockSpec((bm, LANES), lambda i: (i, 0)),
                   pl.BlockSpec((1, LANES), lambda i: (0, 0))],
        out_shape=[jax.ShapeDtypeStruct((n_rows, d // 2), U32),
                   jax.ShapeDtypeStruct((n_rows, LANES), F32),
                   jax.ShapeDtypeStruct((1, LANES), F32)],
        scratch_shapes=[pltpu.VMEM((1, LANES), F32)],
        compiler_params=_cparams(1),
        name="route",
    )(x_all, g.reshape(1, d), mod_l, mod_l, w_router)


def _slot(base_ref, e_ref, r_ref, j):
    return base_ref[e_ref[j]] + r_ref[j]


def _dispatch_kernel(base_ref, e1_ref, e2_ref, r1_ref, r2_ref, src_ref, _, dst_ref, sem):
    tile = e1_ref.shape[0]

    def issue(j, carry):
        src = src_ref.at[pl.ds(j, 1)]
        pltpu.make_async_copy(src, dst_ref.at[pl.ds(_slot(base_ref, e1_ref, r1_ref, j), 1)], sem).start()
        pltpu.make_async_copy(src, dst_ref.at[pl.ds(_slot(base_ref, e2_ref, r2_ref, j), 1)], sem).start()
        return carry

    lax.fori_loop(0, tile, issue, 0, unroll=8)

    def drain(j, carry):
        pltpu.make_async_copy(src_ref.at[pl.ds(0, 1)], dst_ref.at[pl.ds(0, 1)], sem).wait()
        pltpu.make_async_copy(src_ref.at[pl.ds(0, 1)], dst_ref.at[pl.ds(0, 1)], sem).wait()
        return carry

    lax.fori_loop(0, tile, drain, 0, unroll=8)


def _dispatch_call(hp, base, e1, e2, r1, r2, n_slots):
    n_rows, half = hp.shape
    tile = DISPATCH_TILE
    smem_tile = pl.BlockSpec((tile,), lambda i: (i,), memory_space=pltpu.SMEM)
    return pl.pallas_call(
        _dispatch_kernel,
        grid=(n_rows // tile,),
        in_specs=[pl.BlockSpec(memory_space=pltpu.SMEM), smem_tile, smem_tile, smem_tile, smem_tile,
                  pl.BlockSpec((tile, half), lambda i: (i, 0)), pl.BlockSpec(memory_space=pl.ANY)],
        out_specs=pl.BlockSpec(memory_space=pl.ANY),
        out_shape=jax.ShapeDtypeStruct((n_slots, half), U32),
        scratch_shapes=[pltpu.SemaphoreType.DMA(())],
        input_output_aliases={6: 0},
        compiler_params=_cparams(1),
        name="moe_dispatch",
    )(base, e1, e2, r1, r2, hp, jnp.zeros((n_slots, half), U32))


def _glu_sorted_kernel(te_ref, used_ref, a_ref, wg_ref, wu_ref, o_ref):
    @pl.when(pl.program_id(1) < used_ref[0])
    def _():
        lo, hi = _unpack_halves(a_ref[...])
        half = lo.shape[1]
        g = _dot(lo, wg_ref[:half, :]) + _dot(hi, wg_ref[half:, :])
        u = _dot(lo, wu_ref[:half, :]) + _dot(hi, wu_ref[half:, :])
        o_ref[...] = (_silu(g) * u).astype(o_ref.dtype)

    @pl.when(pl.program_id(1) >= used_ref[0])
    def _():
        o_ref[...] = jnp.zeros_like(o_ref)


def _glu_sorted_call(a_sorted, wg, wu, i_moe, tile_expert, n_used):
    n_slots, half = a_sorted.shape
    d = 2 * half
    de = wg.shape[-1]
    bm = SORT_TILE
    bn = de // 2
    w_spec = pl.BlockSpec((None, None, d, bn), lambda j, i, te, nu: (i_moe, te[i], 0, j))
    return pl.pallas_call(
        _glu_sorted_kernel,
        grid_spec=pltpu.PrefetchScalarGridSpec(
            num_scalar_prefetch=2,
            grid=(de // bn, n_slots // bm),
            in_specs=[pl.BlockSpec((bm, half), lambda j, i, te, nu: (i, 0)), w_spec, w_spec],
            out_specs=pl.BlockSpec((bm, bn), lambda j, i, te, nu: (i, j))),
        out_shape=jax.ShapeDtypeStruct((n_slots, de), BF16),
        compiler_params=_cparams(2),
        name="moe_glu",
    )(tile_expert, n_used, a_sorted, wg, wu)


def _down_sorted_kernel(te_ref, used_ref, a_ref, w_ref, o_ref):
    @pl.when(pl.program_id(1) < used_ref[0])
    def _():
        o_ref[...] = _dot(a_ref[...], w_ref[...])

    @pl.when(pl.program_id(1) >= used_ref[0])
    def _():
        o_ref[...] = jnp.zeros_like(o_ref)


def _down_sorted_call(act, wd, i_moe, tile_expert, n_used):
    n_slots, de = act.shape
    d = wd.shape[-1]
    bm = SORT_TILE
    bn = 1024
    return pl.pallas_call(
        _down_sorted_kernel,
        grid_spec=pltpu.PrefetchScalarGridSpec(
            num_scalar_prefetch=2,
            grid=(d // bn, n_slots // bm),
            in_specs=[pl.BlockSpec((bm, de), lambda j, i, te, nu: (i, 0)),
                      pl.BlockSpec((None, None, de, bn), lambda j, i, te, nu: (i_moe, te[i], 0, j))],
            out_specs=pl.BlockSpec((bm, bn), lambda j, i, te, nu: (i, j))),
        out_shape=jax.ShapeDtypeStruct((n_slots, d), F32),
        compiler_params=_cparams(2),
        name="moe_down",
    )(tile_expert, n_used, act, wd)


def _combine_kernel(base_ref, e1c, e2c, r1c, r2c, e1n, e2n, r1n, r2n, y_ref, route_ref, x_ref, gt_ref,
                    o_ref, buf_ref, sem):
    i = pl.program_id(0)
    n = pl.num_programs(0)
    bm = x_ref.shape[0]

    def gather(e1_ref, e2_ref, r1_ref, r2_ref, buf_slot):
        def issue(j, carry):
            pltpu.make_async_copy(y_ref.at[pl.ds(_slot(base_ref, e1_ref, r1_ref, j), 1)],
                                  buf_ref.at[buf_slot, pl.ds(j, 1)], sem.at[buf_slot]).start()
            pltpu.make_async_copy(y_ref.at[pl.ds(_slot(base_ref, e2_ref, r2_ref, j), 1)],
                                  buf_ref.at[buf_slot, pl.ds(bm + j, 1)], sem.at[buf_slot]).start()
            return carry
        lax.fori_loop(0, bm, issue, 0, unroll=8)

    @pl.when(i == 0)
    def _():
        gather(e1c, e2c, r1c, r2c, 0)

    @pl.when(i + 1 < n)
    def _():
        gather(e1n, e2n, r1n, r2n, (i + 1) % 2)

    cur = i % 2

    def drain(j, carry):
        pltpu.make_async_copy(y_ref.at[pl.ds(0, 1)], buf_ref.at[cur, pl.ds(0, 1)], sem.at[cur]).wait()
        pltpu.make_async_copy(y_ref.at[pl.ds(0, 1)], buf_ref.at[cur, pl.ds(0, 1)], sem.at[cur]).wait()
        return carry

    lax.fori_loop(0, bm, drain, 0, unroll=8)

    rec = route_ref[...]
    w1 = rec[:, _R_W1:_R_W1 + 1]
    w2 = rec[:, _R_W2:_R_W2 + 1]
    y = w1 * buf_ref[cur, :bm, :] + w2 * buf_ref[cur, bm:, :]
    o_ref[...] = x_ref[...] + gt_ref[...] * y


def _combine_call(cfg, y_sorted, route, base, e1, e2, r1, r2, x_all, mod_l, n_rows):
    d = cfg.d
    bm = COMBINE_TILE
    bidx = cfg.bidx_for(bm)
    n_tiles = n_rows // bm
    cur = pl.BlockSpec((bm,), lambda i: (i,), memory_space=pltpu.SMEM)
    nxt = pl.BlockSpec((bm,), lambda i: (jnp.minimum(i + 1, n_tiles - 1),), memory_space=pltpu.SMEM)
    return pl.pallas_call(
        _combine_kernel,
        grid=(n_tiles,),
        in_specs=[pl.BlockSpec(memory_space=pltpu.SMEM), cur, cur, cur, cur, nxt, nxt, nxt, nxt,
                  pl.BlockSpec(memory_space=pl.ANY),
                  pl.BlockSpec((bm, LANES), lambda i: (i, 0)),
                  pl.BlockSpec((bm, d), lambda i: (i, 0)),
                  pl.BlockSpec((None, 1, d), lambda i: (bidx(i), 0, 5))],
        out_specs=pl.BlockSpec((bm, d), lambda i: (i, 0)),
        out_shape=jax.ShapeDtypeStruct((cfg.ntok, d), F32),
        scratch_shapes=[pltpu.VMEM((2, 2 * bm, d), F32), pltpu.SemaphoreType.DMA((2,))],
        input_output_aliases={11: 0},
        compiler_params=_cparams(1),
        name="moe_combine",
    )(base, e1, e2, r1, r2, e1, e2, r1, r2, y_sorted, route, x_all, mod_l)


def _moe_ffn(cfg, x_all, g, mod_l, w_router, wg, wu, wd, i_moe, n_rows):
    bm = SORT_TILE
    hp, route, counts = _route_call(cfg, x_all, g, mod_l, w_router, i_moe, n_rows)

    cnt = counts[0, :N_EXPERTS].astype(jnp.int32)
    padded = (cnt + bm - 1) // bm * bm
    ends = jnp.cumsum(padded)
    base = ends - padded
    n_tiles = 2 * n_rows // bm + N_EXPERTS
    tile_start = jnp.arange(n_tiles, dtype=jnp.int32) * bm
    tile_expert = jnp.minimum(jnp.sum((tile_start[:, None] >= ends[None, :]).astype(jnp.int32), axis=1),
                              N_EXPERTS - 1)
    n_used = (ends[-1:] // bm).astype(jnp.int32)
    e1, e2, r1, r2 = (route[:, k].astype(jnp.int32) for k in (_R_I1, _R_I2, _R_R1, _R_R2))

    a_sorted = _dispatch_call(hp, base, e1, e2, r1, r2, n_tiles * bm)
    act = _glu_sorted_call(a_sorted, wg, wu, i_moe, tile_expert, n_used)
    y_sorted = _down_sorted_call(act, wd, i_moe, tile_expert, n_used)
    return _combine_call(cfg, y_sorted, route, base, e1, e2, r1, r2, x_all, mod_l, n_rows)


def _softmax_pv(s, sink_col, v):
    m = jnp.maximum(jnp.max(s, axis=-1, keepdims=True), sink_col)
    p = jnp.exp(s - m)
    denom = jnp.sum(p, axis=-1, keepdims=True) + jnp.exp(sink_col - m)
    return _dot(p.astype(BF16), v) / denom


def _sink_column(sink_ref, group, rows):
    r = lax.broadcasted_iota(jnp.int32, (Q_PER_KV * rows, 1), 0)
    col = jnp.full((Q_PER_KV * rows, 1), sink_ref[group * Q_PER_KV], F32)
    for j in range(1, Q_PER_KV):
        col = jnp.where(r >= j * rows, sink_ref[group * Q_PER_KV + j], col)
    return col


def _attn_lat_kernel(sink_ref, *refs, kvh, t):
    q_refs = refs[:kvh]
    kp_ref, kc_ref, kn_ref, vp_ref, vc_ref, vn_ref, kx_ref, vx_ref, cos_ref, sin_ref, o_ref = refs[kvh:]
    n = pl.program_id(1)
    nb = t // BLOCK
    lane = lax.broadcasted_iota(jnp.int32, (BLOCK, HEAD_DIM), 1)
    first_half = (lane & (HEAD_DIM // 4)) == 0

    def rot(x, blk):
        start = pl.multiple_of(blk * BLOCK, BLOCK)
        c = cos_ref[pl.ds(start, BLOCK), :]
        s = sin_ref[pl.ds(start, BLOCK), :]
        partner = jnp.where(first_half, pltpu.roll(x, HEAD_DIM - HEAD_DIM // 4, 1),
                            pltpu.roll(x, HEAD_DIM // 4, 1))
        return x * c + partner * s

    blk_p = jnp.maximum(n - 1, 0)
    blk_n = jnp.minimum(n + 1, nb - 1)
    n_loc = 3 * BLOCK
    n_keys = n_loc + kx_ref.shape[0]
    rows = Q_PER_KV * BLOCK
    qi = lax.broadcasted_iota(jnp.int32, (rows, n_keys), 0) & (BLOCK - 1)
    kj = lax.broadcasted_iota(jnp.int32, (rows, n_keys), 1)
    rel = kj - qi
    s_pos = n * BLOCK - BLOCK + kj
    valid = (kj >= n_loc) | ((rel >= 0) & (rel <= 2 * BLOCK) & (s_pos >= 0) & (s_pos < t))
    scale = HEAD_DIM ** -0.5

    for g in range(kvh):
        hs = slice(g * HEAD_DIM, (g + 1) * HEAD_DIM)
        q = jnp.concatenate(
            [rot(q_refs[g][:, j * HEAD_DIM:(j + 1) * HEAD_DIM].astype(F32), n) * scale
             for j in range(Q_PER_KV)], axis=0).astype(BF16)
        k = jnp.concatenate(
            [rot(kp_ref[:, hs].astype(F32), blk_p).astype(BF16),
             rot(kc_ref[:, hs].astype(F32), n).astype(BF16),
             rot(kn_ref[:, hs].astype(F32), blk_n).astype(BF16),
             kx_ref[:, hs]], axis=0)
        v = jnp.concatenate([vp_ref[:, hs], vc_ref[:, hs], vn_ref[:, hs], vx_ref[:, hs]], axis=0)
        s = jnp.where(valid, _dot_nt(q, k), jnp.finfo(F32).min)
        o = _softmax_pv(s, _sink_column(sink_ref, g, BLOCK), v)
        for j in range(Q_PER_KV):
            h = g * Q_PER_KV + j
            o_ref[:, h * HEAD_DIM:(h + 1) * HEAD_DIM] = o[j * BLOCK:(j + 1) * BLOCK].astype(o_ref.dtype)


def _attn_lat_call(cfg, p, sink, cos_a, sin_a):
    b, t, l = cfg.b, cfg.t, cfg.l
    nb = t // BLOCK
    kvh = cfg.kvh
    gw = Q_PER_KV * HEAD_DIM
    kvw = kvh * HEAD_DIM
    q_blk = cfg.off_qa // gw
    k_blk, v_blk = cfg.off_ka // kvw, cfg.off_va // kvw
    ctx_row0 = (b * t) // l

    def kv_spec(col_blk, shift):
        def imap(bi, n):
            return (bi * nb + jnp.clip(n + shift, 0, nb - 1), col_blk)
        return pl.BlockSpec((BLOCK, kvw), imap)

    in_specs = [pl.BlockSpec(memory_space=pltpu.SMEM)]
    in_specs += [pl.BlockSpec((BLOCK, gw), functools.partial(lambda bi, n, g: (bi * nb + n, q_blk + g), g=g))
                 for g in range(kvh)]
    in_specs += [kv_spec(k_blk, -1), kv_spec(k_blk, 0), kv_spec(k_blk, 1),
                 kv_spec(v_blk, -1), kv_spec(v_blk, 0), kv_spec(v_blk, 1),
                 pl.BlockSpec((l, kvw), lambda bi, n: (ctx_row0 + bi, k_blk)),
                 pl.BlockSpec((l, kvw), lambda bi, n: (ctx_row0 + bi, v_blk)),
                 pl.BlockSpec((t, HEAD_DIM), lambda bi, n: (0, 0)),
                 pl.BlockSpec((t, HEAD_DIM), lambda bi, n: (0, 0))]
    return pl.pallas_call(
        functools.partial(_attn_lat_kernel, kvh=kvh, t=t),
        grid=(b, nb),
        in_specs=in_specs,
        out_specs=pl.BlockSpec((BLOCK, cfg.a_q), lambda bi, n: (bi * nb + n, 0)),
        out_shape=jax.ShapeDtypeStruct((b * t, cfg.a_q), BF16),
        compiler_params=_cparams(2),
        name="attn_latent",
    )(sink, *([p] * (kvh + 8)), cos_a, sin_a)


def _attn_ctx_kernel(sink_ref, *refs, kvh):
    q_refs = refs[:kvh]
    kx_ref, vx_ref, o_ref = refs[kvh:]
    l = kx_ref.shape[0]
    scale = HEAD_DIM ** -0.5
    for g in range(kvh):
        hs = slice(g * HEAD_DIM, (g + 1) * HEAD_DIM)
        q = jnp.concatenate(
            [(q_refs[g][:, j * HEAD_DIM:(j + 1) * HEAD_DIM].astype(F32) * scale).astype(BF16)
             for j in range(Q_PER_KV)], axis=0)
        s = _dot_nt(q, kx_ref[:, hs])
        o = _softmax_pv(s, _sink_column(sink_ref, g, l), vx_ref[:, hs])
        for j in range(Q_PER_KV):
            h = g * Q_PER_KV + j
            o_ref[:, h * HEAD_DIM:(h + 1) * HEAD_DIM] = o[j * l:(j + 1) * l].astype(o_ref.dtype)


def _attn_ctx_call(cfg, p, sink):
    b, t, l = cfg.b, cfg.t, cfg.l
    kvh = cfg.kvh
    gw = Q_PER_KV * HEAD_DIM
    kvw = kvh * HEAD_DIM
    q_blk = cfg.off_qa // gw
    k_blk, v_blk = cfg.off_ka // kvw, cfg.off_va // kvw
    ctx_row0 = (b * t) // l
    in_specs = [pl.BlockSpec(memory_space=pltpu.SMEM)]
    in_specs += [pl.BlockSpec((l, gw), functools.partial(lambda bi, g: (ctx_row0 + bi, q_blk + g), g=g))
                 for g in range(kvh)]
    in_specs += [pl.BlockSpec((l, kvw), lambda bi: (ctx_row0 + bi, k_blk)),
                 pl.BlockSpec((l, kvw), lambda bi: (ctx_row0 + bi, v_blk))]
    return pl.pallas_call(
        functools.partial(_attn_ctx_kernel, kvh=kvh),
        grid=(b,),
        in_specs=in_specs,
        out_specs=pl.BlockSpec((l, cfg.a_q), lambda bi: (bi, 0)),
        out_shape=jax.ShapeDtypeStruct((b * l, cfg.a_q), BF16),
        compiler_params=_cparams(1),
        name="attn_ctx",
    )(sink, *([p] * (kvh + 2)))


def _retention_scan(q_of, k_of, v_ref, g_ref, gn, lg_f, lg_b, o_ref, acc_ref, n_chunks, s_f, s_b):
    c = BLOCK
    ii = lax.broadcasted_iota(jnp.int32, (c, c), 0).astype(F32)
    jj = lax.broadcasted_iota(jnp.int32, (c, c), 1).astype(F32)
    diff = ii - jj
    decay = jnp.where(diff > 0, jnp.exp(lg_f * jnp.maximum(diff, 0.0)),
                      jnp.where(diff < 0, jnp.exp(lg_b * jnp.maximum(-diff, 0.0)), 2.0))
    zeta_f = jnp.exp(lg_f * (c - 1 - ii))
    xi_f = jnp.exp(lg_f * (ii + 1))
    zeta_b = jnp.exp(lg_b * ii)
    xi_b = jnp.exp(lg_b * (c - ii))
    cd_f = jnp.exp(jnp.concatenate([lg_f, lg_f], axis=1) * c)
    cd_b = jnp.exp(jnp.concatenate([lg_b, lg_b], axis=1) * c)

    for i in reversed(range(n_chunks)):
        rows = slice(i * c, (i + 1) * c)
        q, k, v = q_of(i), k_of(i), v_ref[rows, :]
        acc_ref[rows, :] = _dot((q * xi_b).astype(BF16), s_b.astype(BF16))
        s_b = cd_b * s_b + _dot_tn((k * zeta_b).astype(BF16), v)

    for i in range(n_chunks):
        rows = slice(i * c, (i + 1) * c)
        q, k, v = q_of(i), k_of(i), v_ref[rows, :]
        scores = _dot_nt(q.astype(BF16), k.astype(BF16)) * decay
        o = (acc_ref[rows, :] + _dot(scores.astype(BF16), v)
             + _dot((q * xi_f).astype(BF16), s_f.astype(BF16)))
        s_f = cd_f * s_f + _dot_tn((k * zeta_f).astype(BF16), v)
        mu = jnp.mean(o, axis=-1, keepdims=True)
        dev = o - mu
        var = jnp.mean(dev * dev, axis=-1, keepdims=True)
        on = dev * lax.rsqrt(var + EPS) * gn
        gate = g_ref[rows, :].astype(F32)
        o_ref[rows, :] = (_silu(gate) * on).astype(o_ref.dtype)
    return s_f, s_b


def _log_decay(a_row):
    return jnp.log1p(-jnp.exp2(-a_row))


def _ret_ctx_kernel(q_ref, k_ref, v_ref, g_ref, gn_ref, af_ref, ab_ref, o_ref, sf_ref, sb_ref, acc_ref):
    scale = HEAD_DIM ** -0.5
    n_chunks = q_ref.shape[0] // BLOCK

    def q_of(i):
        return q_ref[i * BLOCK:(i + 1) * BLOCK, :].astype(F32)

    def k_of(i):
        return k_ref[i * BLOCK:(i + 1) * BLOCK, :].astype(F32) * scale

    zero = jnp.zeros((HEAD_DIM, R_DV), F32)
    s_f, s_b = _retention_scan(q_of, k_of, v_ref, g_ref, gn_ref[...], _log_decay(af_ref[...]),
                               _log_decay(ab_ref[...]), o_ref, acc_ref, n_chunks, zero, zero)
    sf_ref[...] = s_f
    sb_ref[...] = s_b


def _ret_specs(cfg, rows, row_blk):
    qb, kb = cfg.off_qr // HEAD_DIM, cfg.off_kr // HEAD_DIM
    vb, gb = cfg.off_vr // R_DV, cfg.off_gr // R_DV
    return [pl.BlockSpec((rows, HEAD_DIM), lambda bi, h: (row_blk(bi), qb + h)),
            pl.BlockSpec((rows, HEAD_DIM), lambda bi, h: (row_blk(bi), kb + h)),
            pl.BlockSpec((rows, R_DV), lambda bi, h: (row_blk(bi), vb + h)),
            pl.BlockSpec((rows, R_DV), lambda bi, h: (row_blk(bi), gb + h)),
            pl.BlockSpec((1, R_DV), lambda bi, h: (0, h)),
            pl.BlockSpec((None, 1, HEAD_DIM), lambda bi, h: (h, 0, 0)),
            pl.BlockSpec((None, 1, HEAD_DIM), lambda bi, h: (h, 0, 0))]


def _ret_ctx_call(cfg, p, gn, a_f, a_b):
    b, t, l, rh = cfg.b, cfg.t, cfg.l, cfg.rh
    ctx_row0 = (b * t) // l
    state_spec = pl.BlockSpec((None, None, HEAD_DIM, R_DV), lambda bi, h: (bi, h, 0, 0))
    state_shape = jax.ShapeDtypeStruct((b, rh, HEAD_DIM, R_DV), F32)
    return pl.pallas_call(
        _ret_ctx_kernel,
        grid=(b, rh),
        in_specs=_ret_specs(cfg, l, lambda bi: ctx_row0 + bi),
        out_specs=[pl.BlockSpec((l, R_DV), lambda bi, h: (bi, h)), state_spec, state_spec],
        out_shape=[jax.ShapeDtypeStruct((b * l, cfg.r_v), BF16), state_shape, state_shape],
        scratch_shapes=[pltpu.VMEM((l, R_DV), F32)],
        compiler_params=_cparams(2),
        name="retention_ctx",
    )(p, p, p, p, gn, a_f, a_b)


def _ret_lat_kernel(q_ref, k_ref, v_ref, g_ref, gn_ref, af_ref, ab_ref, cos_ref, sin_ref, sf_ref, sb_ref,
                    o_ref, acc_ref, qr_ref, kr_ref):
    scale = HEAD_DIM ** -0.5
    n_chunks = q_ref.shape[0] // BLOCK
    cos, sin = cos_ref[...], sin_ref[...]

    def rot(x):
        return x * cos + pltpu.roll(x, HEAD_DIM // 2, 1) * sin

    qr_ref[...] = rot(q_ref[...].astype(F32))
    kr_ref[...] = rot(k_ref[...].astype(F32)) * scale

    def q_of(i):
        return qr_ref[i * BLOCK:(i + 1) * BLOCK, :]

    def k_of(i):
        return kr_ref[i * BLOCK:(i + 1) * BLOCK, :]

    _retention_scan(q_of, k_of, v_ref, g_ref, gn_ref[...], _log_decay(af_ref[...]), _log_decay(ab_ref[...]),
                    o_ref, acc_ref, n_chunks, sf_ref[...], sb_ref[...])


def _ret_lat_call(cfg, p, gn, a_f, a_b, cos_r, sin_r, s_f, s_b):
    b, t, rh = cfg.b, cfg.t, cfg.rh
    state_spec = pl.BlockSpec((None, None, HEAD_DIM, R_DV), lambda bi, h: (bi, h, 0, 0))
    table_spec = pl.BlockSpec((t, HEAD_DIM), lambda bi, h: (0, 0))
    in_specs = _ret_specs(cfg, t, lambda bi: bi) + [table_spec, table_spec, state_spec, state_spec]
    return pl.pallas_call(
        _ret_lat_kernel,
        grid=(b, rh),
        in_specs=in_specs,
        out_specs=pl.BlockSpec((t, R_DV), lambda bi, h: (bi, h)),
        out_shape=jax.ShapeDtypeStruct((b * t, cfg.r_v), BF16),
        scratch_shapes=[pltpu.VMEM((t, R_DV), F32), pltpu.VMEM((t, HEAD_DIM), F32),
                        pltpu.VMEM((t, HEAD_DIM), F32)],
        compiler_params=_cparams(2),
        name="retention_latent",
    )(p, p, p, p, gn, a_f, a_b, cos_r, sin_r, s_f, s_b)


def _rotary_tables(t):
    rows = t // GRID_W
    row = jnp.repeat(jnp.arange(rows, dtype=F32), GRID_W)
    col = jnp.tile(jnp.arange(GRID_W, dtype=F32), rows)
    nf = HEAD_DIM // 4
    inv = jnp.power(ROPE_BASE, -jnp.arange(nf, dtype=F32) / nf)
    ang_row, ang_col = row[:, None] * inv, col[:, None] * inv
    cos_a = jnp.concatenate([jnp.cos(ang_row)] * 2 + [jnp.cos(ang_col)] * 2, axis=1)
    sin_a = jnp.concatenate([-jnp.sin(ang_row), jnp.sin(ang_row), -jnp.sin(ang_col), jnp.sin(ang_col)], axis=1)
    nf = HEAD_DIM // 2
    inv = jnp.power(ROPE_BASE, -jnp.arange(nf, dtype=F32) / nf)
    ang = jnp.arange(t, dtype=F32)[:, None] * inv
    cos_r = jnp.concatenate([jnp.cos(ang)] * 2, axis=1)
    sin_r = jnp.concatenate([-jnp.sin(ang), jnp.sin(ang)], axis=1)
    return cos_a, sin_a, cos_r, sin_r


class _Config:
    def __init__(self, x, ctx, w_in, w_gate_d):
        self.b, self.t, self.d = x.shape
        self.l = ctx.shape[1]
        self.ntok = self.b * (self.t + self.l)
        self.ha = self.d // 256
        self.kvh = self.ha // Q_PER_KV
        self.rh = self.d // 256
        self.a_q = self.ha * HEAD_DIM
        self.a_kv = self.kvh * HEAD_DIM
        self.r_qk = self.rh * HEAD_DIM
        self.r_v = self.rh * R_DV
        self.off_ka = 0
        self.off_va = self.a_kv
        self.off_kr = 2 * self.a_kv
        self.off_vr = self.off_kr + self.r_qk
        self.off_qa = self.off_vr + self.r_v
        self.off_qr = self.off_qa + self.a_q
        self.off_gr = self.off_qr + self.r_qk
        self.off_za = self.off_gr + self.r_v
        self.off_zr = self.off_za + self.d
        assert w_in.shape[-1] == self.off_zr + self.d
        self.bm = min(ROW_TILE, self.t)
        self.norm_bm = min(NORM_ROW_TILE, self.t)
        assert self.t % self.bm == 0 and (self.b * self.l) % self.bm == 0
        assert self.t % GRID_W == 0 and self.t % BLOCK == 0 and self.l % BLOCK == 0

    def bidx_for(self, tile):
        assert self.t % tile == 0 and (self.b * self.l) % tile == 0
        tiles_per_batch = self.t // tile
        n_batch = self.b
        return lambda i: jnp.minimum(i // tiles_per_batch, n_batch)


def kernel(x, c, ctx, c_ctx, w_ada, b_ada, g_mix, g_ffn, w_in, attn_sink, ret_a_fwd, ret_a_bwd, ret_gn,
           w_br_attn, w_br_ret, w_out, w_gate_d, w_up_d, w_down_d, w_router, w_gate_e, w_up_e, w_down_e,
           g_final):
    cfg = _Config(x, ctx, w_in, w_gate_d)
    b, t, d, l = cfg.b, cfg.t, cfg.d, cfg.l
    depth = w_ada.shape[0]
    n_lat = b * t

    x_all = jnp.concatenate([x.reshape(n_lat, d), ctx.reshape(b * l, d)], axis=0)
    mod_rows = -(-(b + 1) // 8) * 8
    c_all = jnp.zeros((mod_rows, d), F32).at[:b].set(c).at[b].set(c_ctx)
    mod = _ada_call(c_all, w_ada, b_ada).reshape(depth, mod_rows, 1, 6 * d)

    cos_a, sin_a, cos_r, sin_r = _rotary_tables(t)
    w_in_h, w_br_attn_h, w_br_ret_h, w_out_h = (w.astype(BF16) for w in (w_in, w_br_attn, w_br_ret, w_out))
    w_gate_d_h, w_up_d_h, w_down_d_h = (w.astype(BF16) for w in (w_gate_d, w_up_d, w_down_d))
    w_gate_e_h, w_up_e_h, w_down_e_h = (w.astype(BF16) for w in (w_gate_e, w_up_e, w_down_e))
    w_router_h = jnp.pad(w_router, ((0, 0), (0, 0), (0, LANES - N_EXPERTS))).astype(BF16)
    a_f = jnp.broadcast_to(ret_a_fwd.astype(F32)[:, :, None, None], ret_a_fwd.shape + (1, HEAD_DIM))
    a_b = jnp.broadcast_to(ret_a_bwd.astype(F32)[:, :, None, None], ret_a_bwd.shape + (1, HEAD_DIM))

    for layer in range(depth):
        last = layer == depth - 1
        n_rows = n_lat if last else cfg.ntok
        mod_l = mod[layer]
        gn = ret_gn[layer].reshape(1, cfg.r_v)

        h = _norm_mod_call(cfg, x_all, g_mix[layer], mod_l, 0, cfg.ntok)
        p = _in_proj_call(cfg, h, w_in_h, layer)
        y_a = _attn_lat_call(cfg, p, attn_sink[layer], cos_a, sin_a)
        y_r_ctx, s_f, s_b = _ret_ctx_call(cfg, p, gn, a_f[layer], a_b[layer])
        y_r = _ret_lat_call(cfg, p, gn, a_f[layer], a_b[layer], cos_r, sin_r, s_f, s_b)
        if last:
            y_a_ctx, y_r_ctx = y_a, y_r
        else:
            y_a_ctx = _attn_ctx_call(cfg, p, attn_sink[layer])
        m = _merge_call(cfg, y_a, y_r, y_a_ctx, y_r_ctx, w_br_attn_h, w_br_ret_h, layer, p, n_rows)
        x_all = _resid_call(cfg, m, w_out_h, layer, x_all, mod_l, 2, n_rows, bn=1024)

        i = layer // 2
        if layer % 2 == 0:
            hf = _norm_mod_call(cfg, x_all, g_ffn[layer], mod_l, 3, n_rows)
            act = _glu_call(cfg, hf, w_gate_d_h, w_up_d_h, i, n_rows)
            x_all = _resid_call(cfg, act, w_down_d_h, i, x_all, mod_l, 5, n_rows, bn=512)
        else:
            x_all = _moe_ffn(cfg, x_all, g_ffn[layer], mod_l, w_router_h, w_gate_e_h, w_up_e_h, w_down_e_h,
                             i, n_rows)

    return _final_norm_call(cfg, x_all, g_final).reshape(b, t, d)
```

```python
import functools

import numpy as np
import jax
import jax.numpy as jnp
from jax import lax
from jax.experimental import pallas as pl
from jax.experimental.pallas import tpu as pltpu

F32 = jnp.float32
BF16 = jnp.bfloat16

EPS = 1e-6
ROPE_BASE = 10000.0
GRID_W = 64
HEAD_DIM = 128
R_DV = 256
BLOCK = 128
Q_PER_KV = 4
N_EXPERTS = 8
LANES = 128

V7X_VMEM_BYTES = 64 * 1024 * 1024
VMEM_LIMIT = V7X_VMEM_BYTES - 8 * 1024 * 1024
ROW_TILE = 1024
NORM_ROW_TILE = 512
SORT_TILE = 512
DISPATCH_TILE = 512
COMBINE_TILE = 256
U32 = jnp.uint32


def _cparams(n_axes):
    return pltpu.CompilerParams(dimension_semantics=("arbitrary",) * n_axes,
                                vmem_limit_bytes=VMEM_LIMIT)


def _dot(a, b):
    return jnp.dot(a, b, preferred_element_type=F32)


def _dot_nt(a, b):
    return lax.dot_general(a, b, (((1,), (1,)), ((), ())), preferred_element_type=F32)


def _dot_tn(a, b):
    return lax.dot_general(a, b, (((0,), (0,)), ((), ())), preferred_element_type=F32)


def _silu(x):
    return x * jax.nn.sigmoid(x)


def _ada_kernel(c_ref, w_ref, b_ref, o_ref):
    s = _silu(c_ref[...]).astype(BF16)
    o_ref[...] = _dot(s, w_ref[...].astype(BF16)) + b_ref[...]


def _ada_call(c_all, w_ada, b_ada):
    depth, d, n6 = w_ada.shape
    r = c_all.shape[0]
    bn = 1024
    return pl.pallas_call(
        _ada_kernel,
        grid=(depth, n6 // bn),
        in_specs=[pl.BlockSpec((r, d), lambda l, j: (0, 0)),
                  pl.BlockSpec((None, d, bn), lambda l, j: (l, 0, j)),
                  pl.BlockSpec((None, 1, bn), lambda l, j: (l, 0, j))],
        out_specs=pl.BlockSpec((None, r, bn), lambda l, j: (l, 0, j)),
        out_shape=jax.ShapeDtypeStruct((depth, r, n6), F32),
        compiler_params=_cparams(2),
        name="ada_mod",
    )(c_all, w_ada, b_ada.reshape(depth, 1, n6))


def _norm_mod_kernel(x_ref, g_ref, sh_ref, sc_ref, o_ref):
    x = x_ref[...]
    y = x * lax.rsqrt(jnp.mean(x * x, axis=-1, keepdims=True) + EPS) * g_ref[...]
    o_ref[...] = (y * (1.0 + sc_ref[...]) + sh_ref[...]).astype(o_ref.dtype)


def _norm_kernel(x_ref, g_ref, o_ref):
    x = x_ref[...]
    o_ref[...] = x * lax.rsqrt(jnp.mean(x * x, axis=-1, keepdims=True) + EPS) * g_ref[...]


def _norm_mod_call(cfg, x_all, g, mod_l, shift_chunk, n_rows):
    d, bm = cfg.d, cfg.norm_bm
    bidx = cfg.bidx_for(bm)
    return pl.pallas_call(
        _norm_mod_kernel,
        grid=(n_rows // bm,),
        in_specs=[pl.BlockSpec((bm, d), lambda i: (i, 0)),
                  pl.BlockSpec((1, d), lambda i: (0, 0)),
                  pl.BlockSpec((None, 1, d), lambda i: (bidx(i), 0, shift_chunk)),
                  pl.BlockSpec((None, 1, d), lambda i: (bidx(i), 0, shift_chunk + 1))],
        out_specs=pl.BlockSpec((bm, d), lambda i: (i, 0)),
        out_shape=jax.ShapeDtypeStruct((n_rows, d), BF16),
        compiler_params=_cparams(1),
        name="norm_mod",
    )(x_all, g.reshape(1, d), mod_l, mod_l)


def _final_norm_call(cfg, x_all, g):
    d, bm = cfg.d, cfg.norm_bm
    n_rows = cfg.b * cfg.t
    return pl.pallas_call(
        _norm_kernel,
        grid=(n_rows // bm,),
        in_specs=[pl.BlockSpec((bm, d), lambda i: (i, 0)),
                  pl.BlockSpec((1, d), lambda i: (0, 0))],
        out_specs=pl.BlockSpec((bm, d), lambda i: (i, 0)),
        out_shape=jax.ShapeDtypeStruct((n_rows, d), F32),
        compiler_params=_cparams(1),
        name="final_norm",
    )(x_all, g.reshape(1, d))


def _mm_kernel(a_ref, w_ref, o_ref):
    o_ref[...] = _dot(a_ref[...], w_ref[...].astype(BF16)).astype(o_ref.dtype)


def _wide_tile(cfg, n_rows):
    return 2 * cfg.bm if n_rows % (2 * cfg.bm) == 0 else cfg.bm


def _mm_ctx_cols_kernel(a_ref, w_ref, o_ref, *, lat_tiles, ctx_blocks):
    unused = (pl.program_id(0) >= lat_tiles) & (pl.program_id(1) >= ctx_blocks)

    @pl.when(jnp.logical_not(unused))
    def _():
        _mm_kernel(a_ref, w_ref, o_ref)

    @pl.when(unused)
    def _():
        o_ref[...] = jnp.zeros_like(o_ref)


def _in_proj_call(cfg, h, w, layer, ctx_kv_only):
    d, bm = cfg.d, _wide_tile(cfg, cfg.ntok)
    n = w.shape[-1]
    bn = 512
    kern = _mm_kernel
    if ctx_kv_only and (cfg.b * cfg.t) % bm == 0 and cfg.off_qa % bn == 0:
        kern = functools.partial(_mm_ctx_cols_kernel, lat_tiles=(cfg.b * cfg.t) // bm,
                                 ctx_blocks=cfg.off_qa // bn)
    return pl.pallas_call(
        kern,
        grid=(cfg.ntok // bm, n // bn),
        in_specs=[pl.BlockSpec((bm, d), lambda i, j: (i, 0)),
                  pl.BlockSpec((None, d, bn), lambda i, j: (layer, 0, j))],
        out_specs=pl.BlockSpec((bm, bn), lambda i, j: (i, j)),
        out_shape=jax.ShapeDtypeStruct((cfg.ntok, n), BF16),
        compiler_params=_cparams(2),
        name="in_proj",
    )(h, w)


def _merge_kernel(ya_ref, yr_ref, yac_ref, yrc_ref, wa_ref, wr_ref, za_ref, zr_ref, o_ref, *, lat_tiles):
    def merge(ya, yr):
        a = _dot(ya[...], wa_ref[...])
        r = _dot(yr[...], wr_ref[...])
        za = za_ref[...].astype(F32)
        zr = zr_ref[...].astype(F32)
        o_ref[...] = (jax.nn.sigmoid(za) * a + jax.nn.sigmoid(zr) * r).astype(o_ref.dtype)

    i = pl.program_id(0)

    @pl.when(i < lat_tiles)
    def _():
        merge(ya_ref, yr_ref)

    @pl.when(i >= lat_tiles)
    def _():
        merge(yac_ref, yrc_ref)


def _merge_call(cfg, y_a, y_r, y_a_ctx, y_r_ctx, w_a, w_r, layer, p, n_rows):
    d, bm = cfg.d, cfg.bm
    bn = 512
    za_blk, zr_blk = cfg.off_za // bn, cfg.off_zr // bn
    lat_tiles = (cfg.b * cfg.t) // bm

    def lat(i, j):
        return (jnp.minimum(i, lat_tiles - 1), 0)

    def ctx(i, j):
        return (jnp.maximum(i - lat_tiles, 0), 0)

    return pl.pallas_call(
        functools.partial(_merge_kernel, lat_tiles=lat_tiles),
        grid=(n_rows // bm, d // bn),
        in_specs=[pl.BlockSpec((bm, cfg.a_q), lat),
                  pl.BlockSpec((bm, cfg.r_v), lat),
                  pl.BlockSpec((bm, cfg.a_q), ctx),
                  pl.BlockSpec((bm, cfg.r_v), ctx),
                  pl.BlockSpec((None, cfg.a_q, bn), lambda i, j: (layer, 0, j)),
                  pl.BlockSpec((None, cfg.r_v, bn), lambda i, j: (layer, 0, j)),
                  pl.BlockSpec((bm, bn), lambda i, j: (i, za_blk + j)),
                  pl.BlockSpec((bm, bn), lambda i, j: (i, zr_blk + j))],
        out_specs=pl.BlockSpec((bm, bn), lambda i, j: (i, j)),
        out_shape=jax.ShapeDtypeStruct((n_rows, d), BF16),
        compiler_params=_cparams(2),
        name="merge",
    )(y_a, y_r, y_a_ctx, y_r_ctx, w_a, w_r, p, p)


def _resid_kernel(a_ref, w_ref, x_ref, gt_ref, o_ref):
    o_ref[...] = x_ref[...] + gt_ref[...] * _dot(a_ref[...], w_ref[...])


def _resid_call(cfg, a, w, widx, x_all, mod_l, gate_chunk, n_rows, bn):
    d, bm = cfg.d, cfg.bm
    k = a.shape[1]
    bidx = cfg.bidx_for(bm)
    gblk = gate_chunk * (d // bn)
    return pl.pallas_call(
        _resid_kernel,
        grid=(n_rows // bm, d // bn),
        in_specs=[pl.BlockSpec((bm, k), lambda i, j: (i, 0)),
                  pl.BlockSpec((None, k, bn), lambda i, j: (widx, 0, j)),
                  pl.BlockSpec((bm, bn), lambda i, j: (i, j)),
                  pl.BlockSpec((None, 1, bn), lambda i, j: (bidx(i), 0, gblk + j))],
        out_specs=pl.BlockSpec((bm, bn), lambda i, j: (i, j)),
        out_shape=jax.ShapeDtypeStruct((cfg.ntok, d), F32),
        input_output_aliases={2: 0},
        compiler_params=_cparams(2),
        name="resid_proj",
    )(a, w, x_all, mod_l)


def _glu_kernel(a_ref, wg_ref, wu_ref, o_ref):
    a = a_ref[...]
    g = _dot(a, wg_ref[...].astype(BF16))
    u = _dot(a, wu_ref[...].astype(BF16))
    o_ref[...] = (_silu(g) * u).astype(o_ref.dtype)


def _glu_call(cfg, h, wg, wu, widx, n_rows):
    d, bm = cfg.d, _wide_tile(cfg, n_rows)
    n = wg.shape[-1]
    bn = 512
    w_spec = pl.BlockSpec((None, d, bn), lambda i, j: (widx, 0, j))
    return pl.pallas_call(
        _glu_kernel,
        grid=(n_rows // bm, n // bn),
        in_specs=[pl.BlockSpec((bm, d), lambda i, j: (i, 0)), w_spec, w_spec],
        out_specs=pl.BlockSpec((bm, bn), lambda i, j: (i, j)),
        out_shape=jax.ShapeDtypeStruct((n_rows, n), BF16),
        compiler_params=_cparams(2),
        name="glu_up",
    )(h, wg, wu)


_R_I1, _R_I2, _R_W1, _R_W2, _R_R1, _R_R2 = range(6)


def _pack_halves(h):
    half = h.shape[1] // 2
    lo = lax.bitcast_convert_type(h[:, :half].astype(F32), U32)
    hi = lax.bitcast_convert_type(h[:, half:].astype(F32), U32)
    return (hi & jnp.uint32(0xFFFF0000)) | lax.shift_right_logical(lo, jnp.uint32(16))


def _unpack_halves(packed):
    lo = lax.bitcast_convert_type(lax.shift_left(packed, jnp.uint32(16)), F32)
    hi = lax.bitcast_convert_type(packed & jnp.uint32(0xFFFF0000), F32)
    return lo.astype(BF16), hi.astype(BF16)


def _route_kernel(x_ref, g_ref, sh_ref, sc_ref, wr_ref, hp_ref, route_ref, cnt_ref, carry_ref):
    @pl.when(pl.program_id(0) == 0)
    def _():
        carry_ref[...] = jnp.zeros_like(carry_ref)

    x = x_ref[...]
    y = x * lax.rsqrt(jnp.mean(x * x, axis=-1, keepdims=True) + EPS) * g_ref[...]
    h = (y * (1.0 + sc_ref[...]) + sh_ref[...]).astype(BF16)
    hp_ref[...] = _pack_halves(h)

    logits = _dot(h, wr_ref[...])
    rows = logits.shape[0]
    lane = lax.broadcasted_iota(jnp.int32, logits.shape, 1).astype(F32)
    neg = -jnp.inf
    lg = jnp.where(lane < N_EXPERTS, logits, neg)
    m1 = jnp.max(lg, axis=-1, keepdims=True)
    i1 = jnp.min(jnp.where(lg == m1, lane, float(LANES)), axis=-1, keepdims=True)
    lg2 = jnp.where(lane == i1, neg, lg)
    m2 = jnp.max(lg2, axis=-1, keepdims=True)
    i2 = jnp.min(jnp.where(lg2 == m2, lane, float(LANES)), axis=-1, keepdims=True)
    e = jnp.exp(m2 - m1)
    w1 = 1.0 / (1.0 + e)
    w2 = e / (1.0 + e)

    chosen = jnp.where((lane == i1) | (lane == i2), 1.0, 0.0)
    rr = lax.broadcasted_iota(jnp.int32, (rows, rows), 0)
    cc = lax.broadcasted_iota(jnp.int32, (rows, rows), 1)
    earlier = jnp.where(cc < rr, 1.0, 0.0).astype(BF16)
    rank = _dot(earlier, chosen.astype(BF16)) + carry_ref[...]
    r1 = jnp.sum(jnp.where(lane == i1, rank, 0.0), axis=-1, keepdims=True)
    r2 = jnp.sum(jnp.where(lane == i2, rank, 0.0), axis=-1, keepdims=True)
    carry_ref[...] += jnp.sum(chosen, axis=0, keepdims=True)
    cnt_ref[...] = carry_ref[...]

    rec = jnp.zeros_like(logits)
    for slot, val in ((_R_I1, i1), (_R_I2, i2), (_R_W1, w1), (_R_W2, w2), (_R_R1, r1), (_R_R2, r2)):
        rec = jnp.where(lane == slot, val, rec)
    route_ref[...] = rec


def _route_call(cfg, x_all, g, mod_l, w_router, i_moe, n_rows):
    d, bm = cfg.d, cfg.norm_bm
    bidx = cfg.bidx_for(bm)
    return pl.pallas_call(
        _route_kernel,
        grid=(n_rows // bm,),
        in_specs=[pl.BlockSpec((bm, d), lambda i: (i, 0)),
                  pl.BlockSpec((1, d), lambda i: (0, 0)),
                  pl.BlockSpec((None, 1, d), lambda i: (bidx(i), 0, 3)),
                  pl.BlockSpec((None, 1, d), lambda i: (bidx(i), 0, 4)),
                  pl.BlockSpec((None, d, LANES), lambda i: (i_moe, 0, 0))],
        out_specs=[pl.BlockSpec((bm, d // 2), lambda i: (i, 0)),
                   pl.BlockSpec((bm, LANES), lambda i: (i, 0)),
                   pl.BlockSpec((1, LANES), lambda i: (0, 0))],
        out_shape=[jax.ShapeDtypeStruct((n_rows, d // 2), U32),
                   jax.ShapeDtypeStruct((n_rows, LANES), F32),
                   jax.ShapeDtypeStruct((1, LANES), F32)],
        scratch_shapes=[pltpu.VMEM((1, LANES), F32)],
        compiler_params=_cparams(1),
        name="route",
    )(x_all, g.reshape(1, d), mod_l, mod_l, w_router)


def _slot(base_ref, e_ref, r_ref, j):
    return base_ref[e_ref[j]] + r_ref[j]


def _dispatch_kernel(base_ref, e1_ref, e2_ref, r1_ref, r2_ref, src_ref, _, dst_ref, sem):
    tile = e1_ref.shape[0]

    def issue(j, carry):
        src = src_ref.at[pl.ds(j, 1)]
        pltpu.make_async_copy(src, dst_ref.at[pl.ds(_slot(base_ref, e1_ref, r1_ref, j), 1)], sem).start()
        pltpu.make_async_copy(src, dst_ref.at[pl.ds(_slot(base_ref, e2_ref, r2_ref, j), 1)], sem).start()
        return carry

    lax.fori_loop(0, tile, issue, 0, unroll=8)

    for _ in range(2):
        pltpu.make_async_copy(src_ref, dst_ref.at[pl.ds(0, tile)], sem).wait()


def _dispatch_call(hp, base, e1, e2, r1, r2, n_slots):
    n_rows, half = hp.shape
    tile = DISPATCH_TILE
    smem_tile = pl.BlockSpec((tile,), lambda i: (i,), memory_space=pltpu.SMEM)
    return pl.pallas_call(
        _dispatch_kernel,
        grid=(n_rows // tile,),
        in_specs=[pl.BlockSpec(memory_space=pltpu.SMEM), smem_tile, smem_tile, smem_tile, smem_tile,
                  pl.BlockSpec((tile, half), lambda i: (i, 0)), pl.BlockSpec(memory_space=pl.ANY)],
        out_specs=pl.BlockSpec(memory_space=pl.ANY),
        out_shape=jax.ShapeDtypeStruct((n_slots, half), U32),
        scratch_shapes=[pltpu.SemaphoreType.DMA(())],
        input_output_aliases={6: 0},
        compiler_params=_cparams(1),
        name="moe_dispatch",
    )(base, e1, e2, r1, r2, hp, jnp.zeros((n_slots, half), U32))


def _glu_sorted_kernel(te_ref, used_ref, a_ref, wg_ref, wu_ref, o_ref):
    @pl.when(pl.program_id(1) < used_ref[0])
    def _():
        lo, hi = _unpack_halves(a_ref[...])
        half = lo.shape[1]
        g = _dot(lo, wg_ref[:half, :]) + _dot(hi, wg_ref[half:, :])
        u = _dot(lo, wu_ref[:half, :]) + _dot(hi, wu_ref[half:, :])
        o_ref[...] = (_silu(g) * u).astype(o_ref.dtype)

    @pl.when(pl.program_id(1) >= used_ref[0])
    def _():
        o_ref[...] = jnp.zeros_like(o_ref)


def _glu_sorted_call(a_sorted, wg, wu, i_moe, tile_expert, n_used):
    n_slots, half = a_sorted.shape
    d = 2 * half
    de = wg.shape[-1]
    bm = SORT_TILE
    bn = de // 2
    w_spec = pl.BlockSpec((None, None, d, bn), lambda j, i, te, nu: (i_moe, te[i], 0, j))
    return pl.pallas_call(
        _glu_sorted_kernel,
        grid_spec=pltpu.PrefetchScalarGridSpec(
            num_scalar_prefetch=2,
            grid=(de // bn, n_slots // bm),
            in_specs=[pl.BlockSpec((bm, half), lambda j, i, te, nu: (i, 0)), w_spec, w_spec],
            out_specs=pl.BlockSpec((bm, bn), lambda j, i, te, nu: (i, j))),
        out_shape=jax.ShapeDtypeStruct((n_slots, de), BF16),
        compiler_params=_cparams(2),
        name="moe_glu",
    )(tile_expert, n_used, a_sorted, wg, wu)


def _down_sorted_kernel(te_ref, used_ref, a_ref, w_ref, o_ref):
    @pl.when(pl.program_id(1) < used_ref[0])
    def _():
        o_ref[...] = _dot(a_ref[...], w_ref[...].astype(BF16))

    @pl.when(pl.program_id(1) >= used_ref[0])
    def _():
        o_ref[...] = jnp.zeros_like(o_ref)


def _down_sorted_call(act, wd, i_moe, tile_expert, n_used):
    n_slots, de = act.shape
    d = wd.shape[-1]
    bm = SORT_TILE
    bn = 1024
    return pl.pallas_call(
        _down_sorted_kernel,
        grid_spec=pltpu.PrefetchScalarGridSpec(
            num_scalar_prefetch=2,
            grid=(d // bn, n_slots // bm),
            in_specs=[pl.BlockSpec((bm, de), lambda j, i, te, nu: (i, 0)),
                      pl.BlockSpec((None, None, de, bn), lambda j, i, te, nu: (i_moe, te[i], 0, j))],
            out_specs=pl.BlockSpec((bm, bn), lambda j, i, te, nu: (i, j))),
        out_shape=jax.ShapeDtypeStruct((n_slots, d), F32),
        compiler_params=_cparams(2),
        name="moe_down",
    )(tile_expert, n_used, act, wd)


def _combine_kernel(base_ref, e1c, e2c, r1c, r2c, e1n, e2n, r1n, r2n, y_ref, route_ref, x_ref, gt_ref,
                    o_ref, buf_ref, sem):
    i = pl.program_id(0)
    n = pl.num_programs(0)
    bm = x_ref.shape[0]

    def gather(e1_ref, e2_ref, r1_ref, r2_ref, buf_slot):
        def issue(j, carry):
            pltpu.make_async_copy(y_ref.at[pl.ds(_slot(base_ref, e1_ref, r1_ref, j), 1)],
                                  buf_ref.at[buf_slot, pl.ds(j, 1)], sem.at[buf_slot]).start()
            pltpu.make_async_copy(y_ref.at[pl.ds(_slot(base_ref, e2_ref, r2_ref, j), 1)],
                                  buf_ref.at[buf_slot, pl.ds(bm + j, 1)], sem.at[buf_slot]).start()
            return carry
        lax.fori_loop(0, bm, issue, 0, unroll=8)

    @pl.when(i == 0)
    def _():
        gather(e1c, e2c, r1c, r2c, 0)

    @pl.when(i + 1 < n)
    def _():
        gather(e1n, e2n, r1n, r2n, (i + 1) % 2)

    cur = i % 2
    pltpu.make_async_copy(y_ref.at[pl.ds(0, 2 * bm)], buf_ref.at[cur], sem.at[cur]).wait()

    rec = route_ref[...]
    w1 = rec[:, _R_W1:_R_W1 + 1]
    w2 = rec[:, _R_W2:_R_W2 + 1]
    y = w1 * buf_ref[cur, :bm, :] + w2 * buf_ref[cur, bm:, :]
    o_ref[...] = x_ref[...] + gt_ref[...] * y


def _combine_call(cfg, y_sorted, route, base, e1, e2, r1, r2, x_all, mod_l, n_rows):
    d = cfg.d
    bm = COMBINE_TILE
    bidx = cfg.bidx_for(bm)
    n_tiles = n_rows // bm
    cur = pl.BlockSpec((bm,), lambda i: (i,), memory_space=pltpu.SMEM)
    nxt = pl.BlockSpec((bm,), lambda i: (jnp.minimum(i + 1, n_tiles - 1),), memory_space=pltpu.SMEM)
    return pl.pallas_call(
        _combine_kernel,
        grid=(n_tiles,),
        in_specs=[pl.BlockSpec(memory_space=pltpu.SMEM), cur, cur, cur, cur, nxt, nxt, nxt, nxt,
                  pl.BlockSpec(memory_space=pl.ANY),
                  pl.BlockSpec((bm, LANES), lambda i: (i, 0)),
                  pl.BlockSpec((bm, d), lambda i: (i, 0)),
                  pl.BlockSpec((None, 1, d), lambda i: (bidx(i), 0, 5))],
        out_specs=pl.BlockSpec((bm, d), lambda i: (i, 0)),
        out_shape=jax.ShapeDtypeStruct((cfg.ntok, d), F32),
        scratch_shapes=[pltpu.VMEM((2, 2 * bm, d), F32), pltpu.SemaphoreType.DMA((2,))],
        input_output_aliases={11: 0},
        compiler_params=_cparams(1),
        name="moe_combine",
    )(base, e1, e2, r1, r2, e1, e2, r1, r2, y_sorted, route, x_all, mod_l)


def _moe_ffn(cfg, x_all, g, mod_l, w_router, wg, wu, wd, i_moe, n_rows):
    bm = SORT_TILE
    hp, route, counts = _route_call(cfg, x_all, g, mod_l, w_router, i_moe, n_rows)

    cnt = counts[0, :N_EXPERTS].astype(jnp.int32)
    padded = (cnt + bm - 1) // bm * bm
    ends = jnp.cumsum(padded)
    base = ends - padded
    n_tiles = 2 * n_rows // bm + N_EXPERTS
    tile_start = jnp.arange(n_tiles, dtype=jnp.int32) * bm
    tile_expert = jnp.minimum(jnp.sum((tile_start[:, None] >= ends[None, :]).astype(jnp.int32), axis=1),
                              N_EXPERTS - 1)
    n_used = (ends[-1:] // bm).astype(jnp.int32)
    e1, e2, r1, r2 = (route[:, k].astype(jnp.int32) for k in (_R_I1, _R_I2, _R_R1, _R_R2))

    a_sorted = _dispatch_call(hp, base, e1, e2, r1, r2, n_tiles * bm)
    act = _glu_sorted_call(a_sorted, wg, wu, i_moe, tile_expert, n_used)
    y_sorted = _down_sorted_call(act, wd, i_moe, tile_expert, n_used)
    return _combine_call(cfg, y_sorted, route, base, e1, e2, r1, r2, x_all, mod_l, n_rows)


def _softmax_pv(s, sink_col, v):
    m = jnp.maximum(jnp.max(s, axis=-1, keepdims=True), sink_col)
    p = jnp.exp(s - m)
    denom = jnp.sum(p, axis=-1, keepdims=True) + jnp.exp(sink_col - m)
    return _dot(p.astype(BF16), v) / denom


def _sink_column(sink_ref, group, rows):
    r = lax.broadcasted_iota(jnp.int32, (Q_PER_KV * rows, 1), 0)
    col = jnp.full((Q_PER_KV * rows, 1), sink_ref[group * Q_PER_KV], F32)
    for j in range(1, Q_PER_KV):
        col = jnp.where(r >= j * rows, sink_ref[group * Q_PER_KV + j], col)
    return col


def _attn_lat_kernel(sink_ref, *refs, kvh, t):
    q_refs = refs[:kvh]
    kp_ref, kc_ref, kn_ref, vp_ref, vc_ref, vn_ref, kx_ref, vx_ref, cos_ref, sin_ref, o_ref = refs[kvh:]
    n = pl.program_id(1)
    nb = t // BLOCK
    lane = lax.broadcasted_iota(jnp.int32, (BLOCK, HEAD_DIM), 1)
    first_half = (lane & (HEAD_DIM // 4)) == 0

    def rot(x, blk):
        start = pl.multiple_of(blk * BLOCK, BLOCK)
        c = cos_ref[pl.ds(start, BLOCK), :]
        s = sin_ref[pl.ds(start, BLOCK), :]
        partner = jnp.where(first_half, pltpu.roll(x, HEAD_DIM - HEAD_DIM // 4, 1),
                            pltpu.roll(x, HEAD_DIM // 4, 1))
        return x * c + partner * s

    blk_p = jnp.maximum(n - 1, 0)
    blk_n = jnp.minimum(n + 1, nb - 1)
    n_loc = 3 * BLOCK
    qi = lax.broadcasted_iota(jnp.int32, (BLOCK, n_loc), 0)
    kj = lax.broadcasted_iota(jnp.int32, (BLOCK, n_loc), 1)
    rel = kj - qi
    s_pos = n * BLOCK - BLOCK + kj
    valid = ((rel >= 0) & (rel <= 2 * BLOCK) & (s_pos >= 0) & (s_pos < t))[None]
    log2e = 1.0 / np.log(2.0)
    scale = HEAD_DIM ** -0.5 * log2e

    for g in range(kvh):
        hs = slice(g * HEAD_DIM, (g + 1) * HEAD_DIM)
        q = jnp.concatenate(
            [rot(q_refs[g][:, j * HEAD_DIM:(j + 1) * HEAD_DIM].astype(F32), n) * scale
             for j in range(Q_PER_KV)], axis=0).astype(BF16)
        k_loc = jnp.concatenate(
            [rot(kp_ref[:, hs].astype(F32), blk_p).astype(BF16),
             rot(kc_ref[:, hs].astype(F32), n).astype(BF16),
             rot(kn_ref[:, hs].astype(F32), blk_n).astype(BF16)], axis=0)
        v_loc = jnp.concatenate([vp_ref[:, hs], vc_ref[:, hs], vn_ref[:, hs]], axis=0)
        s_loc = _dot_nt(q, k_loc).reshape(Q_PER_KV, BLOCK, n_loc)
        s_loc = jnp.where(valid, s_loc, jnp.finfo(F32).min).reshape(Q_PER_KV * BLOCK, n_loc)
        s_ctx = _dot_nt(q, kx_ref[:, hs])
        sink_col = _sink_column(sink_ref, g, BLOCK) * log2e
        m = jnp.maximum(jnp.maximum(jnp.max(s_loc, axis=-1, keepdims=True),
                                    jnp.max(s_ctx, axis=-1, keepdims=True)), sink_col)
        p_loc = jnp.exp2(s_loc - m)
        p_ctx = jnp.exp2(s_ctx - m)
        denom = (jnp.sum(p_loc, axis=-1, keepdims=True) + jnp.sum(p_ctx, axis=-1, keepdims=True)
                 + jnp.exp2(sink_col - m))
        o = (_dot(p_loc.astype(BF16), v_loc) + _dot(p_ctx.astype(BF16), vx_ref[:, hs])) / denom
        for j in range(Q_PER_KV):
            h = g * Q_PER_KV + j
            o_ref[:, h * HEAD_DIM:(h + 1) * HEAD_DIM] = o[j * BLOCK:(j + 1) * BLOCK].astype(o_ref.dtype)


def _attn_lat_call(cfg, p, sink, cos_a, sin_a):
    b, t, l = cfg.b, cfg.t, cfg.l
    nb = t // BLOCK
    kvh = cfg.kvh
    gw = Q_PER_KV * HEAD_DIM
    kvw = kvh * HEAD_DIM
    q_blk = cfg.off_qa // gw
    k_blk, v_blk = cfg.off_ka // kvw, cfg.off_va // kvw
    ctx_row0 = (b * t) // l

    def kv_spec(col_blk, shift):
        def imap(bi, n):
            return (bi * nb + jnp.clip(n + shift, 0, nb - 1), col_blk)
        return pl.BlockSpec((BLOCK, kvw), imap)

    in_specs = [pl.BlockSpec(memory_space=pltpu.SMEM)]
    in_specs += [pl.BlockSpec((BLOCK, gw), functools.partial(lambda bi, n, g: (bi * nb + n, q_blk + g), g=g))
                 for g in range(kvh)]
    in_specs += [kv_spec(k_blk, -1), kv_spec(k_blk, 0), kv_spec(k_blk, 1),
                 kv_spec(v_blk, -1), kv_spec(v_blk, 0), kv_spec(v_blk, 1),
                 pl.BlockSpec((l, kvw), lambda bi, n: (ctx_row0 + bi, k_blk)),
                 pl.BlockSpec((l, kvw), lambda bi, n: (ctx_row0 + bi, v_blk)),
                 pl.BlockSpec((t, HEAD_DIM), lambda bi, n: (0, 0)),
                 pl.BlockSpec((t, HEAD_DIM), lambda bi, n: (0, 0))]
    return pl.pallas_call(
        functools.partial(_attn_lat_kernel, kvh=kvh, t=t),
        grid=(b, nb),
        in_specs=in_specs,
        out_specs=pl.BlockSpec((BLOCK, cfg.a_q), lambda bi, n: (bi * nb + n, 0)),
        out_shape=jax.ShapeDtypeStruct((b * t, cfg.a_q), BF16),
        compiler_params=_cparams(2),
        name="attn_latent",
    )(sink, *([p] * (kvh + 8)), cos_a, sin_a)


def _attn_ctx_kernel(sink_ref, *refs, kvh):
    q_refs = refs[:kvh]
    kx_ref, vx_ref, o_ref = refs[kvh:]
    l = kx_ref.shape[0]
    scale = HEAD_DIM ** -0.5
    for g in range(kvh):
        hs = slice(g * HEAD_DIM, (g + 1) * HEAD_DIM)
        q = jnp.concatenate(
            [(q_refs[g][:, j * HEAD_DIM:(j + 1) * HEAD_DIM].astype(F32) * scale).astype(BF16)
             for j in range(Q_PER_KV)], axis=0)
        s = _dot_nt(q, kx_ref[:, hs])
        o = _softmax_pv(s, _sink_column(sink_ref, g, l), vx_ref[:, hs])
        for j in range(Q_PER_KV):
            h = g * Q_PER_KV + j
            o_ref[:, h * HEAD_DIM:(h + 1) * HEAD_DIM] = o[j * l:(j + 1) * l].astype(o_ref.dtype)


def _attn_ctx_call(cfg, p, sink):
    b, t, l = cfg.b, cfg.t, cfg.l
    kvh = cfg.kvh
    gw = Q_PER_KV * HEAD_DIM
    kvw = kvh * HEAD_DIM
    q_blk = cfg.off_qa // gw
    k_blk, v_blk = cfg.off_ka // kvw, cfg.off_va // kvw
    ctx_row0 = (b * t) // l
    in_specs = [pl.BlockSpec(memory_space=pltpu.SMEM)]
    in_specs += [pl.BlockSpec((l, gw), functools.partial(lambda bi, g: (ctx_row0 + bi, q_blk + g), g=g))
                 for g in range(kvh)]
    in_specs += [pl.BlockSpec((l, kvw), lambda bi: (ctx_row0 + bi, k_blk)),
                 pl.BlockSpec((l, kvw), lambda bi: (ctx_row0 + bi, v_blk))]
    return pl.pallas_call(
        functools.partial(_attn_ctx_kernel, kvh=kvh),
        grid=(b,),
        in_specs=in_specs,
        out_specs=pl.BlockSpec((l, cfg.a_q), lambda bi: (bi, 0)),
        out_shape=jax.ShapeDtypeStruct((b * l, cfg.a_q), BF16),
        compiler_params=_cparams(1),
        name="attn_ctx",
    )(sink, *([p] * (kvh + 2)))


def _retention_scan(q_of, k_of, v_ref, g_ref, gn, lg_f, lg_b, o_ref, acc_ref, n_chunks, s_f, s_b):
    c = BLOCK
    ii = lax.broadcasted_iota(jnp.int32, (c, c), 0).astype(F32)
    jj = lax.broadcasted_iota(jnp.int32, (c, c), 1).astype(F32)
    diff = ii - jj
    decay = jnp.where(diff > 0, jnp.exp(lg_f * jnp.maximum(diff, 0.0)),
                      jnp.where(diff < 0, jnp.exp(lg_b * jnp.maximum(-diff, 0.0)), 2.0))
    zeta_f = jnp.exp(lg_f * (c - 1 - ii))
    xi_f = jnp.exp(lg_f * (ii + 1))
    zeta_b = jnp.exp(lg_b * ii)
    xi_b = jnp.exp(lg_b * (c - ii))
    cd_f = jnp.exp(jnp.concatenate([lg_f, lg_f], axis=1) * c)
    cd_b = jnp.exp(jnp.concatenate([lg_b, lg_b], axis=1) * c)

    for i in reversed(range(n_chunks)):
        rows = slice(i * c, (i + 1) * c)
        q, k, v = q_of(i), k_of(i), v_ref[rows, :]
        acc_ref[rows, :] = _dot((q * xi_b).astype(BF16), s_b.astype(BF16))
        s_b = cd_b * s_b + _dot_tn((k * zeta_b).astype(BF16), v)

    for i in range(n_chunks):
        rows = slice(i * c, (i + 1) * c)
        q, k, v = q_of(i), k_of(i), v_ref[rows, :]
        scores = _dot_nt(q.astype(BF16), k.astype(BF16)) * decay
        o = (acc_ref[rows, :] + _dot(scores.astype(BF16), v)
             + _dot((q * xi_f).astype(BF16), s_f.astype(BF16)))
        s_f = cd_f * s_f + _dot_tn((k * zeta_f).astype(BF16), v)
        mu = jnp.mean(o, axis=-1, keepdims=True)
        dev = o - mu
        var = jnp.mean(dev * dev, axis=-1, keepdims=True)
        on = dev * lax.rsqrt(var + EPS) * gn
        gate = g_ref[rows, :].astype(F32)
        o_ref[rows, :] = (_silu(gate) * on).astype(o_ref.dtype)
    return s_f, s_b


def _log_decay(a_row):
    return jnp.log1p(-jnp.exp2(-a_row))


def _ret_ctx_kernel(q_ref, k_ref, v_ref, g_ref, gn_ref, af_ref, ab_ref, o_ref, sf_ref, sb_ref, acc_ref):
    scale = HEAD_DIM ** -0.5
    n_chunks = q_ref.shape[0] // BLOCK

    def q_of(i):
        return q_ref[i * BLOCK:(i + 1) * BLOCK, :].astype(F32)

    def k_of(i):
        return k_ref[i * BLOCK:(i + 1) * BLOCK, :].astype(F32) * scale

    zero = jnp.zeros((HEAD_DIM, R_DV), F32)
    s_f, s_b = _retention_scan(q_of, k_of, v_ref, g_ref, gn_ref[...], _log_decay(af_ref[...]),
                               _log_decay(ab_ref[...]), o_ref, acc_ref, n_chunks, zero, zero)
    sf_ref[...] = s_f
    sb_ref[...] = s_b


def _ret_specs(cfg, rows, row_blk):
    qb, kb = cfg.off_qr // HEAD_DIM, cfg.off_kr // HEAD_DIM
    vb, gb = cfg.off_vr // R_DV, cfg.off_gr // R_DV
    return [pl.BlockSpec((rows, HEAD_DIM), lambda bi, h: (row_blk(bi), qb + h)),
            pl.BlockSpec((rows, HEAD_DIM), lambda bi, h: (row_blk(bi), kb + h)),
            pl.BlockSpec((rows, R_DV), lambda bi, h: (row_blk(bi), vb + h)),
            pl.BlockSpec((rows, R_DV), lambda bi, h: (row_blk(bi), gb + h)),
            pl.BlockSpec((1, R_DV), lambda bi, h: (0, h)),
            pl.BlockSpec((None, 1, HEAD_DIM), lambda bi, h: (h, 0, 0)),
            pl.BlockSpec((None, 1, HEAD_DIM), lambda bi, h: (h, 0, 0))]


def _ret_ctx_call(cfg, p, gn, a_f, a_b):
    b, t, l, rh = cfg.b, cfg.t, cfg.l, cfg.rh
    ctx_row0 = (b * t) // l
    state_spec = pl.BlockSpec((None, None, HEAD_DIM, R_DV), lambda bi, h: (bi, h, 0, 0))
    state_shape = jax.ShapeDtypeStruct((b, rh, HEAD_DIM, R_DV), F32)
    return pl.pallas_call(
        _ret_ctx_kernel,
        grid=(b, rh),
        in_specs=_ret_specs(cfg, l, lambda bi: ctx_row0 + bi),
        out_specs=[pl.BlockSpec((l, R_DV), lambda bi, h: (bi, h)), state_spec, state_spec],
        out_shape=[jax.ShapeDtypeStruct((b * l, cfg.r_v), BF16), state_shape, state_shape],
        scratch_shapes=[pltpu.VMEM((l, R_DV), F32)],
        compiler_params=_cparams(2),
        name="retention_ctx",
    )(p, p, p, p, gn, a_f, a_b)


def _ret_lat_kernel(q_ref, k_ref, v_ref, g_ref, gn_ref, af_ref, ab_ref, cos_ref, sin_ref, sf_ref, sb_ref,
                    o_ref, acc_ref, qr_ref, kr_ref):
    scale = HEAD_DIM ** -0.5
    n_chunks = q_ref.shape[0] // BLOCK
    cos, sin = cos_ref[...], sin_ref[...]

    def rot(x):
        return x * cos + pltpu.roll(x, HEAD_DIM // 2, 1) * sin

    qr_ref[...] = rot(q_ref[...].astype(F32))
    kr_ref[...] = rot(k_ref[...].astype(F32)) * scale

    def q_of(i):
        return qr_ref[i * BLOCK:(i + 1) * BLOCK, :]

    def k_of(i):
        return kr_ref[i * BLOCK:(i + 1) * BLOCK, :]

    _retention_scan(q_of, k_of, v_ref, g_ref, gn_ref[...], _log_decay(af_ref[...]), _log_decay(ab_ref[...]),
                    o_ref, acc_ref, n_chunks, sf_ref[...], sb_ref[...])


def _ret_lat_call(cfg, p, gn, a_f, a_b, cos_r, sin_r, s_f, s_b):
    b, t, rh = cfg.b, cfg.t, cfg.rh
    state_spec = pl.BlockSpec((None, None, HEAD_DIM, R_DV), lambda bi, h: (bi, h, 0, 0))
    table_spec = pl.BlockSpec((t, HEAD_DIM), lambda bi, h: (0, 0))
    in_specs = _ret_specs(cfg, t, lambda bi: bi) + [table_spec, table_spec, state_spec, state_spec]
    return pl.pallas_call(
        _ret_lat_kernel,
        grid=(b, rh),
        in_specs=in_specs,
        out_specs=pl.BlockSpec((t, R_DV), lambda bi, h: (bi, h)),
        out_shape=jax.ShapeDtypeStruct((b * t, cfg.r_v), BF16),
        scratch_shapes=[pltpu.VMEM((t, R_DV), F32), pltpu.VMEM((t, HEAD_DIM), F32),
                        pltpu.VMEM((t, HEAD_DIM), F32)],
        compiler_params=_cparams(2),
        name="retention_latent",
    )(p, p, p, p, gn, a_f, a_b, cos_r, sin_r, s_f, s_b)


def _rotary_tables(t):
    rows = t // GRID_W
    row = jnp.repeat(jnp.arange(rows, dtype=F32), GRID_W)
    col = jnp.tile(jnp.arange(GRID_W, dtype=F32), rows)
    nf = HEAD_DIM // 4
    inv = jnp.power(ROPE_BASE, -jnp.arange(nf, dtype=F32) / nf)
    ang_row, ang_col = row[:, None] * inv, col[:, None] * inv
    cos_a = jnp.concatenate([jnp.cos(ang_row)] * 2 + [jnp.cos(ang_col)] * 2, axis=1)
    sin_a = jnp.concatenate([-jnp.sin(ang_row), jnp.sin(ang_row), -jnp.sin(ang_col), jnp.sin(ang_col)], axis=1)
    nf = HEAD_DIM // 2
    inv = jnp.power(ROPE_BASE, -jnp.arange(nf, dtype=F32) / nf)
    ang = jnp.arange(t, dtype=F32)[:, None] * inv
    cos_r = jnp.concatenate([jnp.cos(ang)] * 2, axis=1)
    sin_r = jnp.concatenate([-jnp.sin(ang), jnp.sin(ang)], axis=1)
    return cos_a, sin_a, cos_r, sin_r


class _Config:
    def __init__(self, x, ctx, w_in, w_gate_d):
        self.b, self.t, self.d = x.shape
        self.l = ctx.shape[1]
        self.ntok = self.b * (self.t + self.l)
        self.ha = self.d // 256
        self.kvh = self.ha // Q_PER_KV
        self.rh = self.d // 256
        self.a_q = self.ha * HEAD_DIM
        self.a_kv = self.kvh * HEAD_DIM
        self.r_qk = self.rh * HEAD_DIM
        self.r_v = self.rh * R_DV
        self.off_ka = 0
        self.off_va = self.a_kv
        self.off_kr = 2 * self.a_kv
        self.off_vr = self.off_kr + self.r_qk
        self.off_qa = self.off_vr + self.r_v
        self.off_qr = self.off_qa + self.a_q
        self.off_gr = self.off_qr + self.r_qk
        self.off_za = self.off_gr + self.r_v
        self.off_zr = self.off_za + self.d
        assert w_in.shape[-1] == self.off_zr + self.d
        self.bm = min(ROW_TILE, self.t)
        self.norm_bm = min(NORM_ROW_TILE, self.t)
        assert self.t % self.bm == 0 and (self.b * self.l) % self.bm == 0
        assert self.t % GRID_W == 0 and self.t % BLOCK == 0 and self.l % BLOCK == 0

    def bidx_for(self, tile):
        assert self.t % tile == 0 and (self.b * self.l) % tile == 0
        tiles_per_batch = self.t // tile
        n_batch = self.b
        return lambda i: jnp.minimum(i // tiles_per_batch, n_batch)


def kernel(x, c, ctx, c_ctx, w_ada, b_ada, g_mix, g_ffn, w_in, attn_sink, ret_a_fwd, ret_a_bwd, ret_gn,
           w_br_attn, w_br_ret, w_out, w_gate_d, w_up_d, w_down_d, w_router, w_gate_e, w_up_e, w_down_e,
           g_final):
    cfg = _Config(x, ctx, w_in, w_gate_d)
    b, t, d, l = cfg.b, cfg.t, cfg.d, cfg.l
    depth = w_ada.shape[0]
    n_lat = b * t

    x_all = jnp.concatenate([x.reshape(n_lat, d), ctx.reshape(b * l, d)], axis=0)
    mod_rows = -(-(b + 1) // 8) * 8
    c_all = jnp.zeros((mod_rows, d), F32).at[:b].set(c).at[b].set(c_ctx)
    mod = _ada_call(c_all, w_ada, b_ada).reshape(depth, mod_rows, 1, 6 * d)

    cos_a, sin_a, cos_r, sin_r = _rotary_tables(t)
    w_br_attn_h, w_br_ret_h, w_out_h, w_down_d_h = (w.astype(BF16) for w in (w_br_attn, w_br_ret, w_out, w_down_d))
    w_gate_e_h, w_up_e_h = w_gate_e.astype(BF16), w_up_e.astype(BF16)
    w_router_h = jnp.pad(w_router, ((0, 0), (0, 0), (0, LANES - N_EXPERTS))).astype(BF16)
    a_f = jnp.broadcast_to(ret_a_fwd.astype(F32)[:, :, None, None], ret_a_fwd.shape + (1, HEAD_DIM))
    a_b = jnp.broadcast_to(ret_a_bwd.astype(F32)[:, :, None, None], ret_a_bwd.shape + (1, HEAD_DIM))

    for layer in range(depth):
        last = layer == depth - 1
        n_rows = n_lat if last else cfg.ntok
        mod_l = mod[layer]
        gn = ret_gn[layer].reshape(1, cfg.r_v)

        h = _norm_mod_call(cfg, x_all, g_mix[layer], mod_l, 0, cfg.ntok)
        p = _in_proj_call(cfg, h, w_in, layer, ctx_kv_only=last)
        y_a = _attn_lat_call(cfg, p, attn_sink[layer], cos_a, sin_a)
        y_r_ctx, s_f, s_b = _ret_ctx_call(cfg, p, gn, a_f[layer], a_b[layer])
        y_r = _ret_lat_call(cfg, p, gn, a_f[layer], a_b[layer], cos_r, sin_r, s_f, s_b)
        if last:
            y_a_ctx, y_r_ctx = y_a, y_r
        else:
            y_a_ctx = _attn_ctx_call(cfg, p, attn_sink[layer])
        m = _merge_call(cfg, y_a, y_r, y_a_ctx, y_r_ctx, w_br_attn_h, w_br_ret_h, layer, p, n_rows)
        x_all = _resid_call(cfg, m, w_out_h, layer, x_all, mod_l, 2, n_rows, bn=1024)

        i = layer // 2
        if layer % 2 == 0:
            hf = _norm_mod_call(cfg, x_all, g_ffn[layer], mod_l, 3, n_rows)
            act = _glu_call(cfg, hf, w_gate_d, w_up_d, i, n_rows)
            x_all = _resid_call(cfg, act, w_down_d_h, i, x_all, mod_l, 5, n_rows, bn=512)
        else:
            x_all = _moe_ffn(cfg, x_all, g_ffn[layer], mod_l, w_router_h, w_gate_e_h, w_up_e_h, w_down_e,
                             i, n_rows)

    return _final_norm_call(cfg, x_all, g_final).reshape(b, t, d)
```

```python
import functools

import numpy as np
import jax
import jax.numpy as jnp
from jax import lax
from jax.experimental import pallas as pl
from jax.experimental.pallas import tpu as pltpu

F32 = jnp.float32
BF16 = jnp.bfloat16

EPS = 1e-6
ROPE_BASE = 10000.0
GRID_W = 64
HEAD_DIM = 128
R_DV = 256
BLOCK = 128
Q_PER_KV = 4
N_EXPERTS = 8
LANES = 128

V7X_VMEM_BYTES = 64 * 1024 * 1024
VMEM_LIMIT = V7X_VMEM_BYTES - 8 * 1024 * 1024
ROW_TILE = 1024
NORM_ROW_TILE = 512
SORT_TILE = 512
DISPATCH_TILE = 512
COMBINE_TILE = 256
GLU_COL_BLOCKS = 2
U32 = jnp.uint32


def _cparams(n_axes):
    return pltpu.CompilerParams(dimension_semantics=("arbitrary",) * n_axes,
                                vmem_limit_bytes=VMEM_LIMIT)


def _dot(a, b):
    return jnp.dot(a, b, preferred_element_type=F32)


def _dot_nt(a, b):
    return lax.dot_general(a, b, (((1,), (1,)), ((), ())), preferred_element_type=F32)


def _dot_tn(a, b):
    return lax.dot_general(a, b, (((0,), (0,)), ((), ())), preferred_element_type=F32)


def _silu(x):
    return x * jax.nn.sigmoid(x)


def _ada_kernel(c_ref, w_ref, b_ref, o_ref):
    s = _silu(c_ref[...]).astype(BF16)
    o_ref[...] = _dot(s, w_ref[...].astype(BF16)) + b_ref[...]


def _ada_call(c_all, w_ada, b_ada):
    depth, d, n6 = w_ada.shape
    r = c_all.shape[0]
    bn = 1024
    return pl.pallas_call(
        _ada_kernel,
        grid=(depth, n6 // bn),
        in_specs=[pl.BlockSpec((r, d), lambda l, j: (0, 0)),
                  pl.BlockSpec((None, d, bn), lambda l, j: (l, 0, j)),
                  pl.BlockSpec((None, 1, bn), lambda l, j: (l, 0, j))],
        out_specs=pl.BlockSpec((None, r, bn), lambda l, j: (l, 0, j)),
        out_shape=jax.ShapeDtypeStruct((depth, r, n6), F32),
        compiler_params=_cparams(2),
        name="ada_mod",
    )(c_all, w_ada, b_ada.reshape(depth, 1, n6))


def _norm_mod_kernel(x_ref, g_ref, sh_ref, sc_ref, o_ref):
    x = x_ref[...]
    y = x * lax.rsqrt(jnp.mean(x * x, axis=-1, keepdims=True) + EPS) * g_ref[...]
    o_ref[...] = (y * (1.0 + sc_ref[...]) + sh_ref[...]).astype(o_ref.dtype)


def _norm_kernel(x_ref, g_ref, o_ref):
    x = x_ref[...]
    o_ref[...] = x * lax.rsqrt(jnp.mean(x * x, axis=-1, keepdims=True) + EPS) * g_ref[...]


def _two_stream(kernel_fn, lat_tiles):
    def wrapped(x_lat_ref, x_ctx_ref, *rest):
        @pl.when(pl.program_id(0) < lat_tiles)
        def _():
            kernel_fn(x_lat_ref, *rest)

        @pl.when(pl.program_id(0) >= lat_tiles)
        def _():
            kernel_fn(x_ctx_ref, *rest)
    return wrapped


def _norm_mod_call(cfg, x_src, g, mod_l, shift_chunk, n_rows):
    d, bm = cfg.d, cfg.norm_bm
    bidx = cfg.bidx_for(bm)
    if isinstance(x_src, tuple):
        lat_tiles = (cfg.b * cfg.t) // bm
        kern = _two_stream(_norm_mod_kernel, lat_tiles)
        x_specs = [pl.BlockSpec((bm, d), lambda i: (jnp.minimum(i, lat_tiles - 1), 0)),
                   pl.BlockSpec((bm, d), lambda i: (jnp.maximum(i - lat_tiles, 0), 0))]
        x_args = list(x_src)
    else:
        kern, x_specs, x_args = _norm_mod_kernel, [pl.BlockSpec((bm, d), lambda i: (i, 0))], [x_src]
    return pl.pallas_call(
        kern,
        grid=(n_rows // bm,),
        in_specs=x_specs + [pl.BlockSpec((1, d), lambda i: (0, 0)),
                            pl.BlockSpec((None, 1, d), lambda i: (bidx(i), 0, shift_chunk)),
                            pl.BlockSpec((None, 1, d), lambda i: (bidx(i), 0, shift_chunk + 1))],
        out_specs=pl.BlockSpec((bm, d), lambda i: (i, 0)),
        out_shape=jax.ShapeDtypeStruct((n_rows, d), BF16),
        compiler_params=_cparams(1),
        name="norm_mod",
    )(*x_args, g.reshape(1, d), mod_l, mod_l)


def _final_norm_call(cfg, x_all, g):
    d, bm = cfg.d, cfg.norm_bm
    n_rows = cfg.b * cfg.t
    return pl.pallas_call(
        _norm_kernel,
        grid=(n_rows // bm,),
        in_specs=[pl.BlockSpec((bm, d), lambda i: (i, 0)),
                  pl.BlockSpec((1, d), lambda i: (0, 0))],
        out_specs=pl.BlockSpec((bm, d), lambda i: (i, 0)),
        out_shape=jax.ShapeDtypeStruct((n_rows, d), F32),
        compiler_params=_cparams(1),
        name="final_norm",
    )(x_all, g.reshape(1, d))


def _mm_kernel(a_ref, w_ref, o_ref):
    o_ref[...] = _dot(a_ref[...], w_ref[...].astype(BF16)).astype(o_ref.dtype)


def _wide_tile(cfg, n_rows):
    return 2 * cfg.bm if n_rows % (2 * cfg.bm) == 0 else cfg.bm


def _mm_ctx_cols_kernel(a_ref, w_ref, o_ref, *, lat_tiles, ctx_blocks):
    unused = (pl.program_id(0) >= lat_tiles) & (pl.program_id(1) >= ctx_blocks)

    @pl.when(jnp.logical_not(unused))
    def _():
        _mm_kernel(a_ref, w_ref, o_ref)

    @pl.when(unused)
    def _():
        o_ref[...] = jnp.zeros_like(o_ref)


def _in_proj_call(cfg, h, w, layer, ctx_kv_only):
    d, bm = cfg.d, _wide_tile(cfg, cfg.ntok)
    n = w.shape[-1]
    bn = 512
    kern = _mm_kernel
    if ctx_kv_only and (cfg.b * cfg.t) % bm == 0 and cfg.off_qa % bn == 0:
        kern = functools.partial(_mm_ctx_cols_kernel, lat_tiles=(cfg.b * cfg.t) // bm,
                                 ctx_blocks=cfg.off_qa // bn)
    return pl.pallas_call(
        kern,
        grid=(cfg.ntok // bm, n // bn),
        in_specs=[pl.BlockSpec((bm, d), lambda i, j: (i, 0)),
                  pl.BlockSpec((None, d, bn), lambda i, j: (layer, 0, j))],
        out_specs=pl.BlockSpec((bm, bn), lambda i, j: (i, j)),
        out_shape=jax.ShapeDtypeStruct((cfg.ntok, n), BF16),
        compiler_params=_cparams(2),
        name="in_proj",
    )(h, w)


def _merge_kernel(ya_ref, yr_ref, yac_ref, yrc_ref, wa_ref, wr_ref, za_ref, zr_ref, o_ref, *, lat_tiles):
    def merge(ya, yr):
        a = _dot(ya[...], wa_ref[...])
        r = _dot(yr[...], wr_ref[...])
        za = za_ref[...].astype(F32)
        zr = zr_ref[...].astype(F32)
        o_ref[...] = (jax.nn.sigmoid(za) * a + jax.nn.sigmoid(zr) * r).astype(o_ref.dtype)

    i = pl.program_id(0)

    @pl.when(i < lat_tiles)
    def _():
        merge(ya_ref, yr_ref)

    @pl.when(i >= lat_tiles)
    def _():
        merge(yac_ref, yrc_ref)


def _merge_call(cfg, y_a, y_r, y_a_ctx, y_r_ctx, w_a, w_r, layer, p, n_rows):
    d, bm = cfg.d, cfg.bm
    bn = 512
    za_blk, zr_blk = cfg.off_za // bn, cfg.off_zr // bn
    lat_tiles = (cfg.b * cfg.t) // bm

    def lat(i, j):
        return (jnp.minimum(i, lat_tiles - 1), 0)

    def ctx(i, j):
        return (jnp.maximum(i - lat_tiles, 0), 0)

    return pl.pallas_call(
        functools.partial(_merge_kernel, lat_tiles=lat_tiles),
        grid=(n_rows // bm, d // bn),
        in_specs=[pl.BlockSpec((bm, cfg.a_q), lat),
                  pl.BlockSpec((bm, cfg.r_v), lat),
                  pl.BlockSpec((bm, cfg.a_q), ctx),
                  pl.BlockSpec((bm, cfg.r_v), ctx),
                  pl.BlockSpec((None, cfg.a_q, bn), lambda i, j: (layer, 0, j)),
                  pl.BlockSpec((None, cfg.r_v, bn), lambda i, j: (layer, 0, j)),
                  pl.BlockSpec((bm, bn), lambda i, j: (i, za_blk + j)),
                  pl.BlockSpec((bm, bn), lambda i, j: (i, zr_blk + j))],
        out_specs=pl.BlockSpec((bm, bn), lambda i, j: (i, j)),
        out_shape=jax.ShapeDtypeStruct((n_rows, d), BF16),
        compiler_params=_cparams(2),
        name="merge",
    )(y_a, y_r, y_a_ctx, y_r_ctx, w_a, w_r, p, p)


def _resid_kernel(a_ref, w_ref, x_ref, gt_ref, o_ref):
    o_ref[...] = x_ref[...] + gt_ref[...] * _dot(a_ref[...], w_ref[...])


def _resid_two_stream_kernel(a_ref, w_ref, x_lat_ref, x_ctx_ref, gt_ref, o_ref, *, lat_tiles):
    @pl.when(pl.program_id(0) < lat_tiles)
    def _():
        _resid_kernel(a_ref, w_ref, x_lat_ref, gt_ref, o_ref)

    @pl.when(pl.program_id(0) >= lat_tiles)
    def _():
        _resid_kernel(a_ref, w_ref, x_ctx_ref, gt_ref, o_ref)


def _resid_call(cfg, a, w, widx, x_src, mod_l, gate_chunk, n_rows, bn):
    d, bm = cfg.d, cfg.bm
    k = a.shape[1]
    bidx = cfg.bidx_for(bm)
    gblk = gate_chunk * (d // bn)
    if isinstance(x_src, tuple):
        lat_tiles = (cfg.b * cfg.t) // bm
        kern = functools.partial(_resid_two_stream_kernel, lat_tiles=lat_tiles)
        x_specs = [pl.BlockSpec((bm, bn), lambda i, j: (jnp.minimum(i, lat_tiles - 1),
                                                        jnp.where(i < lat_tiles, j, 0))),
                   pl.BlockSpec((bm, bn), lambda i, j: (jnp.maximum(i - lat_tiles, 0),
                                                        jnp.where(i >= lat_tiles, j, 0)))]
        x_args, aliases, out_rows = list(x_src), {}, n_rows
    else:
        kern, x_specs = _resid_kernel, [pl.BlockSpec((bm, bn), lambda i, j: (i, j))]
        x_args, aliases, out_rows = [x_src], {2: 0}, x_src.shape[0]
    return pl.pallas_call(
        kern,
        grid=(n_rows // bm, d // bn),
        in_specs=[pl.BlockSpec((bm, k), lambda i, j: (i, 0)),
                  pl.BlockSpec((None, k, bn), lambda i, j: (widx, 0, j))] + x_specs
                 + [pl.BlockSpec((None, 1, bn), lambda i, j: (bidx(i), 0, gblk + j))],
        out_specs=pl.BlockSpec((bm, bn), lambda i, j: (i, j)),
        out_shape=jax.ShapeDtypeStruct((out_rows, d), F32),
        input_output_aliases=aliases,
        compiler_params=_cparams(2),
        name="resid_proj",
    )(a, w, *x_args, mod_l)


def _glu_kernel(a_ref, wg_ref, wu_ref, o_ref):
    a = a_ref[...]
    g = _dot(a, wg_ref[...].astype(BF16))
    u = _dot(a, wu_ref[...].astype(BF16))
    o_ref[...] = (_silu(g) * u).astype(o_ref.dtype)


def _glu_call(cfg, h, wg, wu, widx, n_rows):
    d, bm = cfg.d, _wide_tile(cfg, n_rows)
    n = wg.shape[-1]
    bn = 512
    w_spec = pl.BlockSpec((None, d, bn), lambda i, j: (widx, 0, j))
    return pl.pallas_call(
        _glu_kernel,
        grid=(n_rows // bm, n // bn),
        in_specs=[pl.BlockSpec((bm, d), lambda i, j: (i, 0)), w_spec, w_spec],
        out_specs=pl.BlockSpec((bm, bn), lambda i, j: (i, j)),
        out_shape=jax.ShapeDtypeStruct((n_rows, n), BF16),
        compiler_params=_cparams(2),
        name="glu_up",
    )(h, wg, wu)


_R_I1, _R_I2, _R_W1, _R_W2, _R_R1, _R_R2 = range(6)


def _pack_halves(h):
    half = h.shape[1] // 2
    lo = lax.bitcast_convert_type(h[:, :half].astype(F32), U32)
    hi = lax.bitcast_convert_type(h[:, half:].astype(F32), U32)
    return (hi & jnp.uint32(0xFFFF0000)) | lax.shift_right_logical(lo, jnp.uint32(16))


def _unpack_halves(packed):
    lo = lax.bitcast_convert_type(lax.shift_left(packed, jnp.uint32(16)), F32)
    hi = lax.bitcast_convert_type(packed & jnp.uint32(0xFFFF0000), F32)
    return lo.astype(BF16), hi.astype(BF16)


def _route_kernel(x_ref, g_ref, sh_ref, sc_ref, wr_ref, hp_ref, route_ref, cnt_ref, carry_ref):
    @pl.when(pl.program_id(0) == 0)
    def _():
        carry_ref[...] = jnp.zeros_like(carry_ref)

    x = x_ref[...]
    y = x * lax.rsqrt(jnp.mean(x * x, axis=-1, keepdims=True) + EPS) * g_ref[...]
    h = (y * (1.0 + sc_ref[...]) + sh_ref[...]).astype(BF16)
    hp_ref[...] = _pack_halves(h)

    logits = _dot(h, wr_ref[...])
    rows = logits.shape[0]
    lane = lax.broadcasted_iota(jnp.int32, logits.shape, 1).astype(F32)
    neg = -jnp.inf
    lg = jnp.where(lane < N_EXPERTS, logits, neg)
    m1 = jnp.max(lg, axis=-1, keepdims=True)
    i1 = jnp.min(jnp.where(lg == m1, lane, float(LANES)), axis=-1, keepdims=True)
    lg2 = jnp.where(lane == i1, neg, lg)
    m2 = jnp.max(lg2, axis=-1, keepdims=True)
    i2 = jnp.min(jnp.where(lg2 == m2, lane, float(LANES)), axis=-1, keepdims=True)
    e = jnp.exp(m2 - m1)
    w1 = 1.0 / (1.0 + e)
    w2 = e / (1.0 + e)

    chosen = jnp.where((lane == i1) | (lane == i2), 1.0, 0.0)
    rr = lax.broadcasted_iota(jnp.int32, (rows, rows), 0)
    cc = lax.broadcasted_iota(jnp.int32, (rows, rows), 1)
    earlier = jnp.where(cc < rr, 1.0, 0.0).astype(BF16)
    rank = _dot(earlier, chosen.astype(BF16)) + carry_ref[...]
    r1 = jnp.sum(jnp.where(lane == i1, rank, 0.0), axis=-1, keepdims=True)
    r2 = jnp.sum(jnp.where(lane == i2, rank, 0.0), axis=-1, keepdims=True)
    carry_ref[...] += jnp.sum(chosen, axis=0, keepdims=True)
    cnt_ref[...] = carry_ref[...]

    rec = jnp.zeros_like(logits)
    for slot, val in ((_R_I1, i1), (_R_I2, i2), (_R_W1, w1), (_R_W2, w2), (_R_R1, r1), (_R_R2, r2)):
        rec = jnp.where(lane == slot, val, rec)
    route_ref[...] = rec


def _route_call(cfg, x_all, g, mod_l, w_router, i_moe, n_rows):
    d, bm = cfg.d, cfg.norm_bm
    bidx = cfg.bidx_for(bm)
    return pl.pallas_call(
        _route_kernel,
        grid=(n_rows // bm,),
        in_specs=[pl.BlockSpec((bm, d), lambda i: (i, 0)),
                  pl.BlockSpec((1, d), lambda i: (0, 0)),
                  pl.BlockSpec((None, 1, d), lambda i: (bidx(i), 0, 3)),
                  pl.BlockSpec((None, 1, d), lambda i: (bidx(i), 0, 4)),
                  pl.BlockSpec((None, d, LANES), lambda i: (i_moe, 0, 0))],
        out_specs=[pl.BlockSpec((bm, d // 2), lambda i: (i, 0)),
                   pl.BlockSpec((bm, LANES), lambda i: (i, 0)),
                   pl.BlockSpec((1, LANES), lambda i: (0, 0))],
        out_shape=[jax.ShapeDtypeStruct((n_rows, d // 2), U32),
                   jax.ShapeDtypeStruct((n_rows, LANES), F32),
                   jax.ShapeDtypeStruct((1, LANES), F32)],
        scratch_shapes=[pltpu.VMEM((1, LANES), F32)],
        compiler_params=_cparams(1),
        name="route",
    )(x_all, g.reshape(1, d), mod_l, mod_l, w_router)


def _slot(base_ref, e_ref, r_ref, j):
    return base_ref[e_ref[j]] + r_ref[j]


def _dispatch_kernel(base_ref, e1_ref, e2_ref, r1_ref, r2_ref, src_ref, _, dst_ref, sem):
    tile = e1_ref.shape[0]

    def issue(j, carry):
        src = src_ref.at[pl.ds(j, 1)]
        pltpu.make_async_copy(src, dst_ref.at[pl.ds(_slot(base_ref, e1_ref, r1_ref, j), 1)], sem).start(priority=0)
        pltpu.make_async_copy(src, dst_ref.at[pl.ds(_slot(base_ref, e2_ref, r2_ref, j), 1)], sem).start(priority=1)
        return carry

    lax.fori_loop(0, tile, issue, 0, unroll=8)

    for _ in range(2):
        pltpu.make_async_copy(src_ref, dst_ref.at[pl.ds(0, tile)], sem).wait()


def _dispatch_call(hp, base, e1, e2, r1, r2, n_slots):
    n_rows, half = hp.shape
    tile = DISPATCH_TILE
    smem_tile = pl.BlockSpec((tile,), lambda i: (i,), memory_space=pltpu.SMEM)
    return pl.pallas_call(
        _dispatch_kernel,
        grid=(n_rows // tile,),
        in_specs=[pl.BlockSpec(memory_space=pltpu.SMEM), smem_tile, smem_tile, smem_tile, smem_tile,
                  pl.BlockSpec((tile, half), lambda i: (i, 0)), pl.BlockSpec(memory_space=pl.ANY)],
        out_specs=pl.BlockSpec(memory_space=pl.ANY),
        out_shape=jax.ShapeDtypeStruct((n_slots, half), U32),
        scratch_shapes=[pltpu.SemaphoreType.DMA(())],
        input_output_aliases={6: 0},
        compiler_params=_cparams(1),
        name="moe_dispatch",
    )(base, e1, e2, r1, r2, hp, jnp.zeros((n_slots, half), U32))


def _glu_sorted_kernel(te_ref, used_ref, a_ref, w_ref, o_ref):
    @pl.when(pl.program_id(1) < used_ref[0])
    def _():
        lo, hi = _unpack_halves(a_ref[...])
        half = lo.shape[1]
        bn = o_ref.shape[1]
        gu = _dot(lo, w_ref[:half, :]) + _dot(hi, w_ref[half:, :])
        o_ref[...] = (_silu(gu[:, :bn]) * gu[:, bn:]).astype(o_ref.dtype)

    @pl.when(pl.program_id(1) >= used_ref[0])
    def _():
        o_ref[...] = jnp.zeros_like(o_ref)


def _interleave_gate_up(wg, wu):
    lead, de = wg.shape[:-1], wg.shape[-1]
    bn = de // GLU_COL_BLOCKS
    g = wg.reshape(lead + (GLU_COL_BLOCKS, 1, bn))
    u = wu.reshape(lead + (GLU_COL_BLOCKS, 1, bn))
    return jnp.concatenate([g, u], axis=-2).reshape(lead + (2 * de,)).astype(BF16)


def _glu_sorted_call(a_sorted, w_gu, i_moe, tile_expert, n_used):
    n_slots, half = a_sorted.shape
    d = 2 * half
    de = w_gu.shape[-1] // 2
    bm = SORT_TILE
    bn = de // GLU_COL_BLOCKS
    return pl.pallas_call(
        _glu_sorted_kernel,
        grid_spec=pltpu.PrefetchScalarGridSpec(
            num_scalar_prefetch=2,
            grid=(GLU_COL_BLOCKS, n_slots // bm),
            in_specs=[pl.BlockSpec((bm, half), lambda j, i, te, nu: (i, 0)),
                      pl.BlockSpec((None, None, d, 2 * bn), lambda j, i, te, nu: (i_moe, te[i], 0, j))],
            out_specs=pl.BlockSpec((bm, bn), lambda j, i, te, nu: (i, j))),
        out_shape=jax.ShapeDtypeStruct((n_slots, de), BF16),
        compiler_params=_cparams(2),
        name="moe_glu",
    )(tile_expert, n_used, a_sorted, w_gu)


def _down_sorted_kernel(te_ref, used_ref, a_ref, w_ref, o_ref):
    @pl.when(pl.program_id(1) < used_ref[0])
    def _():
        o_ref[...] = _dot(a_ref[...], w_ref[...].astype(BF16))

    @pl.when(pl.program_id(1) >= used_ref[0])
    def _():
        o_ref[...] = jnp.zeros_like(o_ref)


def _down_sorted_call(act, wd, i_moe, tile_expert, n_used):
    n_slots, de = act.shape
    d = wd.shape[-1]
    bm = SORT_TILE
    bn = 1024
    return pl.pallas_call(
        _down_sorted_kernel,
        grid_spec=pltpu.PrefetchScalarGridSpec(
            num_scalar_prefetch=2,
            grid=(d // bn, n_slots // bm),
            in_specs=[pl.BlockSpec((bm, de), lambda j, i, te, nu: (i, 0)),
                      pl.BlockSpec((None, None, de, bn), lambda j, i, te, nu: (i_moe, te[i], 0, j))],
            out_specs=pl.BlockSpec((bm, bn), lambda j, i, te, nu: (i, j))),
        out_shape=jax.ShapeDtypeStruct((n_slots, d), F32),
        compiler_params=_cparams(2),
        name="moe_down",
    )(tile_expert, n_used, act, wd)


def _combine_kernel(base_ref, e1c, e2c, r1c, r2c, e1n, e2n, r1n, r2n, y_ref, route_ref, x_ref, gt_ref,
                    o_ref, buf_ref, sem):
    i = pl.program_id(0)
    n = pl.num_programs(0)
    bm = x_ref.shape[0]

    def gather(e1_ref, e2_ref, r1_ref, r2_ref, buf_slot):
        def issue(j, carry):
            pltpu.make_async_copy(y_ref.at[pl.ds(_slot(base_ref, e1_ref, r1_ref, j), 1)],
                                  buf_ref.at[buf_slot, pl.ds(j, 1)], sem.at[buf_slot]).start(priority=0)
            pltpu.make_async_copy(y_ref.at[pl.ds(_slot(base_ref, e2_ref, r2_ref, j), 1)],
                                  buf_ref.at[buf_slot, pl.ds(bm + j, 1)], sem.at[buf_slot]).start(priority=1)
            return carry
        lax.fori_loop(0, bm, issue, 0, unroll=8)

    @pl.when(i == 0)
    def _():
        gather(e1c, e2c, r1c, r2c, 0)

    @pl.when(i + 1 < n)
    def _():
        gather(e1n, e2n, r1n, r2n, (i + 1) % 2)

    cur = i % 2
    pltpu.make_async_copy(y_ref.at[pl.ds(0, 2 * bm)], buf_ref.at[cur], sem.at[cur]).wait()

    rec = route_ref[...]
    w1 = rec[:, _R_W1:_R_W1 + 1]
    w2 = rec[:, _R_W2:_R_W2 + 1]
    y = w1 * buf_ref[cur, :bm, :] + w2 * buf_ref[cur, bm:, :]
    o_ref[...] = x_ref[...] + gt_ref[...] * y


def _combine_call(cfg, y_sorted, route, base, e1, e2, r1, r2, x_all, mod_l, n_rows):
    d = cfg.d
    bm = COMBINE_TILE
    bidx = cfg.bidx_for(bm)
    n_tiles = n_rows // bm
    cur = pl.BlockSpec((bm,), lambda i: (i,), memory_space=pltpu.SMEM)
    nxt = pl.BlockSpec((bm,), lambda i: (jnp.minimum(i + 1, n_tiles - 1),), memory_space=pltpu.SMEM)
    return pl.pallas_call(
        _combine_kernel,
        grid=(n_tiles,),
        in_specs=[pl.BlockSpec(memory_space=pltpu.SMEM), cur, cur, cur, cur, nxt, nxt, nxt, nxt,
                  pl.BlockSpec(memory_space=pl.ANY),
                  pl.BlockSpec((bm, LANES), lambda i: (i, 0)),
                  pl.BlockSpec((bm, d), lambda i: (i, 0)),
                  pl.BlockSpec((None, 1, d), lambda i: (bidx(i), 0, 5))],
        out_specs=pl.BlockSpec((bm, d), lambda i: (i, 0)),
        out_shape=jax.ShapeDtypeStruct(x_all.shape, F32),
        scratch_shapes=[pltpu.VMEM((2, 2 * bm, d), F32), pltpu.SemaphoreType.DMA((2,))],
        input_output_aliases={11: 0},
        compiler_params=_cparams(1),
        name="moe_combine",
    )(base, e1, e2, r1, r2, e1, e2, r1, r2, y_sorted, route, x_all, mod_l)


def _moe_ffn(cfg, x_all, g, mod_l, w_router, w_gu, wd, i_moe, n_rows):
    bm = SORT_TILE
    hp, route, counts = _route_call(cfg, x_all, g, mod_l, w_router, i_moe, n_rows)

    cnt = counts[0, :N_EXPERTS].astype(jnp.int32)
    padded = (cnt + bm - 1) // bm * bm
    ends = jnp.cumsum(padded)
    base = ends - padded
    n_tiles = 2 * n_rows // bm + N_EXPERTS
    tile_start = jnp.arange(n_tiles, dtype=jnp.int32) * bm
    tile_expert = jnp.minimum(jnp.sum((tile_start[:, None] >= ends[None, :]).astype(jnp.int32), axis=1),
                              N_EXPERTS - 1)
    n_used = (ends[-1:] // bm).astype(jnp.int32)
    e1, e2, r1, r2 = (route[:, k].astype(jnp.int32) for k in (_R_I1, _R_I2, _R_R1, _R_R2))

    a_sorted = _dispatch_call(hp, base, e1, e2, r1, r2, n_tiles * bm)
    act = _glu_sorted_call(a_sorted, w_gu, i_moe, tile_expert, n_used)
    y_sorted = _down_sorted_call(act, wd, i_moe, tile_expert, n_used)
    return _combine_call(cfg, y_sorted, route, base, e1, e2, r1, r2, x_all, mod_l, n_rows)


def _softmax_pv(s, sink_col, v):
    m = jnp.maximum(jnp.max(s, axis=-1, keepdims=True), sink_col)
    p = jnp.exp(s - m)
    denom = jnp.sum(p, axis=-1, keepdims=True) + jnp.exp(sink_col - m)
    return _dot(p.astype(BF16), v) / denom


def _sink_column(sink_ref, group, rows):
    r = lax.broadcasted_iota(jnp.int32, (Q_PER_KV * rows, 1), 0)
    col = jnp.full((Q_PER_KV * rows, 1), sink_ref[group * Q_PER_KV], F32)
    for j in range(1, Q_PER_KV):
        col = jnp.where(r >= j * rows, sink_ref[group * Q_PER_KV + j], col)
    return col


def _attn_lat_kernel(sink_ref, *refs, kvh, t):
    q_refs = refs[:kvh]
    kp_ref, kc_ref, kn_ref, vp_ref, vc_ref, vn_ref, kx_ref, vx_ref, cos_ref, sin_ref, o_ref = refs[kvh:]
    n = pl.program_id(1)
    nb = t // BLOCK
    lane = lax.broadcasted_iota(jnp.int32, (BLOCK, HEAD_DIM), 1)
    first_half = (lane & (HEAD_DIM // 4)) == 0

    def rot(x, blk):
        start = pl.multiple_of(blk * BLOCK, BLOCK)
        c = cos_ref[pl.ds(start, BLOCK), :]
        s = sin_ref[pl.ds(start, BLOCK), :]
        partner = jnp.where(first_half, pltpu.roll(x, HEAD_DIM - HEAD_DIM // 4, 1),
                            pltpu.roll(x, HEAD_DIM // 4, 1))
        return x * c + partner * s

    blk_p = jnp.maximum(n - 1, 0)
    blk_n = jnp.minimum(n + 1, nb - 1)
    n_loc = 3 * BLOCK
    qi = lax.broadcasted_iota(jnp.int32, (BLOCK, n_loc), 0)
    kj = lax.broadcasted_iota(jnp.int32, (BLOCK, n_loc), 1)
    rel = kj - qi
    s_pos = n * BLOCK - BLOCK + kj
    valid = ((rel >= 0) & (rel <= 2 * BLOCK) & (s_pos >= 0) & (s_pos < t))[None]
    log2e = 1.0 / np.log(2.0)
    scale = HEAD_DIM ** -0.5 * log2e

    for g in range(kvh):
        hs = slice(g * HEAD_DIM, (g + 1) * HEAD_DIM)
        q = jnp.concatenate(
            [rot(q_refs[g][:, j * HEAD_DIM:(j + 1) * HEAD_DIM].astype(F32), n) * scale
             for j in range(Q_PER_KV)], axis=0).astype(BF16)
        k_loc = jnp.concatenate(
            [rot(kp_ref[:, hs].astype(F32), blk_p).astype(BF16),
             rot(kc_ref[:, hs].astype(F32), n).astype(BF16),
             rot(kn_ref[:, hs].astype(F32), blk_n).astype(BF16)], axis=0)
        v_loc = jnp.concatenate([vp_ref[:, hs], vc_ref[:, hs], vn_ref[:, hs]], axis=0)
        s_loc = _dot_nt(q, k_loc).reshape(Q_PER_KV, BLOCK, n_loc)
        s_loc = jnp.where(valid, s_loc, jnp.finfo(F32).min).reshape(Q_PER_KV * BLOCK, n_loc)
        s_ctx = _dot_nt(q, kx_ref[:, hs])
        sink_col = _sink_column(sink_ref, g, BLOCK) * log2e
        m = jnp.maximum(jnp.maximum(jnp.max(s_loc, axis=-1, keepdims=True),
                                    jnp.max(s_ctx, axis=-1, keepdims=True)), sink_col)
        p_loc = jnp.exp2(s_loc - m)
        p_ctx = jnp.exp2(s_ctx - m)
        denom = (jnp.sum(p_loc, axis=-1, keepdims=True) + jnp.sum(p_ctx, axis=-1, keepdims=True)
                 + jnp.exp2(sink_col - m))
        o = (_dot(p_loc.astype(BF16), v_loc) + _dot(p_ctx.astype(BF16), vx_ref[:, hs])) / denom
        for j in range(Q_PER_KV):
            h = g * Q_PER_KV + j
            o_ref[:, h * HEAD_DIM:(h + 1) * HEAD_DIM] = o[j * BLOCK:(j + 1) * BLOCK].astype(o_ref.dtype)


def _attn_lat_call(cfg, p, sink, cos_a, sin_a):
    b, t, l = cfg.b, cfg.t, cfg.l
    nb = t // BLOCK
    kvh = cfg.kvh
    gw = Q_PER_KV * HEAD_DIM
    kvw = kvh * HEAD_DIM
    q_blk = cfg.off_qa // gw
    k_blk, v_blk = cfg.off_ka // kvw, cfg.off_va // kvw
    ctx_row0 = (b * t) // l

    def kv_spec(col_blk, shift):
        def imap(bi, n):
            return (bi * nb + jnp.clip(n + shift, 0, nb - 1), col_blk)
        return pl.BlockSpec((BLOCK, kvw), imap)

    in_specs = [pl.BlockSpec(memory_space=pltpu.SMEM)]
    in_specs += [pl.BlockSpec((BLOCK, gw), functools.partial(lambda bi, n, g: (bi * nb + n, q_blk + g), g=g))
                 for g in range(kvh)]
    in_specs += [kv_spec(k_blk, -1), kv_spec(k_blk, 0), kv_spec(k_blk, 1),
                 kv_spec(v_blk, -1), kv_spec(v_blk, 0), kv_spec(v_blk, 1),
                 pl.BlockSpec((l, kvw), lambda bi, n: (ctx_row0 + bi, k_blk)),
                 pl.BlockSpec((l, kvw), lambda bi, n: (ctx_row0 + bi, v_blk)),
                 pl.BlockSpec((t, HEAD_DIM), lambda bi, n: (0, 0)),
                 pl.BlockSpec((t, HEAD_DIM), lambda bi, n: (0, 0))]
    return pl.pallas_call(
        functools.partial(_attn_lat_kernel, kvh=kvh, t=t),
        grid=(b, nb),
        in_specs=in_specs,
        out_specs=pl.BlockSpec((BLOCK, cfg.a_q), lambda bi, n: (bi * nb + n, 0)),
        out_shape=jax.ShapeDtypeStruct((b * t, cfg.a_q), BF16),
        compiler_params=_cparams(2),
        name="attn_latent",
    )(sink, *([p] * (kvh + 8)), cos_a, sin_a)


def _attn_ctx_kernel(sink_ref, *refs, kvh):
    q_refs = refs[:kvh]
    kx_ref, vx_ref, o_ref = refs[kvh:]
    l = kx_ref.shape[0]
    scale = HEAD_DIM ** -0.5
    for g in range(kvh):
        hs = slice(g * HEAD_DIM, (g + 1) * HEAD_DIM)
        q = jnp.concatenate(
            [(q_refs[g][:, j * HEAD_DIM:(j + 1) * HEAD_DIM].astype(F32) * scale).astype(BF16)
             for j in range(Q_PER_KV)], axis=0)
        s = _dot_nt(q, kx_ref[:, hs])
        o = _softmax_pv(s, _sink_column(sink_ref, g, l), vx_ref[:, hs])
        for j in range(Q_PER_KV):
            h = g * Q_PER_KV + j
            o_ref[:, h * HEAD_DIM:(h + 1) * HEAD_DIM] = o[j * l:(j + 1) * l].astype(o_ref.dtype)


def _attn_ctx_call(cfg, p, sink):
    b, t, l = cfg.b, cfg.t, cfg.l
    kvh = cfg.kvh
    gw = Q_PER_KV * HEAD_DIM
    kvw = kvh * HEAD_DIM
    q_blk = cfg.off_qa // gw
    k_blk, v_blk = cfg.off_ka // kvw, cfg.off_va // kvw
    ctx_row0 = (b * t) // l
    in_specs = [pl.BlockSpec(memory_space=pltpu.SMEM)]
    in_specs += [pl.BlockSpec((l, gw), functools.partial(lambda bi, g: (ctx_row0 + bi, q_blk + g), g=g))
                 for g in range(kvh)]
    in_specs += [pl.BlockSpec((l, kvw), lambda bi: (ctx_row0 + bi, k_blk)),
                 pl.BlockSpec((l, kvw), lambda bi: (ctx_row0 + bi, v_blk))]
    return pl.pallas_call(
        functools.partial(_attn_ctx_kernel, kvh=kvh),
        grid=(b,),
        in_specs=in_specs,
        out_specs=pl.BlockSpec((l, cfg.a_q), lambda bi: (bi, 0)),
        out_shape=jax.ShapeDtypeStruct((b * l, cfg.a_q), BF16),
        compiler_params=_cparams(1),
        name="attn_ctx",
    )(sink, *([p] * (kvh + 2)))


def _retention_scan(q_of, k_of, v_ref, g_ref, gn, lg_f, lg_b, o_ref, acc_ref, n_chunks, s_f, s_b):
    c = BLOCK
    ii = lax.broadcasted_iota(jnp.int32, (c, c), 0).astype(F32)
    jj = lax.broadcasted_iota(jnp.int32, (c, c), 1).astype(F32)
    diff = ii - jj
    decay = jnp.where(diff > 0, jnp.exp(lg_f * jnp.maximum(diff, 0.0)),
                      jnp.where(diff < 0, jnp.exp(lg_b * jnp.maximum(-diff, 0.0)), 2.0))
    zeta_f = jnp.exp(lg_f * (c - 1 - ii))
    xi_f = jnp.exp(lg_f * (ii + 1))
    zeta_b = jnp.exp(lg_b * ii)
    xi_b = jnp.exp(lg_b * (c - ii))
    cd_f = jnp.exp(jnp.concatenate([lg_f, lg_f], axis=1) * c)
    cd_b = jnp.exp(jnp.concatenate([lg_b, lg_b], axis=1) * c)

    for i in reversed(range(n_chunks)):
        rows = slice(i * c, (i + 1) * c)
        q, k, v = q_of(i), k_of(i), v_ref[rows, :]
        acc_ref[rows, :] = _dot((q * xi_b).astype(BF16), s_b.astype(BF16))
        s_b = cd_b * s_b + _dot_tn((k * zeta_b).astype(BF16), v)

    for i in range(n_chunks):
        rows = slice(i * c, (i + 1) * c)
        q, k, v = q_of(i), k_of(i), v_ref[rows, :]
        scores = _dot_nt(q.astype(BF16), k.astype(BF16)) * decay
        o = (acc_ref[rows, :] + _dot(scores.astype(BF16), v)
             + _dot((q * xi_f).astype(BF16), s_f.astype(BF16)))
        s_f = cd_f * s_f + _dot_tn((k * zeta_f).astype(BF16), v)
        mu = jnp.mean(o, axis=-1, keepdims=True)
        dev = o - mu
        var = jnp.mean(dev * dev, axis=-1, keepdims=True)
        on = dev * lax.rsqrt(var + EPS) * gn
        gate = g_ref[rows, :].astype(F32)
        o_ref[rows, :] = (_silu(gate) * on).astype(o_ref.dtype)
    return s_f, s_b


def _log_decay(a_row):
    return jnp.log1p(-jnp.exp2(-a_row))


def _retention_kernel(qc_ref, kc_ref, vc_ref, gc_ref, q_ref, k_ref, v_ref, g_ref, gn_ref, af_ref, ab_ref,
                      cos_ref, sin_ref, oc_ref, o_ref, acc_ref, qr_ref, kr_ref):
    scale = HEAD_DIM ** -0.5
    lg_f, lg_b = _log_decay(af_ref[...]), _log_decay(ab_ref[...])
    gn = gn_ref[...]

    def qc_of(i):
        return qc_ref[i * BLOCK:(i + 1) * BLOCK, :].astype(F32)

    def kc_of(i):
        return kc_ref[i * BLOCK:(i + 1) * BLOCK, :].astype(F32) * scale

    zero = jnp.zeros((HEAD_DIM, R_DV), F32)
    s_f, s_b = _retention_scan(qc_of, kc_of, vc_ref, gc_ref, gn, lg_f, lg_b, oc_ref, acc_ref,
                               qc_ref.shape[0] // BLOCK, zero, zero)

    cos, sin = cos_ref[...], sin_ref[...]

    def rot(x):
        return x * cos + pltpu.roll(x, HEAD_DIM // 2, 1) * sin

    qr_ref[...] = rot(q_ref[...].astype(F32))
    kr_ref[...] = rot(k_ref[...].astype(F32)) * scale

    def q_of(i):
        return qr_ref[i * BLOCK:(i + 1) * BLOCK, :]

    def k_of(i):
        return kr_ref[i * BLOCK:(i + 1) * BLOCK, :]

    _retention_scan(q_of, k_of, v_ref, g_ref, gn, lg_f, lg_b, o_ref, acc_ref, q_ref.shape[0] // BLOCK, s_f, s_b)


def _retention_call(cfg, p, gn, a_f, a_b, cos_r, sin_r):
    b, t, l, rh = cfg.b, cfg.t, cfg.l, cfg.rh
    assert l <= t
    ctx_row0 = (b * t) // l
    qb, kb = cfg.off_qr // HEAD_DIM, cfg.off_kr // HEAD_DIM
    vb, gb = cfg.off_vr // R_DV, cfg.off_gr // R_DV

    def head_specs(rows, row_blk):
        return [pl.BlockSpec((rows, HEAD_DIM), lambda bi, h: (row_blk(bi), qb + h)),
                pl.BlockSpec((rows, HEAD_DIM), lambda bi, h: (row_blk(bi), kb + h)),
                pl.BlockSpec((rows, R_DV), lambda bi, h: (row_blk(bi), vb + h)),
                pl.BlockSpec((rows, R_DV), lambda bi, h: (row_blk(bi), gb + h))]

    table_spec = pl.BlockSpec((t, HEAD_DIM), lambda bi, h: (0, 0))
    decay_spec = pl.BlockSpec((None, 1, HEAD_DIM), lambda bi, h: (h, 0, 0))
    in_specs = (head_specs(l, lambda bi: ctx_row0 + bi) + head_specs(t, lambda bi: bi)
                + [pl.BlockSpec((1, R_DV), lambda bi, h: (0, h)), decay_spec, decay_spec, table_spec, table_spec])
    return pl.pallas_call(
        _retention_kernel,
        grid=(b, rh),
        in_specs=in_specs,
        out_specs=[pl.BlockSpec((l, R_DV), lambda bi, h: (bi, h)),
                   pl.BlockSpec((t, R_DV), lambda bi, h: (bi, h))],
        out_shape=[jax.ShapeDtypeStruct((b * l, cfg.r_v), BF16),
                   jax.ShapeDtypeStruct((b * t, cfg.r_v), BF16)],
        scratch_shapes=[pltpu.VMEM((t, R_DV), F32), pltpu.VMEM((t, HEAD_DIM), F32),
                        pltpu.VMEM((t, HEAD_DIM), F32)],
        compiler_params=_cparams(2),
        name="retention",
    )(*([p] * 8), gn, a_f, a_b, cos_r, sin_r)


def _rotary_tables(t):
    rows = t // GRID_W
    row = jnp.repeat(jnp.arange(rows, dtype=F32), GRID_W)
    col = jnp.tile(jnp.arange(GRID_W, dtype=F32), rows)
    nf = HEAD_DIM // 4
    inv = jnp.power(ROPE_BASE, -jnp.arange(nf, dtype=F32) / nf)
    ang_row, ang_col = row[:, None] * inv, col[:, None] * inv
    cos_a = jnp.concatenate([jnp.cos(ang_row)] * 2 + [jnp.cos(ang_col)] * 2, axis=1)
    sin_a = jnp.concatenate([-jnp.sin(ang_row), jnp.sin(ang_row), -jnp.sin(ang_col), jnp.sin(ang_col)], axis=1)
    nf = HEAD_DIM // 2
    inv = jnp.power(ROPE_BASE, -jnp.arange(nf, dtype=F32) / nf)
    ang = jnp.arange(t, dtype=F32)[:, None] * inv
    cos_r = jnp.concatenate([jnp.cos(ang)] * 2, axis=1)
    sin_r = jnp.concatenate([-jnp.sin(ang), jnp.sin(ang)], axis=1)
    return cos_a, sin_a, cos_r, sin_r


class _Config:
    def __init__(self, x, ctx, w_in, w_gate_d):
        self.b, self.t, self.d = x.shape
        self.l = ctx.shape[1]
        self.ntok = self.b * (self.t + self.l)
        self.ha = self.d // 256
        self.kvh = self.ha // Q_PER_KV
        self.rh = self.d // 256
        self.a_q = self.ha * HEAD_DIM
        self.a_kv = self.kvh * HEAD_DIM
        self.r_qk = self.rh * HEAD_DIM
        self.r_v = self.rh * R_DV
        self.off_ka = 0
        self.off_va = self.a_kv
        self.off_kr = 2 * self.a_kv
        self.off_vr = self.off_kr + self.r_qk
        self.off_qa = self.off_vr + self.r_v
        self.off_qr = self.off_qa + self.a_q
        self.off_gr = self.off_qr + self.r_qk
        self.off_za = self.off_gr + self.r_v
        self.off_zr = self.off_za + self.d
        assert w_in.shape[-1] == self.off_zr + self.d
        self.bm = min(ROW_TILE, self.t)
        self.norm_bm = min(NORM_ROW_TILE, self.t)
        assert self.t % self.bm == 0 and (self.b * self.l) % self.bm == 0
        assert self.t % GRID_W == 0 and self.t % BLOCK == 0 and self.l % BLOCK == 0

    def bidx_for(self, tile):
        assert self.t % tile == 0 and (self.b * self.l) % tile == 0
        tiles_per_batch = self.t // tile
        n_batch = self.b
        return lambda i: jnp.minimum(i // tiles_per_batch, n_batch)


def kernel(x, c, ctx, c_ctx, w_ada, b_ada, g_mix, g_ffn, w_in, attn_sink, ret_a_fwd, ret_a_bwd, ret_gn,
           w_br_attn, w_br_ret, w_out, w_gate_d, w_up_d, w_down_d, w_router, w_gate_e, w_up_e, w_down_e,
           g_final):
    cfg = _Config(x, ctx, w_in, w_gate_d)
    b, t, d, l = cfg.b, cfg.t, cfg.d, cfg.l
    depth = w_ada.shape[0]
    n_lat = b * t

    x_all = (x.reshape(n_lat, d), ctx.reshape(b * l, d))
    mod_rows = -(-(b + 1) // 8) * 8
    c_all = jnp.zeros((mod_rows, d), F32).at[:b].set(c).at[b].set(c_ctx)
    mod = _ada_call(c_all, w_ada, b_ada).reshape(depth, mod_rows, 1, 6 * d)

    cos_a, sin_a, cos_r, sin_r = _rotary_tables(t)
    w_br_attn_h, w_br_ret_h, w_out_h, w_down_d_h = (w.astype(BF16) for w in (w_br_attn, w_br_ret, w_out, w_down_d))
    w_gu_e_h = _interleave_gate_up(w_gate_e, w_up_e)
    w_router_h = jnp.pad(w_router, ((0, 0), (0, 0), (0, LANES - N_EXPERTS))).astype(BF16)
    a_f = jnp.broadcast_to(ret_a_fwd.astype(F32)[:, :, None, None], ret_a_fwd.shape + (1, HEAD_DIM))
    a_b = jnp.broadcast_to(ret_a_bwd.astype(F32)[:, :, None, None], ret_a_bwd.shape + (1, HEAD_DIM))

    for layer in range(depth):
        last = layer == depth - 1
        n_rows = n_lat if last else cfg.ntok
        mod_l = mod[layer]
        gn = ret_gn[layer].reshape(1, cfg.r_v)

        h = _norm_mod_call(cfg, x_all, g_mix[layer], mod_l, 0, cfg.ntok)
        p = _in_proj_call(cfg, h, w_in, layer, ctx_kv_only=last)
        y_a = _attn_lat_call(cfg, p, attn_sink[layer], cos_a, sin_a)
        y_r_ctx, y_r = _retention_call(cfg, p, gn, a_f[layer], a_b[layer], cos_r, sin_r)
        if last:
            y_a_ctx, y_r_ctx = y_a, y_r
        else:
            y_a_ctx = _attn_ctx_call(cfg, p, attn_sink[layer])
        m = _merge_call(cfg, y_a, y_r, y_a_ctx, y_r_ctx, w_br_attn_h, w_br_ret_h, layer, p, n_rows)
        x_all = _resid_call(cfg, m, w_out_h, layer, x_all, mod_l, 2, n_rows, bn=1024)

        i = layer // 2
        if layer % 2 == 0:
            hf = _norm_mod_call(cfg, x_all, g_ffn[layer], mod_l, 3, n_rows)
            act = _glu_call(cfg, hf, w_gate_d, w_up_d, i, n_rows)
            x_all = _resid_call(cfg, act, w_down_d_h, i, x_all, mod_l, 5, n_rows, bn=512)
        else:
            x_all = _moe_ffn(cfg, x_all, g_ffn[layer], mod_l, w_router_h, w_gu_e_h, w_down_e,
                             i, n_rows)

    return _final_norm_call(cfg, x_all, g_final).reshape(b, t, d)
```

```python
import functools

import numpy as np
import jax
import jax.numpy as jnp
from jax import lax
from jax.experimental import pallas as pl
from jax.experimental.pallas import tpu as pltpu

F32 = jnp.float32
BF16 = jnp.bfloat16

EPS = 1e-6
ROPE_BASE = 10000.0
GRID_W = 64
HEAD_DIM = 128
R_DV = 256
BLOCK = 128
Q_PER_KV = 4
N_EXPERTS = 8
LANES = 128

V7X_VMEM_BYTES = 64 * 1024 * 1024
VMEM_LIMIT = V7X_VMEM_BYTES - 8 * 1024 * 1024
ROW_TILE = 1024
NORM_ROW_TILE = 512
SORT_TILE = 512
DISPATCH_TILE = 512
COMBINE_TILE = 256
GLU_COL_BLOCKS = 2
U32 = jnp.uint32


def _cparams(n_axes):
    return pltpu.CompilerParams(dimension_semantics=("arbitrary",) * n_axes,
                                vmem_limit_bytes=VMEM_LIMIT)


def _dot(a, b):
    return jnp.dot(a, b, preferred_element_type=F32)


def _dot_nt(a, b):
    return lax.dot_general(a, b, (((1,), (1,)), ((), ())), preferred_element_type=F32)


def _dot_tn(a, b):
    return lax.dot_general(a, b, (((0,), (0,)), ((), ())), preferred_element_type=F32)


def _silu(x):
    return x * jax.nn.sigmoid(x)


def _ada_kernel(c_ref, w_ref, b_ref, o_ref):
    s = _silu(c_ref[...]).astype(BF16)
    o_ref[...] = _dot(s, w_ref[...].astype(BF16)) + b_ref[...]


def _ada_call(c_all, w_ada, b_ada):
    depth, d, n6 = w_ada.shape
    r = c_all.shape[0]
    bn = 1024
    return pl.pallas_call(
        _ada_kernel,
        grid=(depth, n6 // bn),
        in_specs=[pl.BlockSpec((r, d), lambda l, j: (0, 0)),
                  pl.BlockSpec((None, d, bn), lambda l, j: (l, 0, j)),
                  pl.BlockSpec((None, 1, bn), lambda l, j: (l, 0, j))],
        out_specs=pl.BlockSpec((None, r, bn), lambda l, j: (l, 0, j)),
        out_shape=jax.ShapeDtypeStruct((depth, r, n6), F32),
        compiler_params=_cparams(2),
        name="ada_mod",
    )(c_all, w_ada, b_ada.reshape(depth, 1, n6))


def _norm_mod_kernel(x_ref, g_ref, sh_ref, sc_ref, o_ref):
    x = x_ref[...]
    y = x * lax.rsqrt(jnp.mean(x * x, axis=-1, keepdims=True) + EPS) * g_ref[...]
    o_ref[...] = (y * (1.0 + sc_ref[...]) + sh_ref[...]).astype(o_ref.dtype)


def _norm_kernel(x_ref, g_ref, o_ref):
    x = x_ref[...]
    o_ref[...] = x * lax.rsqrt(jnp.mean(x * x, axis=-1, keepdims=True) + EPS) * g_ref[...]


def _two_stream(kernel_fn, lat_tiles):
    def wrapped(x_lat_ref, x_ctx_ref, *rest):
        @pl.when(pl.program_id(0) < lat_tiles)
        def _():
            kernel_fn(x_lat_ref, *rest)

        @pl.when(pl.program_id(0) >= lat_tiles)
        def _():
            kernel_fn(x_ctx_ref, *rest)
    return wrapped


def _norm_mod_call(cfg, x_src, g, mod_l, shift_chunk, n_rows):
    d, bm = cfg.d, cfg.norm_bm
    bidx = cfg.bidx_for(bm)
    if isinstance(x_src, tuple):
        lat_tiles = (cfg.b * cfg.t) // bm
        kern = _two_stream(_norm_mod_kernel, lat_tiles)
        x_specs = [pl.BlockSpec((bm, d), lambda i: (jnp.minimum(i, lat_tiles - 1), 0)),
                   pl.BlockSpec((bm, d), lambda i: (jnp.maximum(i - lat_tiles, 0), 0))]
        x_args = list(x_src)
    else:
        kern, x_specs, x_args = _norm_mod_kernel, [pl.BlockSpec((bm, d), lambda i: (i, 0))], [x_src]
    return pl.pallas_call(
        kern,
        grid=(n_rows // bm,),
        in_specs=x_specs + [pl.BlockSpec((1, d), lambda i: (0, 0)),
                            pl.BlockSpec((None, 1, d), lambda i: (bidx(i), 0, shift_chunk)),
                            pl.BlockSpec((None, 1, d), lambda i: (bidx(i), 0, shift_chunk + 1))],
        out_specs=pl.BlockSpec((bm, d), lambda i: (i, 0)),
        out_shape=jax.ShapeDtypeStruct((n_rows, d), BF16),
        compiler_params=_cparams(1),
        name="norm_mod",
    )(*x_args, g.reshape(1, d), mod_l, mod_l)


def _final_norm_call(cfg, x_all, g):
    d, bm = cfg.d, cfg.norm_bm
    n_rows = cfg.b * cfg.t
    return pl.pallas_call(
        _norm_kernel,
        grid=(n_rows // bm,),
        in_specs=[pl.BlockSpec((bm, d), lambda i: (i, 0)),
                  pl.BlockSpec((1, d), lambda i: (0, 0))],
        out_specs=pl.BlockSpec((bm, d), lambda i: (i, 0)),
        out_shape=jax.ShapeDtypeStruct((n_rows, d), F32),
        compiler_params=_cparams(1),
        name="final_norm",
    )(x_all, g.reshape(1, d))


def _mm_kernel(a_ref, w_ref, o_ref):
    o_ref[...] = _dot(a_ref[...], w_ref[...].astype(BF16)).astype(o_ref.dtype)


def _wide_tile(cfg, n_rows):
    return 2 * cfg.bm if n_rows % (2 * cfg.bm) == 0 else cfg.bm


def _mm_ctx_cols_kernel(a_ref, w_ref, o_ref, *, lat_tiles, ctx_blocks):
    unused = (pl.program_id(0) >= lat_tiles) & (pl.program_id(1) >= ctx_blocks)

    @pl.when(jnp.logical_not(unused))
    def _():
        _mm_kernel(a_ref, w_ref, o_ref)

    @pl.when(unused)
    def _():
        o_ref[...] = jnp.zeros_like(o_ref)


def _in_proj_call(cfg, h, w, layer, ctx_kv_only):
    d, bm = cfg.d, _wide_tile(cfg, cfg.ntok)
    n = w.shape[-1]
    bn = 512
    kern = _mm_kernel
    if ctx_kv_only and (cfg.b * cfg.t) % bm == 0 and cfg.off_qa % bn == 0:
        kern = functools.partial(_mm_ctx_cols_kernel, lat_tiles=(cfg.b * cfg.t) // bm,
                                 ctx_blocks=cfg.off_qa // bn)
    return pl.pallas_call(
        kern,
        grid=(cfg.ntok // bm, n // bn),
        in_specs=[pl.BlockSpec((bm, d), lambda i, j: (i, 0)),
                  pl.BlockSpec((None, d, bn), lambda i, j: (layer, 0, j))],
        out_specs=pl.BlockSpec((bm, bn), lambda i, j: (i, j)),
        out_shape=jax.ShapeDtypeStruct((cfg.ntok, n), BF16),
        compiler_params=_cparams(2),
        name="in_proj",
    )(h, w)


def _merge_kernel(ya_ref, yr_ref, yac_ref, yrc_ref, wa_ref, wr_ref, za_ref, zr_ref, o_ref, *, lat_tiles):
    def merge(ya, yr):
        a = _dot(ya[...], wa_ref[...])
        r = _dot(yr[...], wr_ref[...])
        za = za_ref[...].astype(F32)
        zr = zr_ref[...].astype(F32)
        o_ref[...] = (jax.nn.sigmoid(za) * a + jax.nn.sigmoid(zr) * r).astype(o_ref.dtype)

    i = pl.program_id(0)

    @pl.when(i < lat_tiles)
    def _():
        merge(ya_ref, yr_ref)

    @pl.when(i >= lat_tiles)
    def _():
        merge(yac_ref, yrc_ref)


def _merge_call(cfg, y_a, y_r, y_a_ctx, y_r_ctx, w_a, w_r, layer, p, n_rows):
    d, bm = cfg.d, cfg.bm
    bn = 512
    za_blk, zr_blk = cfg.off_za // bn, cfg.off_zr // bn
    lat_tiles = (cfg.b * cfg.t) // bm

    def lat(i, j):
        return (jnp.minimum(i, lat_tiles - 1), 0)

    def ctx(i, j):
        return (jnp.maximum(i - lat_tiles, 0), 0)

    return pl.pallas_call(
        functools.partial(_merge_kernel, lat_tiles=lat_tiles),
        grid=(n_rows // bm, d // bn),
        in_specs=[pl.BlockSpec((bm, cfg.a_q), lat),
                  pl.BlockSpec((bm, cfg.r_v), lat),
                  pl.BlockSpec((bm, cfg.a_q), ctx),
                  pl.BlockSpec((bm, cfg.r_v), ctx),
                  pl.BlockSpec((None, cfg.a_q, bn), lambda i, j: (layer, 0, j)),
                  pl.BlockSpec((None, cfg.r_v, bn), lambda i, j: (layer, 0, j)),
                  pl.BlockSpec((bm, bn), lambda i, j: (i, za_blk + j)),
                  pl.BlockSpec((bm, bn), lambda i, j: (i, zr_blk + j))],
        out_specs=pl.BlockSpec((bm, bn), lambda i, j: (i, j)),
        out_shape=jax.ShapeDtypeStruct((n_rows, d), BF16),
        compiler_params=_cparams(2),
        name="merge",
    )(y_a, y_r, y_a_ctx, y_r_ctx, w_a, w_r, p, p)


def _resid_kernel(a_ref, w_ref, x_ref, gt_ref, o_ref):
    o_ref[...] = x_ref[...] + gt_ref[...] * _dot(a_ref[...], w_ref[...])


def _resid_two_stream_kernel(a_ref, w_ref, x_lat_ref, x_ctx_ref, gt_ref, o_ref, *, lat_tiles):
    @pl.when(pl.program_id(0) < lat_tiles)
    def _():
        _resid_kernel(a_ref, w_ref, x_lat_ref, gt_ref, o_ref)

    @pl.when(pl.program_id(0) >= lat_tiles)
    def _():
        _resid_kernel(a_ref, w_ref, x_ctx_ref, gt_ref, o_ref)


def _resid_call(cfg, a, w, widx, x_src, mod_l, gate_chunk, n_rows, bn):
    d, bm = cfg.d, cfg.bm
    k = a.shape[1]
    bidx = cfg.bidx_for(bm)
    gblk = gate_chunk * (d // bn)
    if isinstance(x_src, tuple):
        lat_tiles = (cfg.b * cfg.t) // bm
        kern = functools.partial(_resid_two_stream_kernel, lat_tiles=lat_tiles)
        x_specs = [pl.BlockSpec((bm, bn), lambda i, j: (jnp.minimum(i, lat_tiles - 1),
                                                        jnp.where(i < lat_tiles, j, 0))),
                   pl.BlockSpec((bm, bn), lambda i, j: (jnp.maximum(i - lat_tiles, 0),
                                                        jnp.where(i >= lat_tiles, j, 0)))]
        x_args, aliases, out_rows = list(x_src), {}, n_rows
    else:
        kern, x_specs = _resid_kernel, [pl.BlockSpec((bm, bn), lambda i, j: (i, j))]
        x_args, aliases, out_rows = [x_src], {2: 0}, x_src.shape[0]
    return pl.pallas_call(
        kern,
        grid=(n_rows // bm, d // bn),
        in_specs=[pl.BlockSpec((bm, k), lambda i, j: (i, 0)),
                  pl.BlockSpec((None, k, bn), lambda i, j: (widx, 0, j))] + x_specs
                 + [pl.BlockSpec((None, 1, bn), lambda i, j: (bidx(i), 0, gblk + j))],
        out_specs=pl.BlockSpec((bm, bn), lambda i, j: (i, j)),
        out_shape=jax.ShapeDtypeStruct((out_rows, d), F32),
        input_output_aliases=aliases,
        compiler_params=_cparams(2),
        name="resid_proj",
    )(a, w, *x_args, mod_l)


def _glu_kernel(a_ref, wg_ref, wu_ref, o_ref):
    a = a_ref[...]
    g = _dot(a, wg_ref[...].astype(BF16))
    u = _dot(a, wu_ref[...].astype(BF16))
    o_ref[...] = (_silu(g) * u).astype(o_ref.dtype)


def _glu_call(cfg, h, wg, wu, widx, n_rows):
    d, bm = cfg.d, _wide_tile(cfg, n_rows)
    n = wg.shape[-1]
    bn = 512
    w_spec = pl.BlockSpec((None, d, bn), lambda i, j: (widx, 0, j))
    return pl.pallas_call(
        _glu_kernel,
        grid=(n_rows // bm, n // bn),
        in_specs=[pl.BlockSpec((bm, d), lambda i, j: (i, 0)), w_spec, w_spec],
        out_specs=pl.BlockSpec((bm, bn), lambda i, j: (i, j)),
        out_shape=jax.ShapeDtypeStruct((n_rows, n), BF16),
        compiler_params=_cparams(2),
        name="glu_up",
    )(h, wg, wu)


_R_I1, _R_I2, _R_W1, _R_W2, _R_R1, _R_R2 = range(6)


def _pack_halves(h):
    half = h.shape[1] // 2
    lo = lax.bitcast_convert_type(h[:, :half].astype(F32), U32)
    hi = lax.bitcast_convert_type(h[:, half:].astype(F32), U32)
    return (hi & jnp.uint32(0xFFFF0000)) | lax.shift_right_logical(lo, jnp.uint32(16))


def _unpack_halves(packed):
    lo = lax.bitcast_convert_type(lax.shift_left(packed, jnp.uint32(16)), F32)
    hi = lax.bitcast_convert_type(packed & jnp.uint32(0xFFFF0000), F32)
    return lo.astype(BF16), hi.astype(BF16)


def _route_kernel(x_ref, g_ref, sh_ref, sc_ref, wr_ref, hp_ref, route_ref, cnt_ref, carry_ref):
    @pl.when(pl.program_id(0) == 0)
    def _():
        carry_ref[...] = jnp.zeros_like(carry_ref)

    x = x_ref[...]
    y = x * lax.rsqrt(jnp.mean(x * x, axis=-1, keepdims=True) + EPS) * g_ref[...]
    h = (y * (1.0 + sc_ref[...]) + sh_ref[...]).astype(BF16)
    hp_ref[...] = _pack_halves(h)

    logits = _dot(h, wr_ref[...])
    rows = logits.shape[0]
    lane = lax.broadcasted_iota(jnp.int32, logits.shape, 1).astype(F32)
    neg = -jnp.inf
    lg = jnp.where(lane < N_EXPERTS, logits, neg)
    m1 = jnp.max(lg, axis=-1, keepdims=True)
    i1 = jnp.min(jnp.where(lg == m1, lane, float(LANES)), axis=-1, keepdims=True)
    lg2 = jnp.where(lane == i1, neg, lg)
    m2 = jnp.max(lg2, axis=-1, keepdims=True)
    i2 = jnp.min(jnp.where(lg2 == m2, lane, float(LANES)), axis=-1, keepdims=True)
    e = jnp.exp(m2 - m1)
    w1 = 1.0 / (1.0 + e)
    w2 = e / (1.0 + e)

    chosen = jnp.where((lane == i1) | (lane == i2), 1.0, 0.0)
    rr = lax.broadcasted_iota(jnp.int32, (rows, rows), 0)
    cc = lax.broadcasted_iota(jnp.int32, (rows, rows), 1)
    earlier = jnp.where(cc < rr, 1.0, 0.0).astype(BF16)
    rank = _dot(earlier, chosen.astype(BF16)) + carry_ref[...]
    r1 = jnp.sum(jnp.where(lane == i1, rank, 0.0), axis=-1, keepdims=True)
    r2 = jnp.sum(jnp.where(lane == i2, rank, 0.0), axis=-1, keepdims=True)
    carry_ref[...] += jnp.sum(chosen, axis=0, keepdims=True)
    cnt_ref[...] = carry_ref[...]

    rec = jnp.zeros_like(logits)
    for slot, val in ((_R_I1, i1), (_R_I2, i2), (_R_W1, w1), (_R_W2, w2), (_R_R1, r1), (_R_R2, r2)):
        rec = jnp.where(lane == slot, val, rec)
    route_ref[...] = rec


def _route_call(cfg, x_all, g, mod_l, w_router, i_moe, n_rows):
    d, bm = cfg.d, cfg.norm_bm
    bidx = cfg.bidx_for(bm)
    return pl.pallas_call(
        _route_kernel,
        grid=(n_rows // bm,),
        in_specs=[pl.BlockSpec((bm, d), lambda i: (i, 0)),
                  pl.BlockSpec((1, d), lambda i: (0, 0)),
                  pl.BlockSpec((None, 1, d), lambda i: (bidx(i), 0, 3)),
                  pl.BlockSpec((None, 1, d), lambda i: (bidx(i), 0, 4)),
                  pl.BlockSpec((None, d, LANES), lambda i: (i_moe, 0, 0))],
        out_specs=[pl.BlockSpec((bm, d // 2), lambda i: (i, 0)),
                   pl.BlockSpec((bm, LANES), lambda i: (i, 0)),
                   pl.BlockSpec((1, LANES), lambda i: (0, 0))],
        out_shape=[jax.ShapeDtypeStruct((n_rows, d // 2), U32),
                   jax.ShapeDtypeStruct((n_rows, LANES), F32),
                   jax.ShapeDtypeStruct((1, LANES), F32)],
        scratch_shapes=[pltpu.VMEM((1, LANES), F32)],
        compiler_params=_cparams(1),
        name="route",
    )(x_all, g.reshape(1, d), mod_l, mod_l, w_router)


def _slot(base_ref, e_ref, r_ref, j):
    return base_ref[e_ref[j]] + r_ref[j]


def _dispatch_kernel(base_ref, e1_ref, e2_ref, r1_ref, r2_ref, src_ref, _, dst_ref, sem):
    tile = e1_ref.shape[0]

    def issue(j, carry):
        src = src_ref.at[pl.ds(j, 1)]
        pltpu.make_async_copy(src, dst_ref.at[pl.ds(_slot(base_ref, e1_ref, r1_ref, j), 1)], sem).start()
        pltpu.make_async_copy(src, dst_ref.at[pl.ds(_slot(base_ref, e2_ref, r2_ref, j), 1)], sem).start()
        return carry

    lax.fori_loop(0, tile, issue, 0, unroll=8)

    for _ in range(2):
        pltpu.make_async_copy(src_ref, dst_ref.at[pl.ds(0, tile)], sem).wait()


def _dispatch_call(hp, base, e1, e2, r1, r2, n_slots):
    n_rows, half = hp.shape
    tile = DISPATCH_TILE
    smem_tile = pl.BlockSpec((tile,), lambda i: (i,), memory_space=pltpu.SMEM)
    return pl.pallas_call(
        _dispatch_kernel,
        grid=(n_rows // tile,),
        in_specs=[pl.BlockSpec(memory_space=pltpu.SMEM), smem_tile, smem_tile, smem_tile, smem_tile,
                  pl.BlockSpec((tile, half), lambda i: (i, 0)), pl.BlockSpec(memory_space=pl.ANY)],
        out_specs=pl.BlockSpec(memory_space=pl.ANY),
        out_shape=jax.ShapeDtypeStruct((n_slots, half), U32),
        scratch_shapes=[pltpu.SemaphoreType.DMA(())],
        input_output_aliases={6: 0},
        compiler_params=_cparams(1),
        name="moe_dispatch",
    )(base, e1, e2, r1, r2, hp, jnp.zeros((n_slots, half), U32))


def _glu_sorted_kernel(te_ref, used_ref, a_ref, wg_ref, wu_ref, o_ref, w_cat_ref):
    i = pl.program_id(1)
    bn = o_ref.shape[1]
    first_tile_of_expert = (i == 0) | (te_ref[i] != te_ref[jnp.maximum(i - 1, 0)])

    @pl.when(first_tile_of_expert)
    def _():
        w_cat_ref[:, :bn] = wg_ref[...]
        w_cat_ref[:, bn:] = wu_ref[...]

    @pl.when(i < used_ref[0])
    def _():
        lo, hi = _unpack_halves(a_ref[...])
        half = lo.shape[1]
        gu = _dot(lo, w_cat_ref[:half, :]) + _dot(hi, w_cat_ref[half:, :])
        o_ref[...] = (_silu(gu[:, :bn]) * gu[:, bn:]).astype(o_ref.dtype)

    @pl.when(i >= used_ref[0])
    def _():
        o_ref[...] = jnp.zeros_like(o_ref)


def _glu_sorted_call(a_sorted, wg, wu, i_moe, tile_expert, n_used):
    n_slots, half = a_sorted.shape
    d = 2 * half
    de = wg.shape[-1]
    bm = SORT_TILE
    bn = de // GLU_COL_BLOCKS
    w_spec = pl.BlockSpec((None, None, d, bn), lambda j, i, te, nu: (i_moe, te[i], 0, j))
    return pl.pallas_call(
        _glu_sorted_kernel,
        grid_spec=pltpu.PrefetchScalarGridSpec(
            num_scalar_prefetch=2,
            grid=(GLU_COL_BLOCKS, n_slots // bm),
            in_specs=[pl.BlockSpec((bm, half), lambda j, i, te, nu: (i, 0)), w_spec, w_spec],
            out_specs=pl.BlockSpec((bm, bn), lambda j, i, te, nu: (i, j)),
            scratch_shapes=[pltpu.VMEM((d, 2 * bn), BF16)]),
        out_shape=jax.ShapeDtypeStruct((n_slots, de), BF16),
        compiler_params=_cparams(2),
        name="moe_glu",
    )(tile_expert, n_used, a_sorted, wg, wu)


def _down_sorted_kernel(te_ref, used_ref, a_ref, w_ref, o_ref):
    @pl.when(pl.program_id(1) < used_ref[0])
    def _():
        o_ref[...] = _dot(a_ref[...], w_ref[...].astype(BF16))

    @pl.when(pl.program_id(1) >= used_ref[0])
    def _():
        o_ref[...] = jnp.zeros_like(o_ref)


def _down_sorted_call(act, wd, i_moe, tile_expert, n_used):
    n_slots, de = act.shape
    d = wd.shape[-1]
    bm = SORT_TILE
    bn = 1024
    return pl.pallas_call(
        _down_sorted_kernel,
        grid_spec=pltpu.PrefetchScalarGridSpec(
            num_scalar_prefetch=2,
            grid=(d // bn, n_slots // bm),
            in_specs=[pl.BlockSpec((bm, de), lambda j, i, te, nu: (i, 0)),
                      pl.BlockSpec((None, None, de, bn), lambda j, i, te, nu: (i_moe, te[i], 0, j))],
            out_specs=pl.BlockSpec((bm, bn), lambda j, i, te, nu: (i, j))),
        out_shape=jax.ShapeDtypeStruct((n_slots, d), F32),
        compiler_params=_cparams(2),
        name="moe_down",
    )(tile_expert, n_used, act, wd)


def _combine_kernel(base_ref, e1c, e2c, r1c, r2c, e1n, e2n, r1n, r2n, y_ref, route_ref, x_ref, gt_ref,
                    o_ref, buf_ref, sem):
    i = pl.program_id(0)
    n = pl.num_programs(0)
    bm = x_ref.shape[0]

    def gather(e1_ref, e2_ref, r1_ref, r2_ref, buf_slot):
        def issue(j, carry):
            pltpu.make_async_copy(y_ref.at[pl.ds(_slot(base_ref, e1_ref, r1_ref, j), 1)],
                                  buf_ref.at[buf_slot, pl.ds(j, 1)], sem.at[buf_slot]).start()
            pltpu.make_async_copy(y_ref.at[pl.ds(_slot(base_ref, e2_ref, r2_ref, j), 1)],
                                  buf_ref.at[buf_slot, pl.ds(bm + j, 1)], sem.at[buf_slot]).start()
            return carry
        lax.fori_loop(0, bm, issue, 0, unroll=8)

    @pl.when(i == 0)
    def _():
        gather(e1c, e2c, r1c, r2c, 0)

    @pl.when(i + 1 < n)
    def _():
        gather(e1n, e2n, r1n, r2n, (i + 1) % 2)

    cur = i % 2
    pltpu.make_async_copy(y_ref.at[pl.ds(0, 2 * bm)], buf_ref.at[cur], sem.at[cur]).wait()

    rec = route_ref[...]
    w1 = rec[:, _R_W1:_R_W1 + 1]
    w2 = rec[:, _R_W2:_R_W2 + 1]
    y = w1 * buf_ref[cur, :bm, :] + w2 * buf_ref[cur, bm:, :]
    o_ref[...] = x_ref[...] + gt_ref[...] * y


def _combine_call(cfg, y_sorted, route, base, e1, e2, r1, r2, x_all, mod_l, n_rows):
    d = cfg.d
    bm = COMBINE_TILE
    bidx = cfg.bidx_for(bm)
    n_tiles = n_rows // bm
    cur = pl.BlockSpec((bm,), lambda i: (i,), memory_space=pltpu.SMEM)
    nxt = pl.BlockSpec((bm,), lambda i: (jnp.minimum(i + 1, n_tiles - 1),), memory_space=pltpu.SMEM)
    return pl.pallas_call(
        _combine_kernel,
        grid=(n_tiles,),
        in_specs=[pl.BlockSpec(memory_space=pltpu.SMEM), cur, cur, cur, cur, nxt, nxt, nxt, nxt,
                  pl.BlockSpec(memory_space=pl.ANY),
                  pl.BlockSpec((bm, LANES), lambda i: (i, 0)),
                  pl.BlockSpec((bm, d), lambda i: (i, 0)),
                  pl.BlockSpec((None, 1, d), lambda i: (bidx(i), 0, 5))],
        out_specs=pl.BlockSpec((bm, d), lambda i: (i, 0)),
        out_shape=jax.ShapeDtypeStruct(x_all.shape, F32),
        scratch_shapes=[pltpu.VMEM((2, 2 * bm, d), F32), pltpu.SemaphoreType.DMA((2,))],
        input_output_aliases={11: 0},
        compiler_params=_cparams(1),
        name="moe_combine",
    )(base, e1, e2, r1, r2, e1, e2, r1, r2, y_sorted, route, x_all, mod_l)


def _moe_ffn(cfg, x_all, g, mod_l, w_router, wg, wu, wd, i_moe, n_rows):
    bm = SORT_TILE
    hp, route, counts = _route_call(cfg, x_all, g, mod_l, w_router, i_moe, n_rows)

    cnt = counts[0, :N_EXPERTS].astype(jnp.int32)
    padded = (cnt + bm - 1) // bm * bm
    ends = jnp.cumsum(padded)
    base = ends - padded
    n_tiles = 2 * n_rows // bm + N_EXPERTS
    tile_start = jnp.arange(n_tiles, dtype=jnp.int32) * bm
    tile_expert = jnp.minimum(jnp.sum((tile_start[:, None] >= ends[None, :]).astype(jnp.int32), axis=1),
                              N_EXPERTS - 1)
    n_used = (ends[-1:] // bm).astype(jnp.int32)
    e1, e2, r1, r2 = (route[:, k].astype(jnp.int32) for k in (_R_I1, _R_I2, _R_R1, _R_R2))

    a_sorted = _dispatch_call(hp, base, e1, e2, r1, r2, n_tiles * bm)
    act = _glu_sorted_call(a_sorted, wg, wu, i_moe, tile_expert, n_used)
    y_sorted = _down_sorted_call(act, wd, i_moe, tile_expert, n_used)
    return _combine_call(cfg, y_sorted, route, base, e1, e2, r1, r2, x_all, mod_l, n_rows)


def _softmax_pv(s, sink_col, v):
    m = jnp.maximum(jnp.max(s, axis=-1, keepdims=True), sink_col)
    p = jnp.exp(s - m)
    denom = jnp.sum(p, axis=-1, keepdims=True) + jnp.exp(sink_col - m)
    return _dot(p.astype(BF16), v) / denom


def _sink_column(sink_ref, group, rows):
    r = lax.broadcasted_iota(jnp.int32, (Q_PER_KV * rows, 1), 0)
    col = jnp.full((Q_PER_KV * rows, 1), sink_ref[group * Q_PER_KV], F32)
    for j in range(1, Q_PER_KV):
        col = jnp.where(r >= j * rows, sink_ref[group * Q_PER_KV + j], col)
    return col


def _attn_lat_kernel(sink_ref, *refs, kvh, t):
    q_refs = refs[:kvh]
    kp_ref, kc_ref, kn_ref, vp_ref, vc_ref, vn_ref, kx_ref, vx_ref, cos_ref, sin_ref, o_ref = refs[kvh:]
    n = pl.program_id(1)
    nb = t // BLOCK
    lane = lax.broadcasted_iota(jnp.int32, (BLOCK, HEAD_DIM), 1)
    first_half = (lane & (HEAD_DIM // 4)) == 0

    def rot(x, blk):
        start = pl.multiple_of(blk * BLOCK, BLOCK)
        c = cos_ref[pl.ds(start, BLOCK), :]
        s = sin_ref[pl.ds(start, BLOCK), :]
        partner = jnp.where(first_half, pltpu.roll(x, HEAD_DIM - HEAD_DIM // 4, 1),
                            pltpu.roll(x, HEAD_DIM // 4, 1))
        return x * c + partner * s

    blk_p = jnp.maximum(n - 1, 0)
    blk_n = jnp.minimum(n + 1, nb - 1)
    n_loc = 3 * BLOCK
    qi = lax.broadcasted_iota(jnp.int32, (BLOCK, n_loc), 0)
    kj = lax.broadcasted_iota(jnp.int32, (BLOCK, n_loc), 1)
    rel = kj - qi
    s_pos = n * BLOCK - BLOCK + kj
    valid = ((rel >= 0) & (rel <= 2 * BLOCK) & (s_pos >= 0) & (s_pos < t))[None]
    log2e = 1.0 / np.log(2.0)
    scale = HEAD_DIM ** -0.5 * log2e

    for g in range(kvh):
        hs = slice(g * HEAD_DIM, (g + 1) * HEAD_DIM)
        q = jnp.concatenate(
            [rot(q_refs[g][:, j * HEAD_DIM:(j + 1) * HEAD_DIM].astype(F32), n) * scale
             for j in range(Q_PER_KV)], axis=0).astype(BF16)
        k_loc = jnp.concatenate(
            [rot(kp_ref[:, hs].astype(F32), blk_p).astype(BF16),
             rot(kc_ref[:, hs].astype(F32), n).astype(BF16),
             rot(kn_ref[:, hs].astype(F32), blk_n).astype(BF16)], axis=0)
        v_loc = jnp.concatenate([vp_ref[:, hs], vc_ref[:, hs], vn_ref[:, hs]], axis=0)
        s_loc = _dot_nt(q, k_loc).reshape(Q_PER_KV, BLOCK, n_loc)
        s_loc = jnp.where(valid, s_loc, jnp.finfo(F32).min).reshape(Q_PER_KV * BLOCK, n_loc)
        s_ctx = _dot_nt(q, kx_ref[:, hs])
        sink_col = _sink_column(sink_ref, g, BLOCK) * log2e
        m = jnp.maximum(jnp.maximum(jnp.max(s_loc, axis=-1, keepdims=True),
                                    jnp.max(s_ctx, axis=-1, keepdims=True)), sink_col)
        p_loc = jnp.exp2(s_loc - m)
        p_ctx = jnp.exp2(s_ctx - m)
        denom = (jnp.sum(p_loc, axis=-1, keepdims=True) + jnp.sum(p_ctx, axis=-1, keepdims=True)
                 + jnp.exp2(sink_col - m))
        o = (_dot(p_loc.astype(BF16), v_loc) + _dot(p_ctx.astype(BF16), vx_ref[:, hs])) / denom
        for j in range(Q_PER_KV):
            h = g * Q_PER_KV + j
            o_ref[:, h * HEAD_DIM:(h + 1) * HEAD_DIM] = o[j * BLOCK:(j + 1) * BLOCK].astype(o_ref.dtype)


def _attn_lat_call(cfg, p, sink, cos_a, sin_a):
    b, t, l = cfg.b, cfg.t, cfg.l
    nb = t // BLOCK
    kvh = cfg.kvh
    gw = Q_PER_KV * HEAD_DIM
    kvw = kvh * HEAD_DIM
    q_blk = cfg.off_qa // gw
    k_blk, v_blk = cfg.off_ka // kvw, cfg.off_va // kvw
    ctx_row0 = (b * t) // l

    def kv_spec(col_blk, shift):
        def imap(bi, n):
            return (bi * nb + jnp.clip(n + shift, 0, nb - 1), col_blk)
        return pl.BlockSpec((BLOCK, kvw), imap)

    in_specs = [pl.BlockSpec(memory_space=pltpu.SMEM)]
    in_specs += [pl.BlockSpec((BLOCK, gw), functools.partial(lambda bi, n, g: (bi * nb + n, q_blk + g), g=g))
                 for g in range(kvh)]
    in_specs += [kv_spec(k_blk, -1), kv_spec(k_blk, 0), kv_spec(k_blk, 1),
                 kv_spec(v_blk, -1), kv_spec(v_blk, 0), kv_spec(v_blk, 1),
                 pl.BlockSpec((l, kvw), lambda bi, n: (ctx_row0 + bi, k_blk)),
                 pl.BlockSpec((l, kvw), lambda bi, n: (ctx_row0 + bi, v_blk)),
                 pl.BlockSpec((t, HEAD_DIM), lambda bi, n: (0, 0)),
                 pl.BlockSpec((t, HEAD_DIM), lambda bi, n: (0, 0))]
    return pl.pallas_call(
        functools.partial(_attn_lat_kernel, kvh=kvh, t=t),
        grid=(b, nb),
        in_specs=in_specs,
        out_specs=pl.BlockSpec((BLOCK, cfg.a_q), lambda bi, n: (bi * nb + n, 0)),
        out_shape=jax.ShapeDtypeStruct((b * t, cfg.a_q), BF16),
        compiler_params=_cparams(2),
        name="attn_latent",
    )(sink, *([p] * (kvh + 8)), cos_a, sin_a)


def _attn_ctx_kernel(sink_ref, *refs, kvh):
    q_refs = refs[:kvh]
    kx_ref, vx_ref, o_ref = refs[kvh:]
    l = kx_ref.shape[0]
    scale = HEAD_DIM ** -0.5
    for g in range(kvh):
        hs = slice(g * HEAD_DIM, (g + 1) * HEAD_DIM)
        q = jnp.concatenate(
            [(q_refs[g][:, j * HEAD_DIM:(j + 1) * HEAD_DIM].astype(F32) * scale).astype(BF16)
             for j in range(Q_PER_KV)], axis=0)
        s = _dot_nt(q, kx_ref[:, hs])
        o = _softmax_pv(s, _sink_column(sink_ref, g, l), vx_ref[:, hs])
        for j in range(Q_PER_KV):
            h = g * Q_PER_KV + j
            o_ref[:, h * HEAD_DIM:(h + 1) * HEAD_DIM] = o[j * l:(j + 1) * l].astype(o_ref.dtype)


def _attn_ctx_call(cfg, p, sink):
    b, t, l = cfg.b, cfg.t, cfg.l
    kvh = cfg.kvh
    gw = Q_PER_KV * HEAD_DIM
    kvw = kvh * HEAD_DIM
    q_blk = cfg.off_qa // gw
    k_blk, v_blk = cfg.off_ka // kvw, cfg.off_va // kvw
    ctx_row0 = (b * t) // l
    in_specs = [pl.BlockSpec(memory_space=pltpu.SMEM)]
    in_specs += [pl.BlockSpec((l, gw), functools.partial(lambda bi, g: (ctx_row0 + bi, q_blk + g), g=g))
                 for g in range(kvh)]
    in_specs += [pl.BlockSpec((l, kvw), lambda bi: (ctx_row0 + bi, k_blk)),
                 pl.BlockSpec((l, kvw), lambda bi: (ctx_row0 + bi, v_blk))]
    return pl.pallas_call(
        functools.partial(_attn_ctx_kernel, kvh=kvh),
        grid=(b,),
        in_specs=in_specs,
        out_specs=pl.BlockSpec((l, cfg.a_q), lambda bi: (bi, 0)),
        out_shape=jax.ShapeDtypeStruct((b * l, cfg.a_q), BF16),
        compiler_params=_cparams(1),
        name="attn_ctx",
    )(sink, *([p] * (kvh + 2)))


def _retention_scan(q_of, k_of, v_ref, g_ref, gn, lg_f, lg_b, o_ref, acc_ref, n_chunks, s_f, s_b):
    c = BLOCK
    ii = lax.broadcasted_iota(jnp.int32, (c, c), 0).astype(F32)
    jj = lax.broadcasted_iota(jnp.int32, (c, c), 1).astype(F32)
    diff = ii - jj
    decay = jnp.where(diff > 0, jnp.exp(lg_f * jnp.maximum(diff, 0.0)),
                      jnp.where(diff < 0, jnp.exp(lg_b * jnp.maximum(-diff, 0.0)), 2.0))
    zeta_f = jnp.exp(lg_f * (c - 1 - ii))
    xi_f = jnp.exp(lg_f * (ii + 1))
    zeta_b = jnp.exp(lg_b * ii)
    xi_b = jnp.exp(lg_b * (c - ii))
    cd_f = jnp.exp(jnp.concatenate([lg_f, lg_f], axis=1) * c)
    cd_b = jnp.exp(jnp.concatenate([lg_b, lg_b], axis=1) * c)

    for i in reversed(range(n_chunks)):
        rows = slice(i * c, (i + 1) * c)
        q, k, v = q_of(i), k_of(i), v_ref[rows, :]
        acc_ref[rows, :] = _dot((q * xi_b).astype(BF16), s_b.astype(BF16))
        s_b = cd_b * s_b + _dot_tn((k * zeta_b).astype(BF16), v)

    for i in range(n_chunks):
        rows = slice(i * c, (i + 1) * c)
        q, k, v = q_of(i), k_of(i), v_ref[rows, :]
        scores = _dot_nt(q.astype(BF16), k.astype(BF16)) * decay
        o = (acc_ref[rows, :] + _dot(scores.astype(BF16), v)
             + _dot((q * xi_f).astype(BF16), s_f.astype(BF16)))
        s_f = cd_f * s_f + _dot_tn((k * zeta_f).astype(BF16), v)
        mu = jnp.mean(o, axis=-1, keepdims=True)
        dev = o - mu
        var = jnp.mean(dev * dev, axis=-1, keepdims=True)
        on = dev * lax.rsqrt(var + EPS) * gn
        gate = g_ref[rows, :].astype(F32)
        o_ref[rows, :] = (_silu(gate) * on).astype(o_ref.dtype)
    return s_f, s_b


def _log_decay(a_row):
    return jnp.log1p(-jnp.exp2(-a_row))


def _retention_kernel(qc_ref, kc_ref, vc_ref, gc_ref, q_ref, k_ref, v_ref, g_ref, gn_ref, af_ref, ab_ref,
                      cos_ref, sin_ref, oc_ref, o_ref, acc_ref, qr_ref, kr_ref):
    scale = HEAD_DIM ** -0.5
    lg_f, lg_b = _log_decay(af_ref[...]), _log_decay(ab_ref[...])
    gn = gn_ref[...]

    def qc_of(i):
        return qc_ref[i * BLOCK:(i + 1) * BLOCK, :].astype(F32)

    def kc_of(i):
        return kc_ref[i * BLOCK:(i + 1) * BLOCK, :].astype(F32) * scale

    zero = jnp.zeros((HEAD_DIM, R_DV), F32)
    s_f, s_b = _retention_scan(qc_of, kc_of, vc_ref, gc_ref, gn, lg_f, lg_b, oc_ref, acc_ref,
                               qc_ref.shape[0] // BLOCK, zero, zero)

    cos, sin = cos_ref[...], sin_ref[...]

    def rot(x):
        return x * cos + pltpu.roll(x, HEAD_DIM // 2, 1) * sin

    qr_ref[...] = rot(q_ref[...].astype(F32))
    kr_ref[...] = rot(k_ref[...].astype(F32)) * scale

    def q_of(i):
        return qr_ref[i * BLOCK:(i + 1) * BLOCK, :]

    def k_of(i):
        return kr_ref[i * BLOCK:(i + 1) * BLOCK, :]

    _retention_scan(q_of, k_of, v_ref, g_ref, gn, lg_f, lg_b, o_ref, acc_ref, q_ref.shape[0] // BLOCK, s_f, s_b)


def _retention_call(cfg, p, gn, a_f, a_b, cos_r, sin_r):
    b, t, l, rh = cfg.b, cfg.t, cfg.l, cfg.rh
    assert l <= t
    ctx_row0 = (b * t) // l
    qb, kb = cfg.off_qr // HEAD_DIM, cfg.off_kr // HEAD_DIM
    vb, gb = cfg.off_vr // R_DV, cfg.off_gr // R_DV

    def head_specs(rows, row_blk):
        return [pl.BlockSpec((rows, HEAD_DIM), lambda bi, h: (row_blk(bi), qb + h)),
                pl.BlockSpec((rows, HEAD_DIM), lambda bi, h: (row_blk(bi), kb + h)),
                pl.BlockSpec((rows, R_DV), lambda bi, h: (row_blk(bi), vb + h)),
                pl.BlockSpec((rows, R_DV), lambda bi, h: (row_blk(bi), gb + h))]

    table_spec = pl.BlockSpec((t, HEAD_DIM), lambda bi, h: (0, 0))
    decay_spec = pl.BlockSpec((None, 1, HEAD_DIM), lambda bi, h: (h, 0, 0))
    in_specs = (head_specs(l, lambda bi: ctx_row0 + bi) + head_specs(t, lambda bi: bi)
                + [pl.BlockSpec((1, R_DV), lambda bi, h: (0, h)), decay_spec, decay_spec, table_spec, table_spec])
    return pl.pallas_call(
        _retention_kernel,
        grid=(b, rh),
        in_specs=in_specs,
        out_specs=[pl.BlockSpec((l, R_DV), lambda bi, h: (bi, h)),
                   pl.BlockSpec((t, R_DV), lambda bi, h: (bi, h))],
        out_shape=[jax.ShapeDtypeStruct((b * l, cfg.r_v), BF16),
                   jax.ShapeDtypeStruct((b * t, cfg.r_v), BF16)],
        scratch_shapes=[pltpu.VMEM((t, R_DV), F32), pltpu.VMEM((t, HEAD_DIM), F32),
                        pltpu.VMEM((t, HEAD_DIM), F32)],
        compiler_params=_cparams(2),
        name="retention",
    )(*([p] * 8), gn, a_f, a_b, cos_r, sin_r)


def _rotary_tables(t):
    rows = t // GRID_W
    row = jnp.repeat(jnp.arange(rows, dtype=F32), GRID_W)
    col = jnp.tile(jnp.arange(GRID_W, dtype=F32), rows)
    nf = HEAD_DIM // 4
    inv = jnp.power(ROPE_BASE, -jnp.arange(nf, dtype=F32) / nf)
    ang_row, ang_col = row[:, None] * inv, col[:, None] * inv
    cos_a = jnp.concatenate([jnp.cos(ang_row)] * 2 + [jnp.cos(ang_col)] * 2, axis=1)
    sin_a = jnp.concatenate([-jnp.sin(ang_row), jnp.sin(ang_row), -jnp.sin(ang_col), jnp.sin(ang_col)], axis=1)
    nf = HEAD_DIM // 2
    inv = jnp.power(ROPE_BASE, -jnp.arange(nf, dtype=F32) / nf)
    ang = jnp.arange(t, dtype=F32)[:, None] * inv
    cos_r = jnp.concatenate([jnp.cos(ang)] * 2, axis=1)
    sin_r = jnp.concatenate([-jnp.sin(ang), jnp.sin(ang)], axis=1)
    return cos_a, sin_a, cos_r, sin_r


class _Config:
    def __init__(self, x, ctx, w_in, w_gate_d):
        self.b, self.t, self.d = x.shape
        self.l = ctx.shape[1]
        self.ntok = self.b * (self.t + self.l)
        self.ha = self.d // 256
        self.kvh = self.ha // Q_PER_KV
        self.rh = self.d // 256
        self.a_q = self.ha * HEAD_DIM
        self.a_kv = self.kvh * HEAD_DIM
        self.r_qk = self.rh * HEAD_DIM
        self.r_v = self.rh * R_DV
        self.off_ka = 0
        self.off_va = self.a_kv
        self.off_kr = 2 * self.a_kv
        self.off_vr = self.off_kr + self.r_qk
        self.off_qa = self.off_vr + self.r_v
        self.off_qr = self.off_qa + self.a_q
        self.off_gr = self.off_qr + self.r_qk
        self.off_za = self.off_gr + self.r_v
        self.off_zr = self.off_za + self.d
        assert w_in.shape[-1] == self.off_zr + self.d
        self.bm = min(ROW_TILE, self.t)
        self.norm_bm = min(NORM_ROW_TILE, self.t)
        assert self.t % self.bm == 0 and (self.b * self.l) % self.bm == 0
        assert self.t % GRID_W == 0 and self.t % BLOCK == 0 and self.l % BLOCK == 0

    def bidx_for(self, tile):
        assert self.t % tile == 0 and (self.b * self.l) % tile == 0
        tiles_per_batch = self.t // tile
        n_batch = self.b
        return lambda i: jnp.minimum(i // tiles_per_batch, n_batch)


def kernel(x, c, ctx, c_ctx, w_ada, b_ada, g_mix, g_ffn, w_in, attn_sink, ret_a_fwd, ret_a_bwd, ret_gn,
           w_br_attn, w_br_ret, w_out, w_gate_d, w_up_d, w_down_d, w_router, w_gate_e, w_up_e, w_down_e,
           g_final):
    cfg = _Config(x, ctx, w_in, w_gate_d)
    b, t, d, l = cfg.b, cfg.t, cfg.d, cfg.l
    depth = w_ada.shape[0]
    n_lat = b * t

    x_all = (x.reshape(n_lat, d), ctx.reshape(b * l, d))
    mod_rows = -(-(b + 1) // 8) * 8
    c_all = jnp.zeros((mod_rows, d), F32).at[:b].set(c).at[b].set(c_ctx)
    mod = _ada_call(c_all, w_ada, b_ada).reshape(depth, mod_rows, 1, 6 * d)

    cos_a, sin_a, cos_r, sin_r = _rotary_tables(t)
    w_br_attn_h, w_br_ret_h, w_out_h, w_down_d_h = (w.astype(BF16) for w in (w_br_attn, w_br_ret, w_out, w_down_d))
    w_gate_e_h, w_up_e_h = w_gate_e.astype(BF16), w_up_e.astype(BF16)
    w_router_h = jnp.pad(w_router, ((0, 0), (0, 0), (0, LANES - N_EXPERTS))).astype(BF16)
    a_f = jnp.broadcast_to(ret_a_fwd.astype(F32)[:, :, None, None], ret_a_fwd.shape + (1, HEAD_DIM))
    a_b = jnp.broadcast_to(ret_a_bwd.astype(F32)[:, :, None, None], ret_a_bwd.shape + (1, HEAD_DIM))

    for layer in range(depth):
        last = layer == depth - 1
        n_rows = n_lat if last else cfg.ntok
        mod_l = mod[layer]
        gn = ret_gn[layer].reshape(1, cfg.r_v)

        h = _norm_mod_call(cfg, x_all, g_mix[layer], mod_l, 0, cfg.ntok)
        p = _in_proj_call(cfg, h, w_in, layer, ctx_kv_only=last)
        y_a = _attn_lat_call(cfg, p, attn_sink[layer], cos_a, sin_a)
        y_r_ctx, y_r = _retention_call(cfg, p, gn, a_f[layer], a_b[layer], cos_r, sin_r)
        if last:
            y_a_ctx, y_r_ctx = y_a, y_r
        else:
            y_a_ctx = _attn_ctx_call(cfg, p, attn_sink[layer])
        m = _merge_call(cfg, y_a, y_r, y_a_ctx, y_r_ctx, w_br_attn_h, w_br_ret_h, layer, p, n_rows)
        x_all = _resid_call(cfg, m, w_out_h, layer, x_all, mod_l, 2, n_rows, bn=1024)

        i = layer // 2
        if layer % 2 == 0:
            hf = _norm_mod_call(cfg, x_all, g_ffn[layer], mod_l, 3, n_rows)
            act = _glu_call(cfg, hf, w_gate_d, w_up_d, i, n_rows)
            x_all = _resid_call(cfg, act, w_down_d_h, i, x_all, mod_l, 5, n_rows, bn=512)
        else:
            x_all = _moe_ffn(cfg, x_all, g_ffn[layer], mod_l, w_router_h, w_gate_e_h, w_up_e_h, w_down_e,
                             i, n_rows)

    return _final_norm_call(cfg, x_all, g_final).reshape(b, t, d)
```

```python
import functools

import numpy as np
import jax
import jax.numpy as jnp
from jax import lax
from jax.experimental import pallas as pl
from jax.experimental.pallas import tpu as pltpu

F32 = jnp.float32
BF16 = jnp.bfloat16

EPS = 1e-6
ROPE_BASE = 10000.0
GRID_W = 64
HEAD_DIM = 128
R_DV = 256
BLOCK = 128
Q_PER_KV = 4
N_EXPERTS = 8
LANES = 128

V7X_VMEM_BYTES = 64 * 1024 * 1024
VMEM_LIMIT = V7X_VMEM_BYTES - 8 * 1024 * 1024
ROW_TILE = 1024
NORM_ROW_TILE = 512
SORT_TILE = 512
DISPATCH_TILE = 512
COMBINE_TILE = 256
GLU_COL_BLOCKS = 2
U32 = jnp.uint32


def _cparams(n_axes):
    return pltpu.CompilerParams(dimension_semantics=("arbitrary",) * n_axes,
                                vmem_limit_bytes=VMEM_LIMIT)


def _dot(a, b):
    return jnp.dot(a, b, preferred_element_type=F32)


def _dot_nt(a, b):
    return lax.dot_general(a, b, (((1,), (1,)), ((), ())), preferred_element_type=F32)


def _dot_tn(a, b):
    return lax.dot_general(a, b, (((0,), (0,)), ((), ())), preferred_element_type=F32)


def _silu(x):
    return x * jax.nn.sigmoid(x)


def _ada_kernel(c_ref, w_ref, b_ref, o_ref):
    s = _silu(c_ref[...]).astype(BF16)
    o_ref[...] = _dot(s, w_ref[...].astype(BF16)) + b_ref[...]


def _ada_call(c_all, w_ada, b_ada):
    depth, d, n6 = w_ada.shape
    r = c_all.shape[0]
    bn = 1024
    return pl.pallas_call(
        _ada_kernel,
        grid=(depth, n6 // bn),
        in_specs=[pl.BlockSpec((r, d), lambda l, j: (0, 0)),
                  pl.BlockSpec((None, d, bn), lambda l, j: (l, 0, j)),
                  pl.BlockSpec((None, 1, bn), lambda l, j: (l, 0, j))],
        out_specs=pl.BlockSpec((None, r, bn), lambda l, j: (l, 0, j)),
        out_shape=jax.ShapeDtypeStruct((depth, r, n6), F32),
        compiler_params=_cparams(2),
        name="ada_mod",
    )(c_all, w_ada, b_ada.reshape(depth, 1, n6))


def _norm_mod_kernel(x_ref, g_ref, sh_ref, sc_ref, o_ref):
    x = x_ref[...]
    gain = g_ref[...] * (1.0 + sc_ref[...])
    y = x * lax.rsqrt(jnp.mean(x * x, axis=-1, keepdims=True) + EPS)
    o_ref[...] = (y * gain + sh_ref[...]).astype(o_ref.dtype)


def _norm_kernel(x_ref, g_ref, o_ref):
    x = x_ref[...]
    o_ref[...] = x * lax.rsqrt(jnp.mean(x * x, axis=-1, keepdims=True) + EPS) * g_ref[...]


def _two_stream(kernel_fn, lat_tiles):
    def wrapped(x_lat_ref, x_ctx_ref, *rest):
        @pl.when(pl.program_id(0) < lat_tiles)
        def _():
            kernel_fn(x_lat_ref, *rest)

        @pl.when(pl.program_id(0) >= lat_tiles)
        def _():
            kernel_fn(x_ctx_ref, *rest)
    return wrapped


def _norm_mod_call(cfg, x_src, g, mod_l, shift_chunk, n_rows):
    d, bm = cfg.d, cfg.norm_bm
    bidx = cfg.bidx_for(bm)
    if isinstance(x_src, tuple):
        lat_tiles = (cfg.b * cfg.t) // bm
        kern = _two_stream(_norm_mod_kernel, lat_tiles)
        x_specs = [pl.BlockSpec((bm, d), lambda i: (jnp.minimum(i, lat_tiles - 1), 0)),
                   pl.BlockSpec((bm, d), lambda i: (jnp.maximum(i - lat_tiles, 0), 0))]
        x_args = list(x_src)
    else:
        kern, x_specs, x_args = _norm_mod_kernel, [pl.BlockSpec((bm, d), lambda i: (i, 0))], [x_src]
    return pl.pallas_call(
        kern,
        grid=(n_rows // bm,),
        in_specs=x_specs + [pl.BlockSpec((1, d), lambda i: (0, 0)),
                            pl.BlockSpec((None, 1, d), lambda i: (bidx(i), 0, shift_chunk)),
                            pl.BlockSpec((None, 1, d), lambda i: (bidx(i), 0, shift_chunk + 1))],
        out_specs=pl.BlockSpec((bm, d), lambda i: (i, 0)),
        out_shape=jax.ShapeDtypeStruct((n_rows, d), BF16),
        compiler_params=_cparams(1),
        name="norm_mod",
    )(*x_args, g.reshape(1, d), mod_l, mod_l)


def _final_norm_call(cfg, x_all, g):
    d, bm = cfg.d, cfg.norm_bm
    n_rows = cfg.b * cfg.t
    return pl.pallas_call(
        _norm_kernel,
        grid=(n_rows // bm,),
        in_specs=[pl.BlockSpec((bm, d), lambda i: (i, 0)),
                  pl.BlockSpec((1, d), lambda i: (0, 0))],
        out_specs=pl.BlockSpec((bm, d), lambda i: (i, 0)),
        out_shape=jax.ShapeDtypeStruct((n_rows, d), F32),
        compiler_params=_cparams(1),
        name="final_norm",
    )(x_all, g.reshape(1, d))


def _mm_kernel(a_ref, w_ref, o_ref):
    o_ref[...] = _dot(a_ref[...], w_ref[...].astype(BF16)).astype(o_ref.dtype)


def _wide_tile(cfg, n_rows):
    return 2 * cfg.bm if n_rows % (2 * cfg.bm) == 0 else cfg.bm


def _mm_ctx_cols_kernel(a_ref, w_ref, o_ref, *, lat_tiles, ctx_blocks):
    unused = (pl.program_id(0) >= lat_tiles) & (pl.program_id(1) >= ctx_blocks)

    @pl.when(jnp.logical_not(unused))
    def _():
        _mm_kernel(a_ref, w_ref, o_ref)

    @pl.when(unused)
    def _():
        o_ref[...] = jnp.zeros_like(o_ref)


def _in_proj_call(cfg, h, w, layer, ctx_kv_only):
    d, bm = cfg.d, _wide_tile(cfg, cfg.ntok)
    n = w.shape[-1]
    bn = 512
    kern = _mm_kernel
    if ctx_kv_only and (cfg.b * cfg.t) % bm == 0 and cfg.off_qa % bn == 0:
        kern = functools.partial(_mm_ctx_cols_kernel, lat_tiles=(cfg.b * cfg.t) // bm,
                                 ctx_blocks=cfg.off_qa // bn)
    return pl.pallas_call(
        kern,
        grid=(cfg.ntok // bm, n // bn),
        in_specs=[pl.BlockSpec((bm, d), lambda i, j: (i, 0)),
                  pl.BlockSpec((None, d, bn), lambda i, j: (layer, 0, j))],
        out_specs=pl.BlockSpec((bm, bn), lambda i, j: (i, j)),
        out_shape=jax.ShapeDtypeStruct((cfg.ntok, n), BF16),
        compiler_params=_cparams(2),
        name="in_proj",
    )(h, w)


def _merge_kernel(ya_ref, yr_ref, yac_ref, yrc_ref, wa_ref, wr_ref, za_ref, zr_ref, o_ref, *, lat_tiles):
    def merge(ya, yr):
        a = _dot(ya[...], wa_ref[...])
        r = _dot(yr[...], wr_ref[...])
        za = za_ref[...].astype(F32)
        zr = zr_ref[...].astype(F32)
        o_ref[...] = (jax.nn.sigmoid(za) * a + jax.nn.sigmoid(zr) * r).astype(o_ref.dtype)

    i = pl.program_id(0)

    @pl.when(i < lat_tiles)
    def _():
        merge(ya_ref, yr_ref)

    @pl.when(i >= lat_tiles)
    def _():
        merge(yac_ref, yrc_ref)


def _merge_call(cfg, y_a, y_r, y_a_ctx, y_r_ctx, w_a, w_r, layer, p, n_rows):
    d, bm = cfg.d, cfg.bm
    bn = 512
    za_blk, zr_blk = cfg.off_za // bn, cfg.off_zr // bn
    lat_tiles = (cfg.b * cfg.t) // bm

    def lat(i, j):
        return (jnp.minimum(i, lat_tiles - 1), 0)

    def ctx(i, j):
        return (jnp.maximum(i - lat_tiles, 0), 0)

    return pl.pallas_call(
        functools.partial(_merge_kernel, lat_tiles=lat_tiles),
        grid=(n_rows // bm, d // bn),
        in_specs=[pl.BlockSpec((bm, cfg.a_q), lat),
                  pl.BlockSpec((bm, cfg.r_v), lat),
                  pl.BlockSpec((bm, cfg.a_q), ctx),
                  pl.BlockSpec((bm, cfg.r_v), ctx),
                  pl.BlockSpec((None, cfg.a_q, bn), lambda i, j: (layer, 0, j)),
                  pl.BlockSpec((None, cfg.r_v, bn), lambda i, j: (layer, 0, j)),
                  pl.BlockSpec((bm, bn), lambda i, j: (i, za_blk + j)),
                  pl.BlockSpec((bm, bn), lambda i, j: (i, zr_blk + j))],
        out_specs=pl.BlockSpec((bm, bn), lambda i, j: (i, j)),
        out_shape=jax.ShapeDtypeStruct((n_rows, d), BF16),
        compiler_params=_cparams(2),
        name="merge",
    )(y_a, y_r, y_a_ctx, y_r_ctx, w_a, w_r, p, p)


def _resid_kernel(a_ref, w_ref, x_ref, gt_ref, o_ref):
    o_ref[...] = x_ref[...] + gt_ref[...] * _dot(a_ref[...], w_ref[...])


def _resid_two_stream_kernel(a_ref, w_ref, x_lat_ref, x_ctx_ref, gt_ref, o_ref, *, lat_tiles):
    @pl.when(pl.program_id(0) < lat_tiles)
    def _():
        _resid_kernel(a_ref, w_ref, x_lat_ref, gt_ref, o_ref)

    @pl.when(pl.program_id(0) >= lat_tiles)
    def _():
        _resid_kernel(a_ref, w_ref, x_ctx_ref, gt_ref, o_ref)


def _resid_call(cfg, a, w, widx, x_src, mod_l, gate_chunk, n_rows, bn):
    d, bm = cfg.d, cfg.bm
    k = a.shape[1]
    bidx = cfg.bidx_for(bm)
    gblk = gate_chunk * (d // bn)
    if isinstance(x_src, tuple):
        lat_tiles = (cfg.b * cfg.t) // bm
        kern = functools.partial(_resid_two_stream_kernel, lat_tiles=lat_tiles)
        x_specs = [pl.BlockSpec((bm, bn), lambda i, j: (jnp.minimum(i, lat_tiles - 1),
                                                        jnp.where(i < lat_tiles, j, 0))),
                   pl.BlockSpec((bm, bn), lambda i, j: (jnp.maximum(i - lat_tiles, 0),
                                                        jnp.where(i >= lat_tiles, j, 0)))]
        x_args, aliases, out_rows = list(x_src), {}, n_rows
    else:
        kern, x_specs = _resid_kernel, [pl.BlockSpec((bm, bn), lambda i, j: (i, j))]
        x_args, aliases, out_rows = [x_src], {2: 0}, x_src.shape[0]
    return pl.pallas_call(
        kern,
        grid=(n_rows // bm, d // bn),
        in_specs=[pl.BlockSpec((bm, k), lambda i, j: (i, 0)),
                  pl.BlockSpec((None, k, bn), lambda i, j: (widx, 0, j))] + x_specs
                 + [pl.BlockSpec((None, 1, bn), lambda i, j: (bidx(i), 0, gblk + j))],
        out_specs=pl.BlockSpec((bm, bn), lambda i, j: (i, j)),
        out_shape=jax.ShapeDtypeStruct((out_rows, d), F32),
        input_output_aliases=aliases,
        compiler_params=_cparams(2),
        name="resid_proj",
    )(a, w, *x_args, mod_l)


def _glu_kernel(a_ref, wg_ref, wu_ref, o_ref):
    a = a_ref[...]
    g = _dot(a, wg_ref[...].astype(BF16))
    u = _dot(a, wu_ref[...].astype(BF16))
    o_ref[...] = (_silu(g) * u).astype(o_ref.dtype)


def _glu_call(cfg, h, wg, wu, widx, n_rows):
    d, bm = cfg.d, _wide_tile(cfg, n_rows)
    n = wg.shape[-1]
    bn = 512
    w_spec = pl.BlockSpec((None, d, bn), lambda i, j: (widx, 0, j))
    return pl.pallas_call(
        _glu_kernel,
        grid=(n_rows // bm, n // bn),
        in_specs=[pl.BlockSpec((bm, d), lambda i, j: (i, 0)), w_spec, w_spec],
        out_specs=pl.BlockSpec((bm, bn), lambda i, j: (i, j)),
        out_shape=jax.ShapeDtypeStruct((n_rows, n), BF16),
        compiler_params=_cparams(2),
        name="glu_up",
    )(h, wg, wu)


_R_I1, _R_I2, _R_W1, _R_W2, _R_R1, _R_R2 = range(6)


def _pack_halves(h):
    half = h.shape[1] // 2
    lo = lax.bitcast_convert_type(h[:, :half].astype(F32), U32)
    hi = lax.bitcast_convert_type(h[:, half:].astype(F32), U32)
    return (hi & jnp.uint32(0xFFFF0000)) | lax.shift_right_logical(lo, jnp.uint32(16))


def _unpack_halves(packed):
    lo = lax.bitcast_convert_type(lax.shift_left(packed, jnp.uint32(16)), F32)
    hi = lax.bitcast_convert_type(packed & jnp.uint32(0xFFFF0000), F32)
    return lo.astype(BF16), hi.astype(BF16)


def _route_kernel(x_ref, g_ref, sh_ref, sc_ref, wr_ref, hp_ref, route_ref, cnt_ref, carry_ref):
    @pl.when(pl.program_id(0) == 0)
    def _():
        carry_ref[...] = jnp.zeros_like(carry_ref)

    x = x_ref[...]
    gain = g_ref[...] * (1.0 + sc_ref[...])
    y = x * lax.rsqrt(jnp.mean(x * x, axis=-1, keepdims=True) + EPS)
    h = (y * gain + sh_ref[...]).astype(BF16)
    hp_ref[...] = _pack_halves(h)

    logits = _dot(h, wr_ref[...])
    rows = logits.shape[0]
    lane = lax.broadcasted_iota(jnp.int32, logits.shape, 1).astype(F32)
    neg = -jnp.inf
    lg = jnp.where(lane < N_EXPERTS, logits, neg)
    m1 = jnp.max(lg, axis=-1, keepdims=True)
    i1 = jnp.min(jnp.where(lg == m1, lane, float(LANES)), axis=-1, keepdims=True)
    lg2 = jnp.where(lane == i1, neg, lg)
    m2 = jnp.max(lg2, axis=-1, keepdims=True)
    i2 = jnp.min(jnp.where(lg2 == m2, lane, float(LANES)), axis=-1, keepdims=True)
    e = jnp.exp(m2 - m1)
    w1 = 1.0 / (1.0 + e)
    w2 = e / (1.0 + e)

    chosen = jnp.where((lane == i1) | (lane == i2), 1.0, 0.0)
    rr = lax.broadcasted_iota(jnp.int32, (rows, rows), 0)
    cc = lax.broadcasted_iota(jnp.int32, (rows, rows), 1)
    earlier = jnp.where(cc < rr, 1.0, 0.0).astype(BF16)
    rank = _dot(earlier, chosen.astype(BF16)) + carry_ref[...]
    r1 = jnp.sum(jnp.where(lane == i1, rank, 0.0), axis=-1, keepdims=True)
    r2 = jnp.sum(jnp.where(lane == i2, rank, 0.0), axis=-1, keepdims=True)
    carry_ref[...] += jnp.sum(chosen, axis=0, keepdims=True)
    cnt_ref[...] = carry_ref[...]

    rec = jnp.zeros_like(logits)
    for slot, val in ((_R_I1, i1), (_R_I2, i2), (_R_W1, w1), (_R_W2, w2), (_R_R1, r1), (_R_R2, r2)):
        rec = jnp.where(lane == slot, val, rec)
    route_ref[...] = rec


def _route_call(cfg, x_all, g, mod_l, w_router, i_moe, n_rows):
    d, bm = cfg.d, cfg.norm_bm
    bidx = cfg.bidx_for(bm)
    return pl.pallas_call(
        _route_kernel,
        grid=(n_rows // bm,),
        in_specs=[pl.BlockSpec((bm, d), lambda i: (i, 0)),
                  pl.BlockSpec((1, d), lambda i: (0, 0)),
                  pl.BlockSpec((None, 1, d), lambda i: (bidx(i), 0, 3)),
                  pl.BlockSpec((None, 1, d), lambda i: (bidx(i), 0, 4)),
                  pl.BlockSpec((None, d, LANES), lambda i: (i_moe, 0, 0))],
        out_specs=[pl.BlockSpec((bm, d // 2), lambda i: (i, 0)),
                   pl.BlockSpec((bm, LANES), lambda i: (i, 0)),
                   pl.BlockSpec((1, LANES), lambda i: (0, 0))],
        out_shape=[jax.ShapeDtypeStruct((n_rows, d // 2), U32),
                   jax.ShapeDtypeStruct((n_rows, LANES), F32),
                   jax.ShapeDtypeStruct((1, LANES), F32)],
        scratch_shapes=[pltpu.VMEM((1, LANES), F32)],
        compiler_params=_cparams(1),
        name="route",
    )(x_all, g.reshape(1, d), mod_l, mod_l, w_router)


def _dispatch_kernel(s1_ref, s2_ref, src_ref, _, dst_ref, sem):
    tile = s1_ref.shape[0]

    def issue(j, carry):
        src = src_ref.at[pl.ds(j, 1)]
        pltpu.make_async_copy(src, dst_ref.at[pl.ds(s1_ref[j], 1)], sem).start()
        pltpu.make_async_copy(src, dst_ref.at[pl.ds(s2_ref[j], 1)], sem).start()
        return carry

    lax.fori_loop(0, tile, issue, 0, unroll=8)

    for _ in range(2):
        pltpu.make_async_copy(src_ref, dst_ref.at[pl.ds(0, tile)], sem).wait()


def _dispatch_call(hp, slot1, slot2, n_slots):
    n_rows, half = hp.shape
    tile = DISPATCH_TILE
    smem_tile = pl.BlockSpec((tile,), lambda i: (i,), memory_space=pltpu.SMEM)
    return pl.pallas_call(
        _dispatch_kernel,
        grid=(n_rows // tile,),
        in_specs=[smem_tile, smem_tile,
                  pl.BlockSpec((tile, half), lambda i: (i, 0)), pl.BlockSpec(memory_space=pl.ANY)],
        out_specs=pl.BlockSpec(memory_space=pl.ANY),
        out_shape=jax.ShapeDtypeStruct((n_slots, half), U32),
        scratch_shapes=[pltpu.SemaphoreType.DMA(())],
        input_output_aliases={3: 0},
        compiler_params=_cparams(1),
        name="moe_dispatch",
    )(slot1, slot2, hp, jnp.zeros((n_slots, half), U32))


def _glu_sorted_kernel(te_ref, used_ref, a_ref, wg_ref, wu_ref, o_ref, w_cat_ref):
    i = pl.program_id(1)
    bn = o_ref.shape[1]
    first_tile_of_expert = (i == 0) | (te_ref[i] != te_ref[jnp.maximum(i - 1, 0)])

    @pl.when(first_tile_of_expert)
    def _():
        w_cat_ref[:, :bn] = wg_ref[...]
        w_cat_ref[:, bn:] = wu_ref[...]

    @pl.when(i < used_ref[0])
    def _():
        lo, hi = _unpack_halves(a_ref[...])
        half = lo.shape[1]
        gu = _dot(lo, w_cat_ref[:half, :]) + _dot(hi, w_cat_ref[half:, :])
        o_ref[...] = (_silu(gu[:, :bn]) * gu[:, bn:]).astype(o_ref.dtype)

    @pl.when(i >= used_ref[0])
    def _():
        o_ref[...] = jnp.zeros_like(o_ref)


def _glu_sorted_call(a_sorted, wg, wu, i_moe, tile_expert, n_used):
    n_slots, half = a_sorted.shape
    d = 2 * half
    de = wg.shape[-1]
    bm = SORT_TILE
    bn = de // GLU_COL_BLOCKS
    w_spec = pl.BlockSpec((None, None, d, bn), lambda j, i, te, nu: (i_moe, te[i], 0, j))
    return pl.pallas_call(
        _glu_sorted_kernel,
        grid_spec=pltpu.PrefetchScalarGridSpec(
            num_scalar_prefetch=2,
            grid=(GLU_COL_BLOCKS, n_slots // bm),
            in_specs=[pl.BlockSpec((bm, half), lambda j, i, te, nu: (i, 0)), w_spec, w_spec],
            out_specs=pl.BlockSpec((bm, bn), lambda j, i, te, nu: (i, j)),
            scratch_shapes=[pltpu.VMEM((d, 2 * bn), BF16)]),
        out_shape=jax.ShapeDtypeStruct((n_slots, de), BF16),
        compiler_params=_cparams(2),
        name="moe_glu",
    )(tile_expert, n_used, a_sorted, wg, wu)


def _down_sorted_kernel(te_ref, used_ref, a_ref, w_ref, o_ref):
    @pl.when(pl.program_id(1) < used_ref[0])
    def _():
        o_ref[...] = _dot(a_ref[...], w_ref[...].astype(BF16))

    @pl.when(pl.program_id(1) >= used_ref[0])
    def _():
        o_ref[...] = jnp.zeros_like(o_ref)


def _down_sorted_call(act, wd, i_moe, tile_expert, n_used):
    n_slots, de = act.shape
    d = wd.shape[-1]
    bm = SORT_TILE
    bn = 1024
    return pl.pallas_call(
        _down_sorted_kernel,
        grid_spec=pltpu.PrefetchScalarGridSpec(
            num_scalar_prefetch=2,
            grid=(d // bn, n_slots // bm),
            in_specs=[pl.BlockSpec((bm, de), lambda j, i, te, nu: (i, 0)),
                      pl.BlockSpec((None, None, de, bn), lambda j, i, te, nu: (i_moe, te[i], 0, j))],
            out_specs=pl.BlockSpec((bm, bn), lambda j, i, te, nu: (i, j))),
        out_shape=jax.ShapeDtypeStruct((n_slots, d), F32),
        compiler_params=_cparams(2),
        name="moe_down",
    )(tile_expert, n_used, act, wd)


def _combine_kernel(s1_cur, s2_cur, s1_next, s2_next, y_ref, route_ref, x_ref, gt_ref, o_ref, buf_ref, sem):
    i = pl.program_id(0)
    n = pl.num_programs(0)
    bm = x_ref.shape[0]

    def gather(s1_ref, s2_ref, buf_slot):
        def issue(j, carry):
            pltpu.make_async_copy(y_ref.at[pl.ds(s1_ref[j], 1)],
                                  buf_ref.at[buf_slot, pl.ds(j, 1)], sem.at[buf_slot]).start()
            pltpu.make_async_copy(y_ref.at[pl.ds(s2_ref[j], 1)],
                                  buf_ref.at[buf_slot, pl.ds(bm + j, 1)], sem.at[buf_slot]).start()
            return carry
        lax.fori_loop(0, bm, issue, 0, unroll=8)

    @pl.when(i == 0)
    def _():
        gather(s1_cur, s2_cur, 0)

    @pl.when(i + 1 < n)
    def _():
        gather(s1_next, s2_next, (i + 1) % 2)

    cur = i % 2
    pltpu.make_async_copy(y_ref.at[pl.ds(0, 2 * bm)], buf_ref.at[cur], sem.at[cur]).wait()

    rec = route_ref[...]
    w1 = rec[:, _R_W1:_R_W1 + 1]
    w2 = rec[:, _R_W2:_R_W2 + 1]
    y = w1 * buf_ref[cur, :bm, :] + w2 * buf_ref[cur, bm:, :]
    o_ref[...] = x_ref[...] + gt_ref[...] * y


def _combine_call(cfg, y_sorted, route, slot1, slot2, x_all, mod_l, n_rows):
    d = cfg.d
    bm = COMBINE_TILE
    bidx = cfg.bidx_for(bm)
    n_tiles = n_rows // bm
    cur = pl.BlockSpec((bm,), lambda i: (i,), memory_space=pltpu.SMEM)
    nxt = pl.BlockSpec((bm,), lambda i: (jnp.minimum(i + 1, n_tiles - 1),), memory_space=pltpu.SMEM)
    return pl.pallas_call(
        _combine_kernel,
        grid=(n_tiles,),
        in_specs=[cur, cur, nxt, nxt,
                  pl.BlockSpec(memory_space=pl.ANY),
                  pl.BlockSpec((bm, LANES), lambda i: (i, 0)),
                  pl.BlockSpec((bm, d), lambda i: (i, 0)),
                  pl.BlockSpec((None, 1, d), lambda i: (bidx(i), 0, 5))],
        out_specs=pl.BlockSpec((bm, d), lambda i: (i, 0)),
        out_shape=jax.ShapeDtypeStruct(x_all.shape, F32),
        scratch_shapes=[pltpu.VMEM((2, 2 * bm, d), F32), pltpu.SemaphoreType.DMA((2,))],
        input_output_aliases={6: 0},
        compiler_params=_cparams(1),
        name="moe_combine",
    )(slot1, slot2, slot1, slot2, y_sorted, route, x_all, mod_l)


def _moe_ffn(cfg, x_all, g, mod_l, w_router, wg, wu, wd, i_moe, n_rows):
    bm = SORT_TILE
    hp, route, counts = _route_call(cfg, x_all, g, mod_l, w_router, i_moe, n_rows)

    cnt = counts[0, :N_EXPERTS].astype(jnp.int32)
    padded = (cnt + bm - 1) // bm * bm
    ends = jnp.cumsum(padded)
    base = ends - padded
    n_tiles = 2 * n_rows // bm + N_EXPERTS
    tile_start = jnp.arange(n_tiles, dtype=jnp.int32) * bm
    tile_expert = jnp.minimum(jnp.sum((tile_start[:, None] >= ends[None, :]).astype(jnp.int32), axis=1),
                              N_EXPERTS - 1)
    n_used = (ends[-1:] // bm).astype(jnp.int32)
    e1, e2, r1, r2 = (route[:, k].astype(jnp.int32) for k in (_R_I1, _R_I2, _R_R1, _R_R2))
    slot1, slot2 = base[e1] + r1, base[e2] + r2

    a_sorted = _dispatch_call(hp, slot1, slot2, n_tiles * bm)
    act = _glu_sorted_call(a_sorted, wg, wu, i_moe, tile_expert, n_used)
    y_sorted = _down_sorted_call(act, wd, i_moe, tile_expert, n_used)
    return _combine_call(cfg, y_sorted, route, slot1, slot2, x_all, mod_l, n_rows)


def _softmax_pv(s, sink_col, v):
    m = jnp.maximum(jnp.max(s, axis=-1, keepdims=True), sink_col)
    p = jnp.exp(s - m)
    denom = jnp.sum(p, axis=-1, keepdims=True) + jnp.exp(sink_col - m)
    return _dot(p.astype(BF16), v) / denom


def _sink_column(sink_ref, group, rows):
    r = lax.broadcasted_iota(jnp.int32, (Q_PER_KV * rows, 1), 0)
    col = jnp.full((Q_PER_KV * rows, 1), sink_ref[group * Q_PER_KV], F32)
    for j in range(1, Q_PER_KV):
        col = jnp.where(r >= j * rows, sink_ref[group * Q_PER_KV + j], col)
    return col


def _attn_lat_kernel(sink_ref, *refs, kvh, t):
    q_refs = refs[:kvh]
    kp_ref, kc_ref, kn_ref, vp_ref, vc_ref, vn_ref, kx_ref, vx_ref, cos_ref, sin_ref, o_ref = refs[kvh:]
    n = pl.program_id(1)
    nb = t // BLOCK
    lane = lax.broadcasted_iota(jnp.int32, (BLOCK, HEAD_DIM), 1)
    first_half = (lane & (HEAD_DIM // 4)) == 0

    def rot(x, blk):
        start = pl.multiple_of(blk * BLOCK, BLOCK)
        c = cos_ref[pl.ds(start, BLOCK), :]
        s = sin_ref[pl.ds(start, BLOCK), :]
        partner = jnp.where(first_half, pltpu.roll(x, HEAD_DIM - HEAD_DIM // 4, 1),
                            pltpu.roll(x, HEAD_DIM // 4, 1))
        return x * c + partner * s

    blk_p = jnp.maximum(n - 1, 0)
    blk_n = jnp.minimum(n + 1, nb - 1)
    n_loc = 3 * BLOCK
    qi = lax.broadcasted_iota(jnp.int32, (BLOCK, n_loc), 0)
    kj = lax.broadcasted_iota(jnp.int32, (BLOCK, n_loc), 1)
    rel = kj - qi
    s_pos = n * BLOCK - BLOCK + kj
    valid = ((rel >= 0) & (rel <= 2 * BLOCK) & (s_pos >= 0) & (s_pos < t))[None]
    log2e = 1.0 / np.log(2.0)
    scale = HEAD_DIM ** -0.5 * log2e

    for g in range(kvh):
        hs = slice(g * HEAD_DIM, (g + 1) * HEAD_DIM)
        q = jnp.concatenate(
            [rot(q_refs[g][:, j * HEAD_DIM:(j + 1) * HEAD_DIM].astype(F32), n) * scale
             for j in range(Q_PER_KV)], axis=0).astype(BF16)
        k_loc = jnp.concatenate(
            [rot(kp_ref[:, hs].astype(F32), blk_p).astype(BF16),
             rot(kc_ref[:, hs].astype(F32), n).astype(BF16),
             rot(kn_ref[:, hs].astype(F32), blk_n).astype(BF16)], axis=0)
        v_loc = jnp.concatenate([vp_ref[:, hs], vc_ref[:, hs], vn_ref[:, hs]], axis=0)
        s_loc = _dot_nt(q, k_loc).reshape(Q_PER_KV, BLOCK, n_loc)
        s_loc = jnp.where(valid, s_loc, jnp.finfo(F32).min).reshape(Q_PER_KV * BLOCK, n_loc)
        s_ctx = _dot_nt(q, kx_ref[:, hs])
        sink_col = _sink_column(sink_ref, g, BLOCK) * log2e
        m = jnp.maximum(jnp.maximum(jnp.max(s_loc, axis=-1, keepdims=True),
                                    jnp.max(s_ctx, axis=-1, keepdims=True)), sink_col)
        p_loc = jnp.exp2(s_loc - m)
        p_ctx = jnp.exp2(s_ctx - m)
        denom = (jnp.sum(p_loc, axis=-1, keepdims=True) + jnp.sum(p_ctx, axis=-1, keepdims=True)
                 + jnp.exp2(sink_col - m))
        o = (_dot(p_loc.astype(BF16), v_loc) + _dot(p_ctx.astype(BF16), vx_ref[:, hs])) / denom
        for j in range(Q_PER_KV):
            h = g * Q_PER_KV + j
            o_ref[:, h * HEAD_DIM:(h + 1) * HEAD_DIM] = o[j * BLOCK:(j + 1) * BLOCK].astype(o_ref.dtype)


def _attn_lat_call(cfg, p, sink, cos_a, sin_a):
    b, t, l = cfg.b, cfg.t, cfg.l
    nb = t // BLOCK
    kvh = cfg.kvh
    gw = Q_PER_KV * HEAD_DIM
    kvw = kvh * HEAD_DIM
    q_blk = cfg.off_qa // gw
    k_blk, v_blk = cfg.off_ka // kvw, cfg.off_va // kvw
    ctx_row0 = (b * t) // l

    def kv_spec(col_blk, shift):
        def imap(bi, n):
            return (bi * nb + jnp.clip(n + shift, 0, nb - 1), col_blk)
        return pl.BlockSpec((BLOCK, kvw), imap)

    in_specs = [pl.BlockSpec(memory_space=pltpu.SMEM)]
    in_specs += [pl.BlockSpec((BLOCK, gw), functools.partial(lambda bi, n, g: (bi * nb + n, q_blk + g), g=g))
                 for g in range(kvh)]
    in_specs += [kv_spec(k_blk, -1), kv_spec(k_blk, 0), kv_spec(k_blk, 1),
                 kv_spec(v_blk, -1), kv_spec(v_blk, 0), kv_spec(v_blk, 1),
                 pl.BlockSpec((l, kvw), lambda bi, n: (ctx_row0 + bi, k_blk)),
                 pl.BlockSpec((l, kvw), lambda bi, n: (ctx_row0 + bi, v_blk)),
                 pl.BlockSpec((t, HEAD_DIM), lambda bi, n: (0, 0)),
                 pl.BlockSpec((t, HEAD_DIM), lambda bi, n: (0, 0))]
    return pl.pallas_call(
        functools.partial(_attn_lat_kernel, kvh=kvh, t=t),
        grid=(b, nb),
        in_specs=in_specs,
        out_specs=pl.BlockSpec((BLOCK, cfg.a_q), lambda bi, n: (bi * nb + n, 0)),
        out_shape=jax.ShapeDtypeStruct((b * t, cfg.a_q), BF16),
        compiler_params=_cparams(2),
        name="attn_latent",
    )(sink, *([p] * (kvh + 8)), cos_a, sin_a)


def _attn_ctx_kernel(sink_ref, *refs, kvh):
    q_refs = refs[:kvh]
    kx_ref, vx_ref, o_ref = refs[kvh:]
    l = kx_ref.shape[0]
    scale = HEAD_DIM ** -0.5
    for g in range(kvh):
        hs = slice(g * HEAD_DIM, (g + 1) * HEAD_DIM)
        q = jnp.concatenate(
            [(q_refs[g][:, j * HEAD_DIM:(j + 1) * HEAD_DIM].astype(F32) * scale).astype(BF16)
             for j in range(Q_PER_KV)], axis=0)
        s = _dot_nt(q, kx_ref[:, hs])
        o = _softmax_pv(s, _sink_column(sink_ref, g, l), vx_ref[:, hs])
        for j in range(Q_PER_KV):
            h = g * Q_PER_KV + j
            o_ref[:, h * HEAD_DIM:(h + 1) * HEAD_DIM] = o[j * l:(j + 1) * l].astype(o_ref.dtype)


def _attn_ctx_call(cfg, p, sink):
    b, t, l = cfg.b, cfg.t, cfg.l
    kvh = cfg.kvh
    gw = Q_PER_KV * HEAD_DIM
    kvw = kvh * HEAD_DIM
    q_blk = cfg.off_qa // gw
    k_blk, v_blk = cfg.off_ka // kvw, cfg.off_va // kvw
    ctx_row0 = (b * t) // l
    in_specs = [pl.BlockSpec(memory_space=pltpu.SMEM)]
    in_specs += [pl.BlockSpec((l, gw), functools.partial(lambda bi, g: (ctx_row0 + bi, q_blk + g), g=g))
                 for g in range(kvh)]
    in_specs += [pl.BlockSpec((l, kvw), lambda bi: (ctx_row0 + bi, k_blk)),
                 pl.BlockSpec((l, kvw), lambda bi: (ctx_row0 + bi, v_blk))]
    return pl.pallas_call(
        functools.partial(_attn_ctx_kernel, kvh=kvh),
        grid=(b,),
        in_specs=in_specs,
        out_specs=pl.BlockSpec((l, cfg.a_q), lambda bi: (bi, 0)),
        out_shape=jax.ShapeDtypeStruct((b * l, cfg.a_q), BF16),
        compiler_params=_cparams(1),
        name="attn_ctx",
    )(sink, *([p] * (kvh + 2)))


def _retention_scan(q_of, k_of, v_ref, g_ref, gn, lg_f, lg_b, o_ref, acc_ref, n_chunks, s_f, s_b):
    c = BLOCK
    ii = lax.broadcasted_iota(jnp.int32, (c, c), 0).astype(F32)
    jj = lax.broadcasted_iota(jnp.int32, (c, c), 1).astype(F32)
    diff = ii - jj
    decay = jnp.where(diff > 0, jnp.exp(lg_f * jnp.maximum(diff, 0.0)),
                      jnp.where(diff < 0, jnp.exp(lg_b * jnp.maximum(-diff, 0.0)), 2.0))
    zeta_f = jnp.exp(lg_f * (c - 1 - ii))
    xi_f = jnp.exp(lg_f * (ii + 1))
    zeta_b = jnp.exp(lg_b * ii)
    xi_b = jnp.exp(lg_b * (c - ii))
    cd_f = jnp.exp(jnp.concatenate([lg_f, lg_f], axis=1) * c)
    cd_b = jnp.exp(jnp.concatenate([lg_b, lg_b], axis=1) * c)

    for i in reversed(range(n_chunks)):
        rows = slice(i * c, (i + 1) * c)
        q, k, v = q_of(i), k_of(i), v_ref[rows, :]
        acc_ref[rows, :] = _dot((q * xi_b).astype(BF16), s_b.astype(BF16))
        s_b = cd_b * s_b + _dot_tn((k * zeta_b).astype(BF16), v)

    for i in range(n_chunks):
        rows = slice(i * c, (i + 1) * c)
        q, k, v = q_of(i), k_of(i), v_ref[rows, :]
        scores = _dot_nt(q.astype(BF16), k.astype(BF16)) * decay
        o = (acc_ref[rows, :] + _dot(scores.astype(BF16), v)
             + _dot((q * xi_f).astype(BF16), s_f.astype(BF16)))
        s_f = cd_f * s_f + _dot_tn((k * zeta_f).astype(BF16), v)
        mu = jnp.mean(o, axis=-1, keepdims=True)
        dev = o - mu
        var = jnp.mean(dev * dev, axis=-1, keepdims=True)
        on = dev * lax.rsqrt(var + EPS) * gn
        gate = g_ref[rows, :].astype(F32)
        o_ref[rows, :] = (_silu(gate) * on).astype(o_ref.dtype)
    return s_f, s_b


def _log_decay(a_row):
    return jnp.log1p(-jnp.exp2(-a_row))


def _retention_kernel(qc_ref, kc_ref, vc_ref, gc_ref, q_ref, k_ref, v_ref, g_ref, gn_ref, af_ref, ab_ref,
                      cos_ref, sin_ref, oc_ref, o_ref, acc_ref, qr_ref, kr_ref):
    scale = HEAD_DIM ** -0.5
    lg_f, lg_b = _log_decay(af_ref[...]), _log_decay(ab_ref[...])
    gn = gn_ref[...]

    def qc_of(i):
        return qc_ref[i * BLOCK:(i + 1) * BLOCK, :].astype(F32)

    def kc_of(i):
        return kc_ref[i * BLOCK:(i + 1) * BLOCK, :].astype(F32) * scale

    zero = jnp.zeros((HEAD_DIM, R_DV), F32)
    s_f, s_b = _retention_scan(qc_of, kc_of, vc_ref, gc_ref, gn, lg_f, lg_b, oc_ref, acc_ref,
                               qc_ref.shape[0] // BLOCK, zero, zero)

    cos, sin = cos_ref[...], sin_ref[...]

    def rot(x):
        return x * cos + pltpu.roll(x, HEAD_DIM // 2, 1) * sin

    qr_ref[...] = rot(q_ref[...].astype(F32))
    kr_ref[...] = rot(k_ref[...].astype(F32)) * scale

    def q_of(i):
        return qr_ref[i * BLOCK:(i + 1) * BLOCK, :]

    def k_of(i):
        return kr_ref[i * BLOCK:(i + 1) * BLOCK, :]

    _retention_scan(q_of, k_of, v_ref, g_ref, gn, lg_f, lg_b, o_ref, acc_ref, q_ref.shape[0] // BLOCK, s_f, s_b)


def _retention_call(cfg, p, gn, a_f, a_b, cos_r, sin_r):
    b, t, l, rh = cfg.b, cfg.t, cfg.l, cfg.rh
    assert l <= t
    ctx_row0 = (b * t) // l
    qb, kb = cfg.off_qr // HEAD_DIM, cfg.off_kr // HEAD_DIM
    vb, gb = cfg.off_vr // R_DV, cfg.off_gr // R_DV

    def head_specs(rows, row_blk):
        return [pl.BlockSpec((rows, HEAD_DIM), lambda bi, h: (row_blk(bi), qb + h)),
                pl.BlockSpec((rows, HEAD_DIM), lambda bi, h: (row_blk(bi), kb + h)),
                pl.BlockSpec((rows, R_DV), lambda bi, h: (row_blk(bi), vb + h)),
                pl.BlockSpec((rows, R_DV), lambda bi, h: (row_blk(bi), gb + h))]

    table_spec = pl.BlockSpec((t, HEAD_DIM), lambda bi, h: (0, 0))
    decay_spec = pl.BlockSpec((None, 1, HEAD_DIM), lambda bi, h: (h, 0, 0))
    in_specs = (head_specs(l, lambda bi: ctx_row0 + bi) + head_specs(t, lambda bi: bi)
                + [pl.BlockSpec((1, R_DV), lambda bi, h: (0, h)), decay_spec, decay_spec, table_spec, table_spec])
    return pl.pallas_call(
        _retention_kernel,
        grid=(b, rh),
        in_specs=in_specs,
        out_specs=[pl.BlockSpec((l, R_DV), lambda bi, h: (bi, h)),
                   pl.BlockSpec((t, R_DV), lambda bi, h: (bi, h))],
        out_shape=[jax.ShapeDtypeStruct((b * l, cfg.r_v), BF16),
                   jax.ShapeDtypeStruct((b * t, cfg.r_v), BF16)],
        scratch_shapes=[pltpu.VMEM((t, R_DV), F32), pltpu.VMEM((t, HEAD_DIM), F32),
                        pltpu.VMEM((t, HEAD_DIM), F32)],
        compiler_params=_cparams(2),
        name="retention",
    )(*([p] * 8), gn, a_f, a_b, cos_r, sin_r)


def _rotary_tables(t):
    rows = t // GRID_W
    row = jnp.repeat(jnp.arange(rows, dtype=F32), GRID_W)
    col = jnp.tile(jnp.arange(GRID_W, dtype=F32), rows)
    nf = HEAD_DIM // 4
    inv = jnp.power(ROPE_BASE, -jnp.arange(nf, dtype=F32) / nf)
    ang_row, ang_col = row[:, None] * inv, col[:, None] * inv
    cos_a = jnp.concatenate([jnp.cos(ang_row)] * 2 + [jnp.cos(ang_col)] * 2, axis=1)
    sin_a = jnp.concatenate([-jnp.sin(ang_row), jnp.sin(ang_row), -jnp.sin(ang_col), jnp.sin(ang_col)], axis=1)
    nf = HEAD_DIM // 2
    inv = jnp.power(ROPE_BASE, -jnp.arange(nf, dtype=F32) / nf)
    ang = jnp.arange(t, dtype=F32)[:, None] * inv
    cos_r = jnp.concatenate([jnp.cos(ang)] * 2, axis=1)
    sin_r = jnp.concatenate([-jnp.sin(ang), jnp.sin(ang)], axis=1)
    return cos_a, sin_a, cos_r, sin_r


class _Config:
    def __init__(self, x, ctx, w_in, w_gate_d):
        self.b, self.t, self.d = x.shape
        self.l = ctx.shape[1]
        self.ntok = self.b * (self.t + self.l)
        self.ha = self.d // 256
        self.kvh = self.ha // Q_PER_KV
        self.rh = self.d // 256
        self.a_q = self.ha * HEAD_DIM
        self.a_kv = self.kvh * HEAD_DIM
        self.r_qk = self.rh * HEAD_DIM
        self.r_v = self.rh * R_DV
        self.off_ka = 0
        self.off_va = self.a_kv
        self.off_kr = 2 * self.a_kv
        self.off_vr = self.off_kr + self.r_qk
        self.off_qa = self.off_vr + self.r_v
        self.off_qr = self.off_qa + self.a_q
        self.off_gr = self.off_qr + self.r_qk
        self.off_za = self.off_gr + self.r_v
        self.off_zr = self.off_za + self.d
        assert w_in.shape[-1] == self.off_zr + self.d
        self.bm = min(ROW_TILE, self.t)
        self.norm_bm = min(NORM_ROW_TILE, self.t)
        assert self.t % self.bm == 0 and (self.b * self.l) % self.bm == 0
        assert self.t % GRID_W == 0 and self.t % BLOCK == 0 and self.l % BLOCK == 0

    def bidx_for(self, tile):
        assert self.t % tile == 0 and (self.b * self.l) % tile == 0
        tiles_per_batch = self.t // tile
        n_batch = self.b
        return lambda i: jnp.minimum(i // tiles_per_batch, n_batch)


def kernel(x, c, ctx, c_ctx, w_ada, b_ada, g_mix, g_ffn, w_in, attn_sink, ret_a_fwd, ret_a_bwd, ret_gn,
           w_br_attn, w_br_ret, w_out, w_gate_d, w_up_d, w_down_d, w_router, w_gate_e, w_up_e, w_down_e,
           g_final):
    cfg = _Config(x, ctx, w_in, w_gate_d)
    b, t, d, l = cfg.b, cfg.t, cfg.d, cfg.l
    depth = w_ada.shape[0]
    n_lat = b * t

    x_all = (x.reshape(n_lat, d), ctx.reshape(b * l, d))
    mod_rows = -(-(b + 1) // 8) * 8
    c_all = jnp.zeros((mod_rows, d), F32).at[:b].set(c).at[b].set(c_ctx)
    mod = _ada_call(c_all, w_ada, b_ada).reshape(depth, mod_rows, 1, 6 * d)

    cos_a, sin_a, cos_r, sin_r = _rotary_tables(t)
    w_br_attn_h, w_br_ret_h, w_out_h, w_down_d_h = (w.astype(BF16) for w in (w_br_attn, w_br_ret, w_out, w_down_d))
    w_gate_e_h, w_up_e_h = w_gate_e.astype(BF16), w_up_e.astype(BF16)
    w_router_h = jnp.pad(w_router, ((0, 0), (0, 0), (0, LANES - N_EXPERTS))).astype(BF16)
    a_f = jnp.broadcast_to(ret_a_fwd.astype(F32)[:, :, None, None], ret_a_fwd.shape + (1, HEAD_DIM))
    a_b = jnp.broadcast_to(ret_a_bwd.astype(F32)[:, :, None, None], ret_a_bwd.shape + (1, HEAD_DIM))

    for layer in range(depth):
        last = layer == depth - 1
        n_rows = n_lat if last else cfg.ntok
        mod_l = mod[layer]
        gn = ret_gn[layer].reshape(1, cfg.r_v)

        h = _norm_mod_call(cfg, x_all, g_mix[layer], mod_l, 0, cfg.ntok)
        p = _in_proj_call(cfg, h, w_in, layer, ctx_kv_only=last)
        y_a = _attn_lat_call(cfg, p, attn_sink[layer], cos_a, sin_a)
        y_r_ctx, y_r = _retention_call(cfg, p, gn, a_f[layer], a_b[layer], cos_r, sin_r)
        if last:
            y_a_ctx, y_r_ctx = y_a, y_r
        else:
            y_a_ctx = _attn_ctx_call(cfg, p, attn_sink[layer])
        m = _merge_call(cfg, y_a, y_r, y_a_ctx, y_r_ctx, w_br_attn_h, w_br_ret_h, layer, p, n_rows)
        x_all = _resid_call(cfg, m, w_out_h, layer, x_all, mod_l, 2, n_rows, bn=1024)

        i = layer // 2
        if layer % 2 == 0:
            hf = _norm_mod_call(cfg, x_all, g_ffn[layer], mod_l, 3, n_rows)
            act = _glu_call(cfg, hf, w_gate_d, w_up_d, i, n_rows)
            x_all = _resid_call(cfg, act, w_down_d_h, i, x_all, mod_l, 5, n_rows, bn=512)
        else:
            x_all = _moe_ffn(cfg, x_all, g_ffn[layer], mod_l, w_router_h, w_gate_e_h, w_up_e_h, w_down_e,
                             i, n_rows)

    return _final_norm_call(cfg, x_all, g_final).reshape(b, t, d)
```

```python
import functools

import numpy as np
import jax
import jax.numpy as jnp
from jax import lax
from jax.experimental import pallas as pl
from jax.experimental.pallas import tpu as pltpu

F32 = jnp.float32
BF16 = jnp.bfloat16

EPS = 1e-6
ROPE_BASE = 10000.0
GRID_W = 64
HEAD_DIM = 128
R_DV = 256
BLOCK = 128
Q_PER_KV = 4
N_EXPERTS = 8
LANES = 128

V7X_VMEM_BYTES = 64 * 1024 * 1024
VMEM_LIMIT = V7X_VMEM_BYTES - 8 * 1024 * 1024
ROW_TILE = 1024
NORM_ROW_TILE = 512
SORT_TILE = 512
DISPATCH_TILE = 512
COMBINE_TILE = 256
GLU_COL_BLOCKS = 2
ATTN_Q_BLOCKS = 8
U32 = jnp.uint32


def _cparams(n_axes):
    return pltpu.CompilerParams(dimension_semantics=("arbitrary",) * n_axes,
                                vmem_limit_bytes=VMEM_LIMIT)


def _dot(a, b):
    return jnp.dot(a, b, preferred_element_type=F32)


def _dot_nt(a, b):
    return lax.dot_general(a, b, (((1,), (1,)), ((), ())), preferred_element_type=F32)


def _dot_tn(a, b):
    return lax.dot_general(a, b, (((0,), (0,)), ((), ())), preferred_element_type=F32)


def _silu(x):
    return x * jax.nn.sigmoid(x)


def _ada_kernel(c_ref, w_ref, b_ref, o_ref):
    s = _silu(c_ref[...]).astype(BF16)
    o_ref[...] = _dot(s, w_ref[...].astype(BF16)) + b_ref[...]


def _ada_call(c_all, w_ada, b_ada):
    depth, d, n6 = w_ada.shape
    r = c_all.shape[0]
    bn = 1024
    return pl.pallas_call(
        _ada_kernel,
        grid=(depth, n6 // bn),
        in_specs=[pl.BlockSpec((r, d), lambda l, j: (0, 0)),
                  pl.BlockSpec((None, d, bn), lambda l, j: (l, 0, j)),
                  pl.BlockSpec((None, 1, bn), lambda l, j: (l, 0, j))],
        out_specs=pl.BlockSpec((None, r, bn), lambda l, j: (l, 0, j)),
        out_shape=jax.ShapeDtypeStruct((depth, r, n6), F32),
        compiler_params=_cparams(2),
        name="ada_mod",
    )(c_all, w_ada, b_ada.reshape(depth, 1, n6))


def _norm_mod_kernel(x_ref, g_ref, sh_ref, sc_ref, o_ref):
    x = x_ref[...]
    gain = g_ref[...] * (1.0 + sc_ref[...])
    y = x * lax.rsqrt(jnp.mean(x * x, axis=-1, keepdims=True) + EPS)
    o_ref[...] = (y * gain + sh_ref[...]).astype(o_ref.dtype)


def _norm_kernel(x_ref, g_ref, o_ref):
    x = x_ref[...]
    o_ref[...] = x * lax.rsqrt(jnp.mean(x * x, axis=-1, keepdims=True) + EPS) * g_ref[...]


def _two_stream(kernel_fn, lat_tiles):
    def wrapped(x_lat_ref, x_ctx_ref, *rest):
        @pl.when(pl.program_id(0) < lat_tiles)
        def _():
            kernel_fn(x_lat_ref, *rest)

        @pl.when(pl.program_id(0) >= lat_tiles)
        def _():
            kernel_fn(x_ctx_ref, *rest)
    return wrapped


def _norm_mod_call(cfg, x_src, g, mod_l, shift_chunk, n_rows):
    d, bm = cfg.d, cfg.norm_bm
    bidx = cfg.bidx_for(bm)
    if isinstance(x_src, tuple):
        lat_tiles = (cfg.b * cfg.t) // bm
        kern = _two_stream(_norm_mod_kernel, lat_tiles)
        x_specs = [pl.BlockSpec((bm, d), lambda i: (jnp.minimum(i, lat_tiles - 1), 0)),
                   pl.BlockSpec((bm, d), lambda i: (jnp.maximum(i - lat_tiles, 0), 0))]
        x_args = list(x_src)
    else:
        kern, x_specs, x_args = _norm_mod_kernel, [pl.BlockSpec((bm, d), lambda i: (i, 0))], [x_src]
    return pl.pallas_call(
        kern,
        grid=(n_rows // bm,),
        in_specs=x_specs + [pl.BlockSpec((1, d), lambda i: (0, 0)),
                            pl.BlockSpec((None, 1, d), lambda i: (bidx(i), 0, shift_chunk)),
                            pl.BlockSpec((None, 1, d), lambda i: (bidx(i), 0, shift_chunk + 1))],
        out_specs=pl.BlockSpec((bm, d), lambda i: (i, 0)),
        out_shape=jax.ShapeDtypeStruct((n_rows, d), BF16),
        compiler_params=_cparams(1),
        name="norm_mod",
    )(*x_args, g.reshape(1, d), mod_l, mod_l)


def _final_norm_call(cfg, x_all, g):
    d, bm = cfg.d, cfg.norm_bm
    n_rows = cfg.b * cfg.t
    return pl.pallas_call(
        _norm_kernel,
        grid=(n_rows // bm,),
        in_specs=[pl.BlockSpec((bm, d), lambda i: (i, 0)),
                  pl.BlockSpec((1, d), lambda i: (0, 0))],
        out_specs=pl.BlockSpec((bm, d), lambda i: (i, 0)),
        out_shape=jax.ShapeDtypeStruct((n_rows, d), F32),
        compiler_params=_cparams(1),
        name="final_norm",
    )(x_all, g.reshape(1, d))


def _mm_kernel(a_ref, w_ref, o_ref):
    o_ref[...] = _dot(a_ref[...], w_ref[...].astype(BF16)).astype(o_ref.dtype)


def _wide_tile(cfg, n_rows):
    return 2 * cfg.bm if n_rows % (2 * cfg.bm) == 0 else cfg.bm


def _mm_ctx_cols_kernel(a_ref, w_ref, o_ref, *, lat_tiles, ctx_blocks):
    unused = (pl.program_id(0) >= lat_tiles) & (pl.program_id(1) >= ctx_blocks)

    @pl.when(jnp.logical_not(unused))
    def _():
        _mm_kernel(a_ref, w_ref, o_ref)

    @pl.when(unused)
    def _():
        o_ref[...] = jnp.zeros_like(o_ref)


def _in_proj_call(cfg, h, w, layer, ctx_kv_only):
    d, bm = cfg.d, _wide_tile(cfg, cfg.ntok)
    n = w.shape[-1]
    bn = 512
    kern = _mm_kernel
    if ctx_kv_only and (cfg.b * cfg.t) % bm == 0 and cfg.off_qa % bn == 0:
        kern = functools.partial(_mm_ctx_cols_kernel, lat_tiles=(cfg.b * cfg.t) // bm,
                                 ctx_blocks=cfg.off_qa // bn)
    return pl.pallas_call(
        kern,
        grid=(cfg.ntok // bm, n // bn),
        in_specs=[pl.BlockSpec((bm, d), lambda i, j: (i, 0)),
                  pl.BlockSpec((None, d, bn), lambda i, j: (layer, 0, j))],
        out_specs=pl.BlockSpec((bm, bn), lambda i, j: (i, j)),
        out_shape=jax.ShapeDtypeStruct((cfg.ntok, n), BF16),
        compiler_params=_cparams(2),
        name="in_proj",
    )(h, w)


def _merge_kernel(ya_ref, yr_ref, yac_ref, yrc_ref, wa_ref, wr_ref, za_ref, zr_ref, o_ref, *, lat_tiles):
    def merge(ya, yr):
        a = _dot(ya[...], wa_ref[...])
        r = _dot(yr[...], wr_ref[...])
        za = za_ref[...].astype(F32)
        zr = zr_ref[...].astype(F32)
        o_ref[...] = (jax.nn.sigmoid(za) * a + jax.nn.sigmoid(zr) * r).astype(o_ref.dtype)

    i = pl.program_id(0)

    @pl.when(i < lat_tiles)
    def _():
        merge(ya_ref, yr_ref)

    @pl.when(i >= lat_tiles)
    def _():
        merge(yac_ref, yrc_ref)


def _merge_call(cfg, y_a, y_r, y_a_ctx, y_r_ctx, w_a, w_r, layer, p, n_rows):
    d, bm = cfg.d, cfg.bm
    bn = 512
    za_blk, zr_blk = cfg.off_za // bn, cfg.off_zr // bn
    lat_tiles = (cfg.b * cfg.t) // bm

    def lat(i, j):
        return (jnp.minimum(i, lat_tiles - 1), 0)

    def ctx(i, j):
        return (jnp.maximum(i - lat_tiles, 0), 0)

    return pl.pallas_call(
        functools.partial(_merge_kernel, lat_tiles=lat_tiles),
        grid=(n_rows // bm, d // bn),
        in_specs=[pl.BlockSpec((bm, cfg.a_q), lat),
                  pl.BlockSpec((bm, cfg.r_v), lat),
                  pl.BlockSpec((bm, cfg.a_q), ctx),
                  pl.BlockSpec((bm, cfg.r_v), ctx),
                  pl.BlockSpec((None, cfg.a_q, bn), lambda i, j: (layer, 0, j)),
                  pl.BlockSpec((None, cfg.r_v, bn), lambda i, j: (layer, 0, j)),
                  pl.BlockSpec((bm, bn), lambda i, j: (i, za_blk + j)),
                  pl.BlockSpec((bm, bn), lambda i, j: (i, zr_blk + j))],
        out_specs=pl.BlockSpec((bm, bn), lambda i, j: (i, j)),
        out_shape=jax.ShapeDtypeStruct((n_rows, d), BF16),
        compiler_params=_cparams(2),
        name="merge",
    )(y_a, y_r, y_a_ctx, y_r_ctx, w_a, w_r, p, p)


def _resid_kernel(a_ref, w_ref, x_ref, gt_ref, o_ref):
    o_ref[...] = x_ref[...] + gt_ref[...] * _dot(a_ref[...], w_ref[...])


def _resid_two_stream_kernel(a_ref, w_ref, x_lat_ref, x_ctx_ref, gt_ref, o_ref, *, lat_tiles):
    @pl.when(pl.program_id(0) < lat_tiles)
    def _():
        _resid_kernel(a_ref, w_ref, x_lat_ref, gt_ref, o_ref)

    @pl.when(pl.program_id(0) >= lat_tiles)
    def _():
        _resid_kernel(a_ref, w_ref, x_ctx_ref, gt_ref, o_ref)


def _resid_call(cfg, a, w, widx, x_src, mod_l, gate_chunk, n_rows, bn):
    d, bm = cfg.d, cfg.bm
    k = a.shape[1]
    bidx = cfg.bidx_for(bm)
    gblk = gate_chunk * (d // bn)
    if isinstance(x_src, tuple):
        lat_tiles = (cfg.b * cfg.t) // bm
        kern = functools.partial(_resid_two_stream_kernel, lat_tiles=lat_tiles)
        x_specs = [pl.BlockSpec((bm, bn), lambda i, j: (jnp.minimum(i, lat_tiles - 1),
                                                        jnp.where(i < lat_tiles, j, 0))),
                   pl.BlockSpec((bm, bn), lambda i, j: (jnp.maximum(i - lat_tiles, 0),
                                                        jnp.where(i >= lat_tiles, j, 0)))]
        x_args, aliases, out_rows = list(x_src), {}, n_rows
    else:
        kern, x_specs = _resid_kernel, [pl.BlockSpec((bm, bn), lambda i, j: (i, j))]
        x_args, aliases, out_rows = [x_src], {2: 0}, x_src.shape[0]
    return pl.pallas_call(
        kern,
        grid=(n_rows // bm, d // bn),
        in_specs=[pl.BlockSpec((bm, k), lambda i, j: (i, 0)),
                  pl.BlockSpec((None, k, bn), lambda i, j: (widx, 0, j))] + x_specs
                 + [pl.BlockSpec((None, 1, bn), lambda i, j: (bidx(i), 0, gblk + j))],
        out_specs=pl.BlockSpec((bm, bn), lambda i, j: (i, j)),
        out_shape=jax.ShapeDtypeStruct((out_rows, d), F32),
        input_output_aliases=aliases,
        compiler_params=_cparams(2),
        name="resid_proj",
    )(a, w, *x_args, mod_l)


def _glu_kernel(a_ref, wg_ref, wu_ref, o_ref):
    a = a_ref[...]
    g = _dot(a, wg_ref[...].astype(BF16))
    u = _dot(a, wu_ref[...].astype(BF16))
    o_ref[...] = (_silu(g) * u).astype(o_ref.dtype)


def _glu_call(cfg, h, wg, wu, widx, n_rows):
    d, bm = cfg.d, _wide_tile(cfg, n_rows)
    n = wg.shape[-1]
    bn = 512
    w_spec = pl.BlockSpec((None, d, bn), lambda i, j: (widx, 0, j))
    return pl.pallas_call(
        _glu_kernel,
        grid=(n_rows // bm, n // bn),
        in_specs=[pl.BlockSpec((bm, d), lambda i, j: (i, 0)), w_spec, w_spec],
        out_specs=pl.BlockSpec((bm, bn), lambda i, j: (i, j)),
        out_shape=jax.ShapeDtypeStruct((n_rows, n), BF16),
        compiler_params=_cparams(2),
        name="glu_up",
    )(h, wg, wu)


_R_I1, _R_I2, _R_W1, _R_W2, _R_R1, _R_R2 = range(6)


def _pack_halves(h):
    half = h.shape[1] // 2
    lo = lax.bitcast_convert_type(h[:, :half].astype(F32), U32)
    hi = lax.bitcast_convert_type(h[:, half:].astype(F32), U32)
    return (hi & jnp.uint32(0xFFFF0000)) | lax.shift_right_logical(lo, jnp.uint32(16))


def _unpack_halves(packed):
    lo = lax.bitcast_convert_type(lax.shift_left(packed, jnp.uint32(16)), F32)
    hi = lax.bitcast_convert_type(packed & jnp.uint32(0xFFFF0000), F32)
    return lo.astype(BF16), hi.astype(BF16)


def _route_kernel(x_ref, g_ref, sh_ref, sc_ref, wr_ref, hp_ref, route_ref, cnt_ref, carry_ref):
    @pl.when(pl.program_id(0) == 0)
    def _():
        carry_ref[...] = jnp.zeros_like(carry_ref)

    x = x_ref[...]
    gain = g_ref[...] * (1.0 + sc_ref[...])
    y = x * lax.rsqrt(jnp.mean(x * x, axis=-1, keepdims=True) + EPS)
    h = (y * gain + sh_ref[...]).astype(BF16)
    hp_ref[...] = _pack_halves(h)

    logits = _dot(h, wr_ref[...])
    rows = logits.shape[0]
    lane = lax.broadcasted_iota(jnp.int32, logits.shape, 1).astype(F32)
    neg = -jnp.inf
    lg = jnp.where(lane < N_EXPERTS, logits, neg)
    m1 = jnp.max(lg, axis=-1, keepdims=True)
    i1 = jnp.min(jnp.where(lg == m1, lane, float(LANES)), axis=-1, keepdims=True)
    lg2 = jnp.where(lane == i1, neg, lg)
    m2 = jnp.max(lg2, axis=-1, keepdims=True)
    i2 = jnp.min(jnp.where(lg2 == m2, lane, float(LANES)), axis=-1, keepdims=True)
    e = jnp.exp(m2 - m1)
    w1 = 1.0 / (1.0 + e)
    w2 = e / (1.0 + e)

    chosen = jnp.where((lane == i1) | (lane == i2), 1.0, 0.0)
    rr = lax.broadcasted_iota(jnp.int32, (rows, rows), 0)
    cc = lax.broadcasted_iota(jnp.int32, (rows, rows), 1)
    earlier = jnp.where(cc < rr, 1.0, 0.0).astype(BF16)
    rank = _dot(earlier, chosen.astype(BF16)) + carry_ref[...]
    r1 = jnp.sum(jnp.where(lane == i1, rank, 0.0), axis=-1, keepdims=True)
    r2 = jnp.sum(jnp.where(lane == i2, rank, 0.0), axis=-1, keepdims=True)
    carry_ref[...] += jnp.sum(chosen, axis=0, keepdims=True)
    cnt_ref[...] = carry_ref[...]

    rec = jnp.zeros_like(logits)
    for slot, val in ((_R_I1, i1), (_R_I2, i2), (_R_W1, w1), (_R_W2, w2), (_R_R1, r1), (_R_R2, r2)):
        rec = jnp.where(lane == slot, val, rec)
    route_ref[...] = rec


def _route_call(cfg, x_all, g, mod_l, w_router, i_moe, n_rows):
    d, bm = cfg.d, cfg.norm_bm
    bidx = cfg.bidx_for(bm)
    return pl.pallas_call(
        _route_kernel,
        grid=(n_rows // bm,),
        in_specs=[pl.BlockSpec((bm, d), lambda i: (i, 0)),
                  pl.BlockSpec((1, d), lambda i: (0, 0)),
                  pl.BlockSpec((None, 1, d), lambda i: (bidx(i), 0, 3)),
                  pl.BlockSpec((None, 1, d), lambda i: (bidx(i), 0, 4)),
                  pl.BlockSpec((None, d, LANES), lambda i: (i_moe, 0, 0))],
        out_specs=[pl.BlockSpec((bm, d // 2), lambda i: (i, 0)),
                   pl.BlockSpec((bm, LANES), lambda i: (i, 0)),
                   pl.BlockSpec((1, LANES), lambda i: (0, 0))],
        out_shape=[jax.ShapeDtypeStruct((n_rows, d // 2), U32),
                   jax.ShapeDtypeStruct((n_rows, LANES), F32),
                   jax.ShapeDtypeStruct((1, LANES), F32)],
        scratch_shapes=[pltpu.VMEM((1, LANES), F32)],
        compiler_params=_cparams(1),
        name="route",
    )(x_all, g.reshape(1, d), mod_l, mod_l, w_router)


def _dispatch_kernel(s1_ref, s2_ref, src_ref, _, dst_ref, sem):
    tile = s1_ref.shape[0]

    def issue(j, carry):
        src = src_ref.at[pl.ds(j, 1)]
        pltpu.make_async_copy(src, dst_ref.at[pl.ds(s1_ref[j], 1)], sem).start()
        pltpu.make_async_copy(src, dst_ref.at[pl.ds(s2_ref[j], 1)], sem).start()
        return carry

    lax.fori_loop(0, tile, issue, 0, unroll=8)

    for _ in range(2):
        pltpu.make_async_copy(src_ref, dst_ref.at[pl.ds(0, tile)], sem).wait()


def _dispatch_call(hp, slot1, slot2, n_slots):
    n_rows, half = hp.shape
    tile = DISPATCH_TILE
    smem_tile = pl.BlockSpec((tile,), lambda i: (i,), memory_space=pltpu.SMEM)
    return pl.pallas_call(
        _dispatch_kernel,
        grid=(n_rows // tile,),
        in_specs=[smem_tile, smem_tile,
                  pl.BlockSpec((tile, half), lambda i: (i, 0)), pl.BlockSpec(memory_space=pl.ANY)],
        out_specs=pl.BlockSpec(memory_space=pl.ANY),
        out_shape=jax.ShapeDtypeStruct((n_slots, half), U32),
        scratch_shapes=[pltpu.SemaphoreType.DMA(())],
        input_output_aliases={3: 0},
        compiler_params=_cparams(1),
        name="moe_dispatch",
    )(slot1, slot2, hp, jnp.zeros((n_slots, half), U32))


def _glu_sorted_kernel(te_ref, used_ref, a_ref, wg_ref, wu_ref, o_ref, w_cat_ref):
    i = pl.program_id(1)
    bn = o_ref.shape[1]
    first_tile_of_expert = (i == 0) | (te_ref[i] != te_ref[jnp.maximum(i - 1, 0)])

    @pl.when(first_tile_of_expert)
    def _():
        w_cat_ref[:, :bn] = wg_ref[...]
        w_cat_ref[:, bn:] = wu_ref[...]

    @pl.when(i < used_ref[0])
    def _():
        lo, hi = _unpack_halves(a_ref[...])
        half = lo.shape[1]
        gu = _dot(lo, w_cat_ref[:half, :]) + _dot(hi, w_cat_ref[half:, :])
        o_ref[...] = (_silu(gu[:, :bn]) * gu[:, bn:]).astype(o_ref.dtype)

    @pl.when(i >= used_ref[0])
    def _():
        o_ref[...] = jnp.zeros_like(o_ref)


def _glu_sorted_call(a_sorted, wg, wu, i_moe, tile_expert, n_used):
    n_slots, half = a_sorted.shape
    d = 2 * half
    de = wg.shape[-1]
    bm = SORT_TILE
    bn = de // GLU_COL_BLOCKS
    w_spec = pl.BlockSpec((None, None, d, bn), lambda j, i, te, nu: (i_moe, te[i], 0, j))
    return pl.pallas_call(
        _glu_sorted_kernel,
        grid_spec=pltpu.PrefetchScalarGridSpec(
            num_scalar_prefetch=2,
            grid=(GLU_COL_BLOCKS, n_slots // bm),
            in_specs=[pl.BlockSpec((bm, half), lambda j, i, te, nu: (i, 0)), w_spec, w_spec],
            out_specs=pl.BlockSpec((bm, bn), lambda j, i, te, nu: (i, j)),
            scratch_shapes=[pltpu.VMEM((d, 2 * bn), BF16)]),
        out_shape=jax.ShapeDtypeStruct((n_slots, de), BF16),
        compiler_params=_cparams(2),
        name="moe_glu",
    )(tile_expert, n_used, a_sorted, wg, wu)


def _down_sorted_kernel(te_ref, used_ref, a_ref, w_ref, o_ref):
    @pl.when(pl.program_id(1) < used_ref[0])
    def _():
        o_ref[...] = _dot(a_ref[...], w_ref[...].astype(BF16))

    @pl.when(pl.program_id(1) >= used_ref[0])
    def _():
        o_ref[...] = jnp.zeros_like(o_ref)


def _down_sorted_call(act, wd, i_moe, tile_expert, n_used):
    n_slots, de = act.shape
    d = wd.shape[-1]
    bm = SORT_TILE
    bn = 1024
    return pl.pallas_call(
        _down_sorted_kernel,
        grid_spec=pltpu.PrefetchScalarGridSpec(
            num_scalar_prefetch=2,
            grid=(d // bn, n_slots // bm),
            in_specs=[pl.BlockSpec((bm, de), lambda j, i, te, nu: (i, 0)),
                      pl.BlockSpec((None, None, de, bn), lambda j, i, te, nu: (i_moe, te[i], 0, j))],
            out_specs=pl.BlockSpec((bm, bn), lambda j, i, te, nu: (i, j))),
        out_shape=jax.ShapeDtypeStruct((n_slots, d), F32),
        compiler_params=_cparams(2),
        name="moe_down",
    )(tile_expert, n_used, act, wd)


def _combine_kernel(s1_cur, s2_cur, s1_next, s2_next, y_ref, route_ref, x_ref, gt_ref, o_ref, buf_ref, sem):
    i = pl.program_id(0)
    n = pl.num_programs(0)
    bm = x_ref.shape[0]

    def gather(s1_ref, s2_ref, buf_slot):
        def issue(j, carry):
            pltpu.make_async_copy(y_ref.at[pl.ds(s1_ref[j], 1)],
                                  buf_ref.at[buf_slot, pl.ds(j, 1)], sem.at[buf_slot]).start()
            pltpu.make_async_copy(y_ref.at[pl.ds(s2_ref[j], 1)],
                                  buf_ref.at[buf_slot, pl.ds(bm + j, 1)], sem.at[buf_slot]).start()
            return carry
        lax.fori_loop(0, bm, issue, 0, unroll=8)

    @pl.when(i == 0)
    def _():
        gather(s1_cur, s2_cur, 0)

    @pl.when(i + 1 < n)
    def _():
        gather(s1_next, s2_next, (i + 1) % 2)

    cur = i % 2
    pltpu.make_async_copy(y_ref.at[pl.ds(0, 2 * bm)], buf_ref.at[cur], sem.at[cur]).wait()

    rec = route_ref[...]
    w1 = rec[:, _R_W1:_R_W1 + 1]
    w2 = rec[:, _R_W2:_R_W2 + 1]
    y = w1 * buf_ref[cur, :bm, :] + w2 * buf_ref[cur, bm:, :]
    o_ref[...] = x_ref[...] + gt_ref[...] * y


def _combine_call(cfg, y_sorted, route, slot1, slot2, x_all, mod_l, n_rows):
    d = cfg.d
    bm = COMBINE_TILE
    bidx = cfg.bidx_for(bm)
    n_tiles = n_rows // bm
    cur = pl.BlockSpec((bm,), lambda i: (i,), memory_space=pltpu.SMEM)
    nxt = pl.BlockSpec((bm,), lambda i: (jnp.minimum(i + 1, n_tiles - 1),), memory_space=pltpu.SMEM)
    return pl.pallas_call(
        _combine_kernel,
        grid=(n_tiles,),
        in_specs=[cur, cur, nxt, nxt,
                  pl.BlockSpec(memory_space=pl.ANY),
                  pl.BlockSpec((bm, LANES), lambda i: (i, 0)),
                  pl.BlockSpec((bm, d), lambda i: (i, 0)),
                  pl.BlockSpec((None, 1, d), lambda i: (bidx(i), 0, 5))],
        out_specs=pl.BlockSpec((bm, d), lambda i: (i, 0)),
        out_shape=jax.ShapeDtypeStruct(x_all.shape, F32),
        scratch_shapes=[pltpu.VMEM((2, 2 * bm, d), F32), pltpu.SemaphoreType.DMA((2,))],
        input_output_aliases={6: 0},
        compiler_params=_cparams(1),
        name="moe_combine",
    )(slot1, slot2, slot1, slot2, y_sorted, route, x_all, mod_l)


def _moe_ffn(cfg, x_all, g, mod_l, w_router, wg, wu, wd, i_moe, n_rows):
    bm = SORT_TILE
    hp, route, counts = _route_call(cfg, x_all, g, mod_l, w_router, i_moe, n_rows)

    cnt = counts[0, :N_EXPERTS].astype(jnp.int32)
    padded = (cnt + bm - 1) // bm * bm
    ends = jnp.cumsum(padded)
    base = ends - padded
    n_tiles = 2 * n_rows // bm + N_EXPERTS
    tile_start = jnp.arange(n_tiles, dtype=jnp.int32) * bm
    tile_expert = jnp.minimum(jnp.sum((tile_start[:, None] >= ends[None, :]).astype(jnp.int32), axis=1),
                              N_EXPERTS - 1)
    n_used = (ends[-1:] // bm).astype(jnp.int32)
    e1, e2, r1, r2 = (route[:, k].astype(jnp.int32) for k in (_R_I1, _R_I2, _R_R1, _R_R2))
    slot1, slot2 = base[e1] + r1, base[e2] + r2

    a_sorted = _dispatch_call(hp, slot1, slot2, n_tiles * bm)
    act = _glu_sorted_call(a_sorted, wg, wu, i_moe, tile_expert, n_used)
    y_sorted = _down_sorted_call(act, wd, i_moe, tile_expert, n_used)
    return _combine_call(cfg, y_sorted, route, slot1, slot2, x_all, mod_l, n_rows)


def _softmax_pv(s, sink_col, v):
    m = jnp.maximum(jnp.max(s, axis=-1, keepdims=True), sink_col)
    p = jnp.exp(s - m)
    denom = jnp.sum(p, axis=-1, keepdims=True) + jnp.exp(sink_col - m)
    return _dot(p.astype(BF16), v) / denom


def _sink_column(sink_ref, group, rows):
    r = lax.broadcasted_iota(jnp.int32, (Q_PER_KV * rows, 1), 0)
    col = jnp.full((Q_PER_KV * rows, 1), sink_ref[group * Q_PER_KV], F32)
    for j in range(1, Q_PER_KV):
        col = jnp.where(r >= j * rows, sink_ref[group * Q_PER_KV + j], col)
    return col


def _attn_lat_kernel(sink_ref, *refs, kvh, t, qb):
    q_refs = refs[:kvh]
    kp_ref, kc_ref, kn_ref, vp_ref, vc_ref, vn_ref, kx_ref, vx_ref, cos_ref, sin_ref, o_ref = refs[kvh:]
    nb = t // BLOCK
    blk0 = pl.program_id(1) * qb
    lane = lax.broadcasted_iota(jnp.int32, (BLOCK, HEAD_DIM), 1)
    first_half = (lane & (HEAD_DIM // 4)) == 0

    def rot(x, blk):
        start = pl.multiple_of(blk * BLOCK, BLOCK)
        c = cos_ref[pl.ds(start, BLOCK), :]
        s = sin_ref[pl.ds(start, BLOCK), :]
        partner = jnp.where(first_half, pltpu.roll(x, HEAD_DIM - HEAD_DIM // 4, 1),
                            pltpu.roll(x, HEAD_DIM // 4, 1))
        return x * c + partner * s

    blk_p = jnp.maximum(blk0 - 1, 0)
    blk_n = jnp.minimum(blk0 + qb, nb - 1)
    n_loc = 3 * BLOCK
    qi = lax.broadcasted_iota(jnp.int32, (BLOCK, n_loc), 0)
    kj = lax.broadcasted_iota(jnp.int32, (BLOCK, n_loc), 1)
    rel = kj - qi
    in_window = (rel >= 0) & (rel <= 2 * BLOCK)
    valid = []
    for a in range(qb):
        s_pos = (blk0 + a) * BLOCK - BLOCK + kj
        valid.append((in_window & (s_pos >= 0) & (s_pos < t))[None])
    log2e = 1.0 / np.log(2.0)
    scale = HEAD_DIM ** -0.5 * log2e

    for g in range(kvh):
        hs = slice(g * HEAD_DIM, (g + 1) * HEAD_DIM)
        own = [slice(a * BLOCK, (a + 1) * BLOCK) for a in range(qb)]
        k_blocks = ([rot(kp_ref[:, hs].astype(F32), blk_p).astype(BF16)]
                    + [rot(kc_ref[own[a], hs].astype(F32), blk0 + a).astype(BF16) for a in range(qb)]
                    + [rot(kn_ref[:, hs].astype(F32), blk_n).astype(BF16)])
        v_blocks = [vp_ref[:, hs]] + [vc_ref[own[a], hs] for a in range(qb)] + [vn_ref[:, hs]]
        sink_col = _sink_column(sink_ref, g, BLOCK) * log2e
        for a in range(qb):
            q = jnp.concatenate(
                [rot(q_refs[g][own[a], j * HEAD_DIM:(j + 1) * HEAD_DIM].astype(F32), blk0 + a) * scale
                 for j in range(Q_PER_KV)], axis=0).astype(BF16)
            k_loc = jnp.concatenate(k_blocks[a:a + 3], axis=0)
            v_loc = jnp.concatenate(v_blocks[a:a + 3], axis=0)
            s_loc = _dot_nt(q, k_loc).reshape(Q_PER_KV, BLOCK, n_loc)
            s_loc = jnp.where(valid[a], s_loc, jnp.finfo(F32).min).reshape(Q_PER_KV * BLOCK, n_loc)
            s_ctx = _dot_nt(q, kx_ref[:, hs])
            m = jnp.maximum(jnp.maximum(jnp.max(s_loc, axis=-1, keepdims=True),
                                        jnp.max(s_ctx, axis=-1, keepdims=True)), sink_col)
            p_loc = jnp.exp2(s_loc - m)
            p_ctx = jnp.exp2(s_ctx - m)
            denom = (jnp.sum(p_loc, axis=-1, keepdims=True) + jnp.sum(p_ctx, axis=-1, keepdims=True)
                     + jnp.exp2(sink_col - m))
            o = (_dot(p_loc.astype(BF16), v_loc) + _dot(p_ctx.astype(BF16), vx_ref[:, hs])) / denom
            for j in range(Q_PER_KV):
                h = g * Q_PER_KV + j
                o_ref[own[a], h * HEAD_DIM:(h + 1) * HEAD_DIM] = o[j * BLOCK:(j + 1) * BLOCK].astype(o_ref.dtype)


def _attn_lat_call(cfg, p, sink, cos_a, sin_a):
    b, t, l = cfg.b, cfg.t, cfg.l
    nb = t // BLOCK
    kvh = cfg.kvh
    gw = Q_PER_KV * HEAD_DIM
    kvw = kvh * HEAD_DIM
    q_blk = cfg.off_qa // gw
    k_blk, v_blk = cfg.off_ka // kvw, cfg.off_va // kvw
    ctx_row0 = (b * t) // l

    qb = np.gcd(ATTN_Q_BLOCKS, nb).item()
    steps = nb // qb

    def own_spec(width, col_blk):
        return pl.BlockSpec((qb * BLOCK, width), lambda bi, n: (bi * steps + n, col_blk))

    def edge_spec(col_blk, shift):
        def imap(bi, n):
            return (bi * nb + jnp.clip(n * qb + shift, 0, nb - 1), col_blk)
        return pl.BlockSpec((BLOCK, kvw), imap)

    in_specs = [pl.BlockSpec(memory_space=pltpu.SMEM)]
    in_specs += [own_spec(gw, q_blk + g) for g in range(kvh)]
    in_specs += [edge_spec(k_blk, -1), own_spec(kvw, k_blk), edge_spec(k_blk, qb),
                 edge_spec(v_blk, -1), own_spec(kvw, v_blk), edge_spec(v_blk, qb),
                 pl.BlockSpec((l, kvw), lambda bi, n: (ctx_row0 + bi, k_blk)),
                 pl.BlockSpec((l, kvw), lambda bi, n: (ctx_row0 + bi, v_blk)),
                 pl.BlockSpec((t, HEAD_DIM), lambda bi, n: (0, 0)),
                 pl.BlockSpec((t, HEAD_DIM), lambda bi, n: (0, 0))]
    return pl.pallas_call(
        functools.partial(_attn_lat_kernel, kvh=kvh, t=t, qb=qb),
        grid=(b, steps),
        in_specs=in_specs,
        out_specs=own_spec(cfg.a_q, 0),
        out_shape=jax.ShapeDtypeStruct((b * t, cfg.a_q), BF16),
        compiler_params=_cparams(2),
        name="attn_latent",
    )(sink, *([p] * (kvh + 8)), cos_a, sin_a)


def _attn_ctx_kernel(sink_ref, *refs, kvh):
    q_refs = refs[:kvh]
    kx_ref, vx_ref, o_ref = refs[kvh:]
    l = kx_ref.shape[0]
    scale = HEAD_DIM ** -0.5
    for g in range(kvh):
        hs = slice(g * HEAD_DIM, (g + 1) * HEAD_DIM)
        q = jnp.concatenate(
            [(q_refs[g][:, j * HEAD_DIM:(j + 1) * HEAD_DIM].astype(F32) * scale).astype(BF16)
             for j in range(Q_PER_KV)], axis=0)
        s = _dot_nt(q, kx_ref[:, hs])
        o = _softmax_pv(s, _sink_column(sink_ref, g, l), vx_ref[:, hs])
        for j in range(Q_PER_KV):
            h = g * Q_PER_KV + j
            o_ref[:, h * HEAD_DIM:(h + 1) * HEAD_DIM] = o[j * l:(j + 1) * l].astype(o_ref.dtype)


def _attn_ctx_call(cfg, p, sink):
    b, t, l = cfg.b, cfg.t, cfg.l
    kvh = cfg.kvh
    gw = Q_PER_KV * HEAD_DIM
    kvw = kvh * HEAD_DIM
    q_blk = cfg.off_qa // gw
    k_blk, v_blk = cfg.off_ka // kvw, cfg.off_va // kvw
    ctx_row0 = (b * t) // l
    in_specs = [pl.BlockSpec(memory_space=pltpu.SMEM)]
    in_specs += [pl.BlockSpec((l, gw), functools.partial(lambda bi, g: (ctx_row0 + bi, q_blk + g), g=g))
                 for g in range(kvh)]
    in_specs += [pl.BlockSpec((l, kvw), lambda bi: (ctx_row0 + bi, k_blk)),
                 pl.BlockSpec((l, kvw), lambda bi: (ctx_row0 + bi, v_blk))]
    return pl.pallas_call(
        functools.partial(_attn_ctx_kernel, kvh=kvh),
        grid=(b,),
        in_specs=in_specs,
        out_specs=pl.BlockSpec((l, cfg.a_q), lambda bi: (bi, 0)),
        out_shape=jax.ShapeDtypeStruct((b * l, cfg.a_q), BF16),
        compiler_params=_cparams(1),
        name="attn_ctx",
    )(sink, *([p] * (kvh + 2)))


def _retention_scan(q_of, k_of, v_ref, g_ref, gn, lg_f, lg_b, o_ref, acc_ref, n_chunks, s_f, s_b):
    c = BLOCK
    ii = lax.broadcasted_iota(jnp.int32, (c, c), 0).astype(F32)
    jj = lax.broadcasted_iota(jnp.int32, (c, c), 1).astype(F32)
    diff = ii - jj
    decay = jnp.where(diff > 0, jnp.exp(lg_f * jnp.maximum(diff, 0.0)),
                      jnp.where(diff < 0, jnp.exp(lg_b * jnp.maximum(-diff, 0.0)), 2.0))
    zeta_f = jnp.exp(lg_f * (c - 1 - ii))
    xi_f = jnp.exp(lg_f * (ii + 1))
    zeta_b = jnp.exp(lg_b * ii)
    xi_b = jnp.exp(lg_b * (c - ii))
    cd_f = jnp.exp(jnp.concatenate([lg_f, lg_f], axis=1) * c)
    cd_b = jnp.exp(jnp.concatenate([lg_b, lg_b], axis=1) * c)

    for i in reversed(range(n_chunks)):
        rows = slice(i * c, (i + 1) * c)
        q, k, v = q_of(i), k_of(i), v_ref[rows, :]
        acc_ref[rows, :] = _dot((q * xi_b).astype(BF16), s_b.astype(BF16))
        s_b = cd_b * s_b + _dot_tn((k * zeta_b).astype(BF16), v)

    for i in range(n_chunks):
        rows = slice(i * c, (i + 1) * c)
        q, k, v = q_of(i), k_of(i), v_ref[rows, :]
        scores = _dot_nt(q.astype(BF16), k.astype(BF16)) * decay
        o = (acc_ref[rows, :] + _dot(scores.astype(BF16), v)
             + _dot((q * xi_f).astype(BF16), s_f.astype(BF16)))
        s_f = cd_f * s_f + _dot_tn((k * zeta_f).astype(BF16), v)
        mu = jnp.mean(o, axis=-1, keepdims=True)
        dev = o - mu
        var = jnp.mean(dev * dev, axis=-1, keepdims=True)
        on = dev * lax.rsqrt(var + EPS) * gn
        gate = g_ref[rows, :].astype(F32)
        o_ref[rows, :] = (_silu(gate) * on).astype(o_ref.dtype)
    return s_f, s_b


def _log_decay(a_row):
    return jnp.log1p(-jnp.exp2(-a_row))


def _retention_kernel(qc_ref, kc_ref, vc_ref, gc_ref, q_ref, k_ref, v_ref, g_ref, gn_ref, af_ref, ab_ref,
                      cos_ref, sin_ref, oc_ref, o_ref, acc_ref, qr_ref, kr_ref):
    scale = HEAD_DIM ** -0.5
    lg_f, lg_b = _log_decay(af_ref[...]), _log_decay(ab_ref[...])
    gn = gn_ref[...]

    def qc_of(i):
        return qc_ref[i * BLOCK:(i + 1) * BLOCK, :].astype(F32)

    def kc_of(i):
        return kc_ref[i * BLOCK:(i + 1) * BLOCK, :].astype(F32) * scale

    zero = jnp.zeros((HEAD_DIM, R_DV), F32)
    s_f, s_b = _retention_scan(qc_of, kc_of, vc_ref, gc_ref, gn, lg_f, lg_b, oc_ref, acc_ref,
                               qc_ref.shape[0] // BLOCK, zero, zero)

    cos, sin = cos_ref[...], sin_ref[...]

    def rot(x):
        return x * cos + pltpu.roll(x, HEAD_DIM // 2, 1) * sin

    qr_ref[...] = rot(q_ref[...].astype(F32))
    kr_ref[...] = rot(k_ref[...].astype(F32)) * scale

    def q_of(i):
        return qr_ref[i * BLOCK:(i + 1) * BLOCK, :]

    def k_of(i):
        return kr_ref[i * BLOCK:(i + 1) * BLOCK, :]

    _retention_scan(q_of, k_of, v_ref, g_ref, gn, lg_f, lg_b, o_ref, acc_ref, q_ref.shape[0] // BLOCK, s_f, s_b)


def _retention_call(cfg, p, gn, a_f, a_b, cos_r, sin_r):
    b, t, l, rh = cfg.b, cfg.t, cfg.l, cfg.rh
    assert l <= t
    ctx_row0 = (b * t) // l
    qb, kb = cfg.off_qr // HEAD_DIM, cfg.off_kr // HEAD_DIM
    vb, gb = cfg.off_vr // R_DV, cfg.off_gr // R_DV

    def head_specs(rows, row_blk):
        return [pl.BlockSpec((rows, HEAD_DIM), lambda bi, h: (row_blk(bi), qb + h)),
                pl.BlockSpec((rows, HEAD_DIM), lambda bi, h: (row_blk(bi), kb + h)),
                pl.BlockSpec((rows, R_DV), lambda bi, h: (row_blk(bi), vb + h)),
                pl.BlockSpec((rows, R_DV), lambda bi, h: (row_blk(bi), gb + h))]

    table_spec = pl.BlockSpec((t, HEAD_DIM), lambda bi, h: (0, 0))
    decay_spec = pl.BlockSpec((None, 1, HEAD_DIM), lambda bi, h: (h, 0, 0))
    in_specs = (head_specs(l, lambda bi: ctx_row0 + bi) + head_specs(t, lambda bi: bi)
                + [pl.BlockSpec((1, R_DV), lambda bi, h: (0, h)), decay_spec, decay_spec, table_spec, table_spec])
    return pl.pallas_call(
        _retention_kernel,
        grid=(b, rh),
        in_specs=in_specs,
        out_specs=[pl.BlockSpec((l, R_DV), lambda bi, h: (bi, h)),
                   pl.BlockSpec((t, R_DV), lambda bi, h: (bi, h))],
        out_shape=[jax.ShapeDtypeStruct((b * l, cfg.r_v), BF16),
                   jax.ShapeDtypeStruct((b * t, cfg.r_v), BF16)],
        scratch_shapes=[pltpu.VMEM((t, R_DV), F32), pltpu.VMEM((t, HEAD_DIM), F32),
                        pltpu.VMEM((t, HEAD_DIM), F32)],
        compiler_params=_cparams(2),
        name="retention",
    )(*([p] * 8), gn, a_f, a_b, cos_r, sin_r)


def _rotary_tables(t):
    rows = t // GRID_W
    row = jnp.repeat(jnp.arange(rows, dtype=F32), GRID_W)
    col = jnp.tile(jnp.arange(GRID_W, dtype=F32), rows)
    nf = HEAD_DIM // 4
    inv = jnp.power(ROPE_BASE, -jnp.arange(nf, dtype=F32) / nf)
    ang_row, ang_col = row[:, None] * inv, col[:, None] * inv
    cos_a = jnp.concatenate([jnp.cos(ang_row)] * 2 + [jnp.cos(ang_col)] * 2, axis=1)
    sin_a = jnp.concatenate([-jnp.sin(ang_row), jnp.sin(ang_row), -jnp.sin(ang_col), jnp.sin(ang_col)], axis=1)
    nf = HEAD_DIM // 2
    inv = jnp.power(ROPE_BASE, -jnp.arange(nf, dtype=F32) / nf)
    ang = jnp.arange(t, dtype=F32)[:, None] * inv
    cos_r = jnp.concatenate([jnp.cos(ang)] * 2, axis=1)
    sin_r = jnp.concatenate([-jnp.sin(ang), jnp.sin(ang)], axis=1)
    return cos_a, sin_a, cos_r, sin_r


class _Config:
    def __init__(self, x, ctx, w_in, w_gate_d):
        self.b, self.t, self.d = x.shape
        self.l = ctx.shape[1]
        self.ntok = self.b * (self.t + self.l)
        self.ha = self.d // 256
        self.kvh = self.ha // Q_PER_KV
        self.rh = self.d // 256
        self.a_q = self.ha * HEAD_DIM
        self.a_kv = self.kvh * HEAD_DIM
        self.r_qk = self.rh * HEAD_DIM
        self.r_v = self.rh * R_DV
        self.off_ka = 0
        self.off_va = self.a_kv
        self.off_kr = 2 * self.a_kv
        self.off_vr = self.off_kr + self.r_qk
        self.off_qa = self.off_vr + self.r_v
        self.off_qr = self.off_qa + self.a_q
        self.off_gr = self.off_qr + self.r_qk
        self.off_za = self.off_gr + self.r_v
        self.off_zr = self.off_za + self.d
        assert w_in.shape[-1] == self.off_zr + self.d
        self.bm = min(ROW_TILE, self.t)
        self.norm_bm = min(NORM_ROW_TILE, self.t)
        assert self.t % self.bm == 0 and (self.b * self.l) % self.bm == 0
        assert self.t % GRID_W == 0 and self.t % BLOCK == 0 and self.l % BLOCK == 0

    def bidx_for(self, tile):
        assert self.t % tile == 0 and (self.b * self.l) % tile == 0
        tiles_per_batch = self.t // tile
        n_batch = self.b
        return lambda i: jnp.minimum(i // tiles_per_batch, n_batch)


def kernel(x, c, ctx, c_ctx, w_ada, b_ada, g_mix, g_ffn, w_in, attn_sink, ret_a_fwd, ret_a_bwd, ret_gn,
           w_br_attn, w_br_ret, w_out, w_gate_d, w_up_d, w_down_d, w_router, w_gate_e, w_up_e, w_down_e,
           g_final):
    cfg = _Config(x, ctx, w_in, w_gate_d)
    b, t, d, l = cfg.b, cfg.t, cfg.d, cfg.l
    depth = w_ada.shape[0]
    n_lat = b * t

    x_all = (x.reshape(n_lat, d), ctx.reshape(b * l, d))
    mod_rows = -(-(b + 1) // 8) * 8
    c_all = jnp.zeros((mod_rows, d), F32).at[:b].set(c).at[b].set(c_ctx)
    mod = _ada_call(c_all, w_ada, b_ada).reshape(depth, mod_rows, 1, 6 * d)

    cos_a, sin_a, cos_r, sin_r = _rotary_tables(t)
    w_br_attn_h, w_br_ret_h, w_out_h, w_down_d_h = (w.astype(BF16) for w in (w_br_attn, w_br_ret, w_out, w_down_d))
    w_gate_e_h, w_up_e_h = w_gate_e.astype(BF16), w_up_e.astype(BF16)
    w_router_h = jnp.pad(w_router, ((0, 0), (0, 0), (0, LANES - N_EXPERTS))).astype(BF16)
    a_f = jnp.broadcast_to(ret_a_fwd.astype(F32)[:, :, None, None], ret_a_fwd.shape + (1, HEAD_DIM))
    a_b = jnp.broadcast_to(ret_a_bwd.astype(F32)[:, :, None, None], ret_a_bwd.shape + (1, HEAD_DIM))

    for layer in range(depth):
        last = layer == depth - 1
        n_rows = n_lat if last else cfg.ntok
        mod_l = mod[layer]
        gn = ret_gn[layer].reshape(1, cfg.r_v)

        h = _norm_mod_call(cfg, x_all, g_mix[layer], mod_l, 0, cfg.ntok)
        p = _in_proj_call(cfg, h, w_in, layer, ctx_kv_only=last)
        y_a = _attn_lat_call(cfg, p, attn_sink[layer], cos_a, sin_a)
        y_r_ctx, y_r = _retention_call(cfg, p, gn, a_f[layer], a_b[layer], cos_r, sin_r)
        if last:
            y_a_ctx, y_r_ctx = y_a, y_r
        else:
            y_a_ctx = _attn_ctx_call(cfg, p, attn_sink[layer])
        m = _merge_call(cfg, y_a, y_r, y_a_ctx, y_r_ctx, w_br_attn_h, w_br_ret_h, layer, p, n_rows)
        x_all = _resid_call(cfg, m, w_out_h, layer, x_all, mod_l, 2, n_rows, bn=1024)

        i = layer // 2
        if layer % 2 == 0:
            hf = _norm_mod_call(cfg, x_all, g_ffn[layer], mod_l, 3, n_rows)
            act = _glu_call(cfg, hf, w_gate_d, w_up_d, i, n_rows)
            x_all = _resid_call(cfg, act, w_down_d_h, i, x_all, mod_l, 5, n_rows, bn=512)
        else:
            x_all = _moe_ffn(cfg, x_all, g_ffn[layer], mod_l, w_router_h, w_gate_e_h, w_up_e_h, w_down_e,
                             i, n_rows)

    return _final_norm_call(cfg, x_all, g_final).reshape(b, t, d)
```

```python
import functools

import numpy as np
import jax
import jax.numpy as jnp
from jax import lax
from jax.experimental import pallas as pl
from jax.experimental.pallas import tpu as pltpu

F32 = jnp.float32
BF16 = jnp.bfloat16

EPS = 1e-6
ROPE_BASE = 10000.0
GRID_W = 64
HEAD_DIM = 128
R_DV = 256
BLOCK = 128
Q_PER_KV = 4
N_EXPERTS = 8
LANES = 128

V7X_VMEM_BYTES = 64 * 1024 * 1024
VMEM_LIMIT = V7X_VMEM_BYTES - 8 * 1024 * 1024
ROW_TILE = 1024
NORM_ROW_TILE = 512
SORT_TILE = 512
DISPATCH_TILE = 512
COMBINE_TILE = 256
GLU_COL_BLOCKS = 2
ATTN_Q_BLOCKS = 8
RET_HEADS_PER_STEP = 2
U32 = jnp.uint32


def _cparams(n_axes):
    return pltpu.CompilerParams(dimension_semantics=("arbitrary",) * n_axes,
                                vmem_limit_bytes=VMEM_LIMIT)


def _dot(a, b):
    return jnp.dot(a, b, preferred_element_type=F32)


def _dot_nt(a, b):
    return lax.dot_general(a, b, (((1,), (1,)), ((), ())), preferred_element_type=F32)


def _dot_tn(a, b):
    return lax.dot_general(a, b, (((0,), (0,)), ((), ())), preferred_element_type=F32)


def _silu(x):
    return x * jax.nn.sigmoid(x)


def _ada_kernel(c_ref, w_ref, b_ref, o_ref):
    s = _silu(c_ref[...]).astype(BF16)
    o_ref[...] = _dot(s, w_ref[...].astype(BF16)) + b_ref[...]


def _ada_call(c_all, w_ada, b_ada):
    depth, d, n6 = w_ada.shape
    r = c_all.shape[0]
    bn = 1024
    return pl.pallas_call(
        _ada_kernel,
        grid=(depth, n6 // bn),
        in_specs=[pl.BlockSpec((r, d), lambda l, j: (0, 0)),
                  pl.BlockSpec((None, d, bn), lambda l, j: (l, 0, j)),
                  pl.BlockSpec((None, 1, bn), lambda l, j: (l, 0, j))],
        out_specs=pl.BlockSpec((None, r, bn), lambda l, j: (l, 0, j)),
        out_shape=jax.ShapeDtypeStruct((depth, r, n6), F32),
        compiler_params=_cparams(2),
        name="ada_mod",
    )(c_all, w_ada, b_ada.reshape(depth, 1, n6))


def _norm_mod_kernel(x_ref, g_ref, sh_ref, sc_ref, o_ref):
    x = x_ref[...]
    gain = g_ref[...] * (1.0 + sc_ref[...])
    y = x * lax.rsqrt(jnp.mean(x * x, axis=-1, keepdims=True) + EPS)
    o_ref[...] = (y * gain + sh_ref[...]).astype(o_ref.dtype)


def _norm_kernel(x_ref, g_ref, o_ref):
    x = x_ref[...]
    o_ref[...] = x * lax.rsqrt(jnp.mean(x * x, axis=-1, keepdims=True) + EPS) * g_ref[...]


def _two_stream(kernel_fn, lat_tiles):
    def wrapped(x_lat_ref, x_ctx_ref, *rest):
        @pl.when(pl.program_id(0) < lat_tiles)
        def _():
            kernel_fn(x_lat_ref, *rest)

        @pl.when(pl.program_id(0) >= lat_tiles)
        def _():
            kernel_fn(x_ctx_ref, *rest)
    return wrapped


def _norm_mod_call(cfg, x_src, g, mod_l, shift_chunk, n_rows):
    d, bm = cfg.d, cfg.norm_bm
    bidx = cfg.bidx_for(bm)
    if isinstance(x_src, tuple):
        lat_tiles = (cfg.b * cfg.t) // bm
        kern = _two_stream(_norm_mod_kernel, lat_tiles)
        x_specs = [pl.BlockSpec((bm, d), lambda i: (jnp.minimum(i, lat_tiles - 1), 0)),
                   pl.BlockSpec((bm, d), lambda i: (jnp.maximum(i - lat_tiles, 0), 0))]
        x_args = list(x_src)
    else:
        kern, x_specs, x_args = _norm_mod_kernel, [pl.BlockSpec((bm, d), lambda i: (i, 0))], [x_src]
    return pl.pallas_call(
        kern,
        grid=(n_rows // bm,),
        in_specs=x_specs + [pl.BlockSpec((1, d), lambda i: (0, 0)),
                            pl.BlockSpec((None, 1, d), lambda i: (bidx(i), 0, shift_chunk)),
                            pl.BlockSpec((None, 1, d), lambda i: (bidx(i), 0, shift_chunk + 1))],
        out_specs=pl.BlockSpec((bm, d), lambda i: (i, 0)),
        out_shape=jax.ShapeDtypeStruct((n_rows, d), BF16),
        compiler_params=_cparams(1),
        name="norm_mod",
    )(*x_args, g.reshape(1, d), mod_l, mod_l)


def _final_norm_call(cfg, x_all, g):
    d, bm = cfg.d, cfg.norm_bm
    n_rows = cfg.b * cfg.t
    return pl.pallas_call(
        _norm_kernel,
        grid=(n_rows // bm,),
        in_specs=[pl.BlockSpec((bm, d), lambda i: (i, 0)),
                  pl.BlockSpec((1, d), lambda i: (0, 0))],
        out_specs=pl.BlockSpec((bm, d), lambda i: (i, 0)),
        out_shape=jax.ShapeDtypeStruct((n_rows, d), F32),
        compiler_params=_cparams(1),
        name="final_norm",
    )(x_all, g.reshape(1, d))


def _mm_kernel(a_ref, w_ref, o_ref):
    o_ref[...] = _dot(a_ref[...], w_ref[...].astype(BF16)).astype(o_ref.dtype)


def _wide_tile(cfg, n_rows):
    return 2 * cfg.bm if n_rows % (2 * cfg.bm) == 0 else cfg.bm


def _mm_ctx_cols_kernel(a_ref, w_ref, o_ref, *, lat_tiles, ctx_blocks):
    unused = (pl.program_id(0) >= lat_tiles) & (pl.program_id(1) >= ctx_blocks)

    @pl.when(jnp.logical_not(unused))
    def _():
        _mm_kernel(a_ref, w_ref, o_ref)

    @pl.when(unused)
    def _():
        o_ref[...] = jnp.zeros_like(o_ref)


def _in_proj_call(cfg, h, w, layer, ctx_kv_only):
    d, bm = cfg.d, _wide_tile(cfg, cfg.ntok)
    n = w.shape[-1]
    bn = 512
    kern = _mm_kernel
    if ctx_kv_only and (cfg.b * cfg.t) % bm == 0 and cfg.off_qa % bn == 0:
        kern = functools.partial(_mm_ctx_cols_kernel, lat_tiles=(cfg.b * cfg.t) // bm,
                                 ctx_blocks=cfg.off_qa // bn)
    return pl.pallas_call(
        kern,
        grid=(cfg.ntok // bm, n // bn),
        in_specs=[pl.BlockSpec((bm, d), lambda i, j: (i, 0)),
                  pl.BlockSpec((None, d, bn), lambda i, j: (layer, 0, j))],
        out_specs=pl.BlockSpec((bm, bn), lambda i, j: (i, j)),
        out_shape=jax.ShapeDtypeStruct((cfg.ntok, n), BF16),
        compiler_params=_cparams(2),
        name="in_proj",
    )(h, w)


def _merge_kernel(ya_ref, yr_ref, yac_ref, yrc_ref, wa_ref, wr_ref, za_ref, zr_ref, o_ref, *, lat_tiles):
    def merge(ya, yr):
        a = _dot(ya[...], wa_ref[...])
        r = _dot(yr[...], wr_ref[...])
        za = za_ref[...].astype(F32)
        zr = zr_ref[...].astype(F32)
        o_ref[...] = (jax.nn.sigmoid(za) * a + jax.nn.sigmoid(zr) * r).astype(o_ref.dtype)

    i = pl.program_id(0)

    @pl.when(i < lat_tiles)
    def _():
        merge(ya_ref, yr_ref)

    @pl.when(i >= lat_tiles)
    def _():
        merge(yac_ref, yrc_ref)


def _merge_call(cfg, y_a, y_r, y_a_ctx, y_r_ctx, w_a, w_r, layer, p, n_rows):
    d, bm = cfg.d, cfg.bm
    bn = 512
    za_blk, zr_blk = cfg.off_za // bn, cfg.off_zr // bn
    lat_tiles = (cfg.b * cfg.t) // bm

    def lat(i, j):
        return (jnp.minimum(i, lat_tiles - 1), 0)

    def ctx(i, j):
        return (jnp.maximum(i - lat_tiles, 0), 0)

    return pl.pallas_call(
        functools.partial(_merge_kernel, lat_tiles=lat_tiles),
        grid=(n_rows // bm, d // bn),
        in_specs=[pl.BlockSpec((bm, cfg.a_q), lat),
                  pl.BlockSpec((bm, cfg.r_v), lat),
                  pl.BlockSpec((bm, cfg.a_q), ctx),
                  pl.BlockSpec((bm, cfg.r_v), ctx),
                  pl.BlockSpec((None, cfg.a_q, bn), lambda i, j: (layer, 0, j)),
                  pl.BlockSpec((None, cfg.r_v, bn), lambda i, j: (layer, 0, j)),
                  pl.BlockSpec((bm, bn), lambda i, j: (i, za_blk + j)),
                  pl.BlockSpec((bm, bn), lambda i, j: (i, zr_blk + j))],
        out_specs=pl.BlockSpec((bm, bn), lambda i, j: (i, j)),
        out_shape=jax.ShapeDtypeStruct((n_rows, d), BF16),
        compiler_params=_cparams(2),
        name="merge",
    )(y_a, y_r, y_a_ctx, y_r_ctx, w_a, w_r, p, p)


def _resid_kernel(a_ref, w_ref, x_ref, gt_ref, o_ref):
    o_ref[...] = x_ref[...] + gt_ref[...] * _dot(a_ref[...], w_ref[...])


def _resid_two_stream_kernel(a_ref, w_ref, x_lat_ref, x_ctx_ref, gt_ref, o_ref, *, lat_tiles):
    @pl.when(pl.program_id(0) < lat_tiles)
    def _():
        _resid_kernel(a_ref, w_ref, x_lat_ref, gt_ref, o_ref)

    @pl.when(pl.program_id(0) >= lat_tiles)
    def _():
        _resid_kernel(a_ref, w_ref, x_ctx_ref, gt_ref, o_ref)


def _resid_call(cfg, a, w, widx, x_src, mod_l, gate_chunk, n_rows, bn):
    d, bm = cfg.d, cfg.bm
    k = a.shape[1]
    bidx = cfg.bidx_for(bm)
    gblk = gate_chunk * (d // bn)
    if isinstance(x_src, tuple):
        lat_tiles = (cfg.b * cfg.t) // bm
        kern = functools.partial(_resid_two_stream_kernel, lat_tiles=lat_tiles)
        x_specs = [pl.BlockSpec((bm, bn), lambda i, j: (jnp.minimum(i, lat_tiles - 1),
                                                        jnp.where(i < lat_tiles, j, 0))),
                   pl.BlockSpec((bm, bn), lambda i, j: (jnp.maximum(i - lat_tiles, 0),
                                                        jnp.where(i >= lat_tiles, j, 0)))]
        x_args, aliases, out_rows = list(x_src), {}, n_rows
    else:
        kern, x_specs = _resid_kernel, [pl.BlockSpec((bm, bn), lambda i, j: (i, j))]
        x_args, aliases, out_rows = [x_src], {2: 0}, x_src.shape[0]
    return pl.pallas_call(
        kern,
        grid=(n_rows // bm, d // bn),
        in_specs=[pl.BlockSpec((bm, k), lambda i, j: (i, 0)),
                  pl.BlockSpec((None, k, bn), lambda i, j: (widx, 0, j))] + x_specs
                 + [pl.BlockSpec((None, 1, bn), lambda i, j: (bidx(i), 0, gblk + j))],
        out_specs=pl.BlockSpec((bm, bn), lambda i, j: (i, j)),
        out_shape=jax.ShapeDtypeStruct((out_rows, d), F32),
        input_output_aliases=aliases,
        compiler_params=_cparams(2),
        name="resid_proj",
    )(a, w, *x_args, mod_l)


def _glu_kernel(a_ref, wg_ref, wu_ref, o_ref):
    a = a_ref[...]
    g = _dot(a, wg_ref[...].astype(BF16))
    u = _dot(a, wu_ref[...].astype(BF16))
    o_ref[...] = (_silu(g) * u).astype(o_ref.dtype)


def _glu_call(cfg, h, wg, wu, widx, n_rows):
    d, bm = cfg.d, _wide_tile(cfg, n_rows)
    n = wg.shape[-1]
    bn = 512
    w_spec = pl.BlockSpec((None, d, bn), lambda i, j: (widx, 0, j))
    return pl.pallas_call(
        _glu_kernel,
        grid=(n_rows // bm, n // bn),
        in_specs=[pl.BlockSpec((bm, d), lambda i, j: (i, 0)), w_spec, w_spec],
        out_specs=pl.BlockSpec((bm, bn), lambda i, j: (i, j)),
        out_shape=jax.ShapeDtypeStruct((n_rows, n), BF16),
        compiler_params=_cparams(2),
        name="glu_up",
    )(h, wg, wu)


_R_I1, _R_I2, _R_W1, _R_W2, _R_R1, _R_R2 = range(6)


def _pack_halves(h):
    half = h.shape[1] // 2
    lo = lax.bitcast_convert_type(h[:, :half].astype(F32), U32)
    hi = lax.bitcast_convert_type(h[:, half:].astype(F32), U32)
    return (hi & jnp.uint32(0xFFFF0000)) | lax.shift_right_logical(lo, jnp.uint32(16))


def _unpack_halves(packed):
    lo = lax.bitcast_convert_type(lax.shift_left(packed, jnp.uint32(16)), F32)
    hi = lax.bitcast_convert_type(packed & jnp.uint32(0xFFFF0000), F32)
    return lo.astype(BF16), hi.astype(BF16)


def _route_kernel(x_ref, g_ref, sh_ref, sc_ref, wr_ref, hp_ref, route_ref, cnt_ref, carry_ref):
    @pl.when(pl.program_id(0) == 0)
    def _():
        carry_ref[...] = jnp.zeros_like(carry_ref)

    x = x_ref[...]
    gain = g_ref[...] * (1.0 + sc_ref[...])
    y = x * lax.rsqrt(jnp.mean(x * x, axis=-1, keepdims=True) + EPS)
    h = (y * gain + sh_ref[...]).astype(BF16)
    hp_ref[...] = _pack_halves(h)

    logits = _dot(h, wr_ref[...])
    rows = logits.shape[0]
    lane = lax.broadcasted_iota(jnp.int32, logits.shape, 1).astype(F32)
    neg = -jnp.inf
    lg = jnp.where(lane < N_EXPERTS, logits, neg)
    m1 = jnp.max(lg, axis=-1, keepdims=True)
    i1 = jnp.min(jnp.where(lg == m1, lane, float(LANES)), axis=-1, keepdims=True)
    lg2 = jnp.where(lane == i1, neg, lg)
    m2 = jnp.max(lg2, axis=-1, keepdims=True)
    i2 = jnp.min(jnp.where(lg2 == m2, lane, float(LANES)), axis=-1, keepdims=True)
    e = jnp.exp(m2 - m1)
    w1 = 1.0 / (1.0 + e)
    w2 = e / (1.0 + e)

    chosen = jnp.where((lane == i1) | (lane == i2), 1.0, 0.0)
    rr = lax.broadcasted_iota(jnp.int32, (rows, rows), 0)
    cc = lax.broadcasted_iota(jnp.int32, (rows, rows), 1)
    earlier = jnp.where(cc < rr, 1.0, 0.0).astype(BF16)
    rank = _dot(earlier, chosen.astype(BF16)) + carry_ref[...]
    r1 = jnp.sum(jnp.where(lane == i1, rank, 0.0), axis=-1, keepdims=True)
    r2 = jnp.sum(jnp.where(lane == i2, rank, 0.0), axis=-1, keepdims=True)
    carry_ref[...] += jnp.sum(chosen, axis=0, keepdims=True)
    cnt_ref[...] = carry_ref[...]

    rec = jnp.zeros_like(logits)
    for slot, val in ((_R_I1, i1), (_R_I2, i2), (_R_W1, w1), (_R_W2, w2), (_R_R1, r1), (_R_R2, r2)):
        rec = jnp.where(lane == slot, val, rec)
    route_ref[...] = rec


def _route_call(cfg, x_all, g, mod_l, w_router, i_moe, n_rows):
    d, bm = cfg.d, cfg.norm_bm
    bidx = cfg.bidx_for(bm)
    return pl.pallas_call(
        _route_kernel,
        grid=(n_rows // bm,),
        in_specs=[pl.BlockSpec((bm, d), lambda i: (i, 0)),
                  pl.BlockSpec((1, d), lambda i: (0, 0)),
                  pl.BlockSpec((None, 1, d), lambda i: (bidx(i), 0, 3)),
                  pl.BlockSpec((None, 1, d), lambda i: (bidx(i), 0, 4)),
                  pl.BlockSpec((None, d, LANES), lambda i: (i_moe, 0, 0))],
        out_specs=[pl.BlockSpec((bm, d // 2), lambda i: (i, 0)),
                   pl.BlockSpec((bm, LANES), lambda i: (i, 0)),
                   pl.BlockSpec((1, LANES), lambda i: (0, 0))],
        out_shape=[jax.ShapeDtypeStruct((n_rows, d // 2), U32),
                   jax.ShapeDtypeStruct((n_rows, LANES), F32),
                   jax.ShapeDtypeStruct((1, LANES), F32)],
        scratch_shapes=[pltpu.VMEM((1, LANES), F32)],
        compiler_params=_cparams(1),
        name="route",
    )(x_all, g.reshape(1, d), mod_l, mod_l, w_router)


def _dispatch_kernel(s1_ref, s2_ref, src_ref, _, dst_ref, sem):
    tile = s1_ref.shape[0]

    def issue(j, carry):
        src = src_ref.at[pl.ds(j, 1)]
        pltpu.make_async_copy(src, dst_ref.at[pl.ds(s1_ref[j], 1)], sem).start()
        pltpu.make_async_copy(src, dst_ref.at[pl.ds(s2_ref[j], 1)], sem).start()
        return carry

    lax.fori_loop(0, tile, issue, 0, unroll=8)

    for _ in range(2):
        pltpu.make_async_copy(src_ref, dst_ref.at[pl.ds(0, tile)], sem).wait()


def _dispatch_call(hp, slot1, slot2, n_slots):
    n_rows, half = hp.shape
    tile = DISPATCH_TILE
    smem_tile = pl.BlockSpec((tile,), lambda i: (i,), memory_space=pltpu.SMEM)
    return pl.pallas_call(
        _dispatch_kernel,
        grid=(n_rows // tile,),
        in_specs=[smem_tile, smem_tile,
                  pl.BlockSpec((tile, half), lambda i: (i, 0)), pl.BlockSpec(memory_space=pl.ANY)],
        out_specs=pl.BlockSpec(memory_space=pl.ANY),
        out_shape=jax.ShapeDtypeStruct((n_slots, half), U32),
        scratch_shapes=[pltpu.SemaphoreType.DMA(())],
        input_output_aliases={3: 0},
        compiler_params=_cparams(1),
        name="moe_dispatch",
    )(slot1, slot2, hp, jnp.zeros((n_slots, half), U32))


def _glu_sorted_kernel(te_ref, used_ref, a_ref, wg_ref, wu_ref, o_ref, w_cat_ref):
    i = pl.program_id(1)
    bn = o_ref.shape[1]
    first_tile_of_expert = (i == 0) | (te_ref[i] != te_ref[jnp.maximum(i - 1, 0)])

    @pl.when(first_tile_of_expert)
    def _():
        w_cat_ref[:, :bn] = wg_ref[...]
        w_cat_ref[:, bn:] = wu_ref[...]

    @pl.when(i < used_ref[0])
    def _():
        lo, hi = _unpack_halves(a_ref[...])
        half = lo.shape[1]
        gu = _dot(lo, w_cat_ref[:half, :]) + _dot(hi, w_cat_ref[half:, :])
        o_ref[...] = (_silu(gu[:, :bn]) * gu[:, bn:]).astype(o_ref.dtype)

    @pl.when(i >= used_ref[0])
    def _():
        o_ref[...] = jnp.zeros_like(o_ref)


def _glu_sorted_call(a_sorted, wg, wu, i_moe, tile_expert, n_used):
    n_slots, half = a_sorted.shape
    d = 2 * half
    de = wg.shape[-1]
    bm = SORT_TILE
    bn = de // GLU_COL_BLOCKS
    w_spec = pl.BlockSpec((None, None, d, bn), lambda j, i, te, nu: (i_moe, te[i], 0, j))
    return pl.pallas_call(
        _glu_sorted_kernel,
        grid_spec=pltpu.PrefetchScalarGridSpec(
            num_scalar_prefetch=2,
            grid=(GLU_COL_BLOCKS, n_slots // bm),
            in_specs=[pl.BlockSpec((bm, half), lambda j, i, te, nu: (i, 0)), w_spec, w_spec],
            out_specs=pl.BlockSpec((bm, bn), lambda j, i, te, nu: (i, j)),
            scratch_shapes=[pltpu.VMEM((d, 2 * bn), BF16)]),
        out_shape=jax.ShapeDtypeStruct((n_slots, de), BF16),
        compiler_params=_cparams(2),
        name="moe_glu",
    )(tile_expert, n_used, a_sorted, wg, wu)


def _down_sorted_kernel(te_ref, used_ref, a_ref, w_ref, o_ref):
    @pl.when(pl.program_id(1) < used_ref[0])
    def _():
        o_ref[...] = _dot(a_ref[...], w_ref[...].astype(BF16))

    @pl.when(pl.program_id(1) >= used_ref[0])
    def _():
        o_ref[...] = jnp.zeros_like(o_ref)


def _down_sorted_call(act, wd, i_moe, tile_expert, n_used):
    n_slots, de = act.shape
    d = wd.shape[-1]
    bm = SORT_TILE
    bn = 1024
    return pl.pallas_call(
        _down_sorted_kernel,
        grid_spec=pltpu.PrefetchScalarGridSpec(
            num_scalar_prefetch=2,
            grid=(d // bn, n_slots // bm),
            in_specs=[pl.BlockSpec((bm, de), lambda j, i, te, nu: (i, 0)),
                      pl.BlockSpec((None, None, de, bn), lambda j, i, te, nu: (i_moe, te[i], 0, j))],
            out_specs=pl.BlockSpec((bm, bn), lambda j, i, te, nu: (i, j))),
        out_shape=jax.ShapeDtypeStruct((n_slots, d), F32),
        compiler_params=_cparams(2),
        name="moe_down",
    )(tile_expert, n_used, act, wd)


def _combine_kernel(s1_cur, s2_cur, s1_next, s2_next, y_ref, route_ref, x_ref, gt_ref, o_ref, buf_ref, sem):
    i = pl.program_id(0)
    n = pl.num_programs(0)
    bm = x_ref.shape[0]

    def gather(s1_ref, s2_ref, buf_slot):
        def issue(j, carry):
            pltpu.make_async_copy(y_ref.at[pl.ds(s1_ref[j], 1)],
                                  buf_ref.at[buf_slot, pl.ds(j, 1)], sem.at[buf_slot]).start()
            pltpu.make_async_copy(y_ref.at[pl.ds(s2_ref[j], 1)],
                                  buf_ref.at[buf_slot, pl.ds(bm + j, 1)], sem.at[buf_slot]).start()
            return carry
        lax.fori_loop(0, bm, issue, 0, unroll=8)

    @pl.when(i == 0)
    def _():
        gather(s1_cur, s2_cur, 0)

    @pl.when(i + 1 < n)
    def _():
        gather(s1_next, s2_next, (i + 1) % 2)

    cur = i % 2
    pltpu.make_async_copy(y_ref.at[pl.ds(0, 2 * bm)], buf_ref.at[cur], sem.at[cur]).wait()

    rec = route_ref[...]
    w1 = rec[:, _R_W1:_R_W1 + 1]
    w2 = rec[:, _R_W2:_R_W2 + 1]
    y = w1 * buf_ref[cur, :bm, :] + w2 * buf_ref[cur, bm:, :]
    o_ref[...] = x_ref[...] + gt_ref[...] * y


def _combine_call(cfg, y_sorted, route, slot1, slot2, x_all, mod_l, n_rows):
    d = cfg.d
    bm = COMBINE_TILE
    bidx = cfg.bidx_for(bm)
    n_tiles = n_rows // bm
    cur = pl.BlockSpec((bm,), lambda i: (i,), memory_space=pltpu.SMEM)
    nxt = pl.BlockSpec((bm,), lambda i: (jnp.minimum(i + 1, n_tiles - 1),), memory_space=pltpu.SMEM)
    return pl.pallas_call(
        _combine_kernel,
        grid=(n_tiles,),
        in_specs=[cur, cur, nxt, nxt,
                  pl.BlockSpec(memory_space=pl.ANY),
                  pl.BlockSpec((bm, LANES), lambda i: (i, 0)),
                  pl.BlockSpec((bm, d), lambda i: (i, 0)),
                  pl.BlockSpec((None, 1, d), lambda i: (bidx(i), 0, 5))],
        out_specs=pl.BlockSpec((bm, d), lambda i: (i, 0)),
        out_shape=jax.ShapeDtypeStruct(x_all.shape, F32),
        scratch_shapes=[pltpu.VMEM((2, 2 * bm, d), F32), pltpu.SemaphoreType.DMA((2,))],
        input_output_aliases={6: 0},
        compiler_params=_cparams(1),
        name="moe_combine",
    )(slot1, slot2, slot1, slot2, y_sorted, route, x_all, mod_l)


def _moe_ffn(cfg, x_all, g, mod_l, w_router, wg, wu, wd, i_moe, n_rows):
    bm = SORT_TILE
    hp, route, counts = _route_call(cfg, x_all, g, mod_l, w_router, i_moe, n_rows)

    cnt = counts[0, :N_EXPERTS].astype(jnp.int32)
    padded = (cnt + bm - 1) // bm * bm
    ends = jnp.cumsum(padded)
    base = ends - padded
    n_tiles = 2 * n_rows // bm + N_EXPERTS
    tile_start = jnp.arange(n_tiles, dtype=jnp.int32) * bm
    tile_expert = jnp.minimum(jnp.sum((tile_start[:, None] >= ends[None, :]).astype(jnp.int32), axis=1),
                              N_EXPERTS - 1)
    n_used = (ends[-1:] // bm).astype(jnp.int32)
    e1, e2, r1, r2 = (route[:, k].astype(jnp.int32) for k in (_R_I1, _R_I2, _R_R1, _R_R2))
    slot1, slot2 = base[e1] + r1, base[e2] + r2

    a_sorted = _dispatch_call(hp, slot1, slot2, n_tiles * bm)
    act = _glu_sorted_call(a_sorted, wg, wu, i_moe, tile_expert, n_used)
    y_sorted = _down_sorted_call(act, wd, i_moe, tile_expert, n_used)
    return _combine_call(cfg, y_sorted, route, slot1, slot2, x_all, mod_l, n_rows)


def _softmax_pv(s, sink_col, v):
    m = jnp.maximum(jnp.max(s, axis=-1, keepdims=True), sink_col)
    p = jnp.exp(s - m)
    denom = jnp.sum(p, axis=-1, keepdims=True) + jnp.exp(sink_col - m)
    return _dot(p.astype(BF16), v) / denom


def _sink_column(sink_ref, group, rows):
    r = lax.broadcasted_iota(jnp.int32, (Q_PER_KV * rows, 1), 0)
    col = jnp.full((Q_PER_KV * rows, 1), sink_ref[group * Q_PER_KV], F32)
    for j in range(1, Q_PER_KV):
        col = jnp.where(r >= j * rows, sink_ref[group * Q_PER_KV + j], col)
    return col


def _attn_lat_kernel(sink_ref, *refs, kvh, t, qb):
    q_refs = refs[:kvh]
    kp_ref, kc_ref, kn_ref, vp_ref, vc_ref, vn_ref, kx_ref, vx_ref, cos_ref, sin_ref, o_ref = refs[kvh:]
    nb = t // BLOCK
    blk0 = pl.program_id(1) * qb
    lane = lax.broadcasted_iota(jnp.int32, (BLOCK, HEAD_DIM), 1)
    first_half = (lane & (HEAD_DIM // 4)) == 0

    def rot(x, blk):
        start = pl.multiple_of(blk * BLOCK, BLOCK)
        c = cos_ref[pl.ds(start, BLOCK), :]
        s = sin_ref[pl.ds(start, BLOCK), :]
        partner = jnp.where(first_half, pltpu.roll(x, HEAD_DIM - HEAD_DIM // 4, 1),
                            pltpu.roll(x, HEAD_DIM // 4, 1))
        return x * c + partner * s

    blk_p = jnp.maximum(blk0 - 1, 0)
    blk_n = jnp.minimum(blk0 + qb, nb - 1)
    n_loc = 3 * BLOCK
    qi = lax.broadcasted_iota(jnp.int32, (BLOCK, n_loc), 0)
    kj = lax.broadcasted_iota(jnp.int32, (BLOCK, n_loc), 1)
    rel = kj - qi
    in_window = (rel >= 0) & (rel <= 2 * BLOCK)
    valid = []
    for a in range(qb):
        s_pos = (blk0 + a) * BLOCK - BLOCK + kj
        valid.append((in_window & (s_pos >= 0) & (s_pos < t))[None])
    log2e = 1.0 / np.log(2.0)
    scale = HEAD_DIM ** -0.5 * log2e

    for g in range(kvh):
        hs = slice(g * HEAD_DIM, (g + 1) * HEAD_DIM)
        own = [slice(a * BLOCK, (a + 1) * BLOCK) for a in range(qb)]
        k_blocks = ([rot(kp_ref[:, hs].astype(F32), blk_p).astype(BF16)]
                    + [rot(kc_ref[own[a], hs].astype(F32), blk0 + a).astype(BF16) for a in range(qb)]
                    + [rot(kn_ref[:, hs].astype(F32), blk_n).astype(BF16)])
        v_blocks = [vp_ref[:, hs]] + [vc_ref[own[a], hs] for a in range(qb)] + [vn_ref[:, hs]]
        sink_col = _sink_column(sink_ref, g, BLOCK) * log2e
        for a in range(qb):
            q = jnp.concatenate(
                [rot(q_refs[g][own[a], j * HEAD_DIM:(j + 1) * HEAD_DIM].astype(F32), blk0 + a) * scale
                 for j in range(Q_PER_KV)], axis=0).astype(BF16)
            k_loc = jnp.concatenate(k_blocks[a:a + 3], axis=0)
            v_loc = jnp.concatenate(v_blocks[a:a + 3], axis=0)
            s_loc = _dot_nt(q, k_loc).reshape(Q_PER_KV, BLOCK, n_loc)
            s_loc = jnp.where(valid[a], s_loc, jnp.finfo(F32).min).reshape(Q_PER_KV * BLOCK, n_loc)
            s_ctx = _dot_nt(q, kx_ref[:, hs])
            m = jnp.maximum(jnp.maximum(jnp.max(s_loc, axis=-1, keepdims=True),
                                        jnp.max(s_ctx, axis=-1, keepdims=True)), sink_col)
            p_loc = jnp.exp2(s_loc - m)
            p_ctx = jnp.exp2(s_ctx - m)
            denom = (jnp.sum(p_loc, axis=-1, keepdims=True) + jnp.sum(p_ctx, axis=-1, keepdims=True)
                     + jnp.exp2(sink_col - m))
            o = (_dot(p_loc.astype(BF16), v_loc) + _dot(p_ctx.astype(BF16), vx_ref[:, hs])) / denom
            for j in range(Q_PER_KV):
                h = g * Q_PER_KV + j
                o_ref[own[a], h * HEAD_DIM:(h + 1) * HEAD_DIM] = o[j * BLOCK:(j + 1) * BLOCK].astype(o_ref.dtype)


def _attn_lat_call(cfg, p, sink, cos_a, sin_a):
    b, t, l = cfg.b, cfg.t, cfg.l
    nb = t // BLOCK
    kvh = cfg.kvh
    gw = Q_PER_KV * HEAD_DIM
    kvw = kvh * HEAD_DIM
    q_blk = cfg.off_qa // gw
    k_blk, v_blk = cfg.off_ka // kvw, cfg.off_va // kvw
    ctx_row0 = (b * t) // l

    qb = np.gcd(ATTN_Q_BLOCKS, nb).item()
    steps = nb // qb

    def own_spec(width, col_blk):
        return pl.BlockSpec((qb * BLOCK, width), lambda bi, n: (bi * steps + n, col_blk))

    def edge_spec(col_blk, shift):
        def imap(bi, n):
            return (bi * nb + jnp.clip(n * qb + shift, 0, nb - 1), col_blk)
        return pl.BlockSpec((BLOCK, kvw), imap)

    in_specs = [pl.BlockSpec(memory_space=pltpu.SMEM)]
    in_specs += [own_spec(gw, q_blk + g) for g in range(kvh)]
    in_specs += [edge_spec(k_blk, -1), own_spec(kvw, k_blk), edge_spec(k_blk, qb),
                 edge_spec(v_blk, -1), own_spec(kvw, v_blk), edge_spec(v_blk, qb),
                 pl.BlockSpec((l, kvw), lambda bi, n: (ctx_row0 + bi, k_blk)),
                 pl.BlockSpec((l, kvw), lambda bi, n: (ctx_row0 + bi, v_blk)),
                 pl.BlockSpec((t, HEAD_DIM), lambda bi, n: (0, 0)),
                 pl.BlockSpec((t, HEAD_DIM), lambda bi, n: (0, 0))]
    return pl.pallas_call(
        functools.partial(_attn_lat_kernel, kvh=kvh, t=t, qb=qb),
        grid=(b, steps),
        in_specs=in_specs,
        out_specs=own_spec(cfg.a_q, 0),
        out_shape=jax.ShapeDtypeStruct((b * t, cfg.a_q), BF16),
        compiler_params=_cparams(2),
        name="attn_latent",
    )(sink, *([p] * (kvh + 8)), cos_a, sin_a)


def _attn_ctx_kernel(sink_ref, *refs, kvh):
    q_refs = refs[:kvh]
    kx_ref, vx_ref, o_ref = refs[kvh:]
    l = kx_ref.shape[0]
    scale = HEAD_DIM ** -0.5
    for g in range(kvh):
        hs = slice(g * HEAD_DIM, (g + 1) * HEAD_DIM)
        q = jnp.concatenate(
            [(q_refs[g][:, j * HEAD_DIM:(j + 1) * HEAD_DIM].astype(F32) * scale).astype(BF16)
             for j in range(Q_PER_KV)], axis=0)
        s = _dot_nt(q, kx_ref[:, hs])
        o = _softmax_pv(s, _sink_column(sink_ref, g, l), vx_ref[:, hs])
        for j in range(Q_PER_KV):
            h = g * Q_PER_KV + j
            o_ref[:, h * HEAD_DIM:(h + 1) * HEAD_DIM] = o[j * l:(j + 1) * l].astype(o_ref.dtype)


def _attn_ctx_call(cfg, p, sink):
    b, t, l = cfg.b, cfg.t, cfg.l
    kvh = cfg.kvh
    gw = Q_PER_KV * HEAD_DIM
    kvw = kvh * HEAD_DIM
    q_blk = cfg.off_qa // gw
    k_blk, v_blk = cfg.off_ka // kvw, cfg.off_va // kvw
    ctx_row0 = (b * t) // l
    in_specs = [pl.BlockSpec(memory_space=pltpu.SMEM)]
    in_specs += [pl.BlockSpec((l, gw), functools.partial(lambda bi, g: (ctx_row0 + bi, q_blk + g), g=g))
                 for g in range(kvh)]
    in_specs += [pl.BlockSpec((l, kvw), lambda bi: (ctx_row0 + bi, k_blk)),
                 pl.BlockSpec((l, kvw), lambda bi: (ctx_row0 + bi, v_blk))]
    return pl.pallas_call(
        functools.partial(_attn_ctx_kernel, kvh=kvh),
        grid=(b,),
        in_specs=in_specs,
        out_specs=pl.BlockSpec((l, cfg.a_q), lambda bi: (bi, 0)),
        out_shape=jax.ShapeDtypeStruct((b * l, cfg.a_q), BF16),
        compiler_params=_cparams(1),
        name="attn_ctx",
    )(sink, *([p] * (kvh + 2)))


def _retention_scan(q_of, k_of, v_ref, g_ref, gn, lg_f, lg_b, o_ref, acc_ref, n_chunks, s_f, s_b):
    c = BLOCK
    heads = range(len(lg_f))
    ii = lax.broadcasted_iota(jnp.int32, (c, c), 0).astype(F32)
    jj = lax.broadcasted_iota(jnp.int32, (c, c), 1).astype(F32)
    diff = ii - jj
    decay = [jnp.where(diff > 0, jnp.exp(lg_f[h] * jnp.maximum(diff, 0.0)),
                       jnp.where(diff < 0, jnp.exp(lg_b[h] * jnp.maximum(-diff, 0.0)), 2.0)) for h in heads]
    zeta_f = [jnp.exp(lg_f[h] * (c - 1 - ii)) for h in heads]
    xi_f = [jnp.exp(lg_f[h] * (ii + 1)) for h in heads]
    zeta_b = [jnp.exp(lg_b[h] * ii) for h in heads]
    xi_b = [jnp.exp(lg_b[h] * (c - ii)) for h in heads]
    cd_f = [jnp.exp(jnp.concatenate([lg_f[h], lg_f[h]], axis=1) * c) for h in heads]
    cd_b = [jnp.exp(jnp.concatenate([lg_b[h], lg_b[h]], axis=1) * c) for h in heads]
    cols = [slice(h * R_DV, (h + 1) * R_DV) for h in heads]
    s_f, s_b = list(s_f), list(s_b)

    for i in reversed(range(n_chunks)):
        rows = slice(i * c, (i + 1) * c)
        for h in heads:
            q, k, v = q_of(h, i), k_of(h, i), v_ref[rows, cols[h]]
            acc_ref[rows, cols[h]] = _dot((q * xi_b[h]).astype(BF16), s_b[h].astype(BF16))
            s_b[h] = cd_b[h] * s_b[h] + _dot_tn((k * zeta_b[h]).astype(BF16), v)

    for i in range(n_chunks):
        rows = slice(i * c, (i + 1) * c)
        for h in heads:
            q, k, v = q_of(h, i), k_of(h, i), v_ref[rows, cols[h]]
            scores = _dot_nt(q.astype(BF16), k.astype(BF16)) * decay[h]
            o = (acc_ref[rows, cols[h]] + _dot(scores.astype(BF16), v)
                 + _dot((q * xi_f[h]).astype(BF16), s_f[h].astype(BF16)))
            s_f[h] = cd_f[h] * s_f[h] + _dot_tn((k * zeta_f[h]).astype(BF16), v)
            mu = jnp.mean(o, axis=-1, keepdims=True)
            dev = o - mu
            var = jnp.mean(dev * dev, axis=-1, keepdims=True)
            on = dev * lax.rsqrt(var + EPS) * gn[:, cols[h]]
            gate = g_ref[rows, cols[h]].astype(F32)
            o_ref[rows, cols[h]] = (_silu(gate) * on).astype(o_ref.dtype)
    return s_f, s_b


def _log_decay(a_row):
    return jnp.log1p(-jnp.exp2(-a_row))


def _retention_kernel(qc_ref, kc_ref, vc_ref, gc_ref, q_ref, k_ref, v_ref, g_ref, gn_ref, af_ref, ab_ref,
                      cos_ref, sin_ref, oc_ref, o_ref, acc_ref, qr_ref, kr_ref):
    scale = HEAD_DIM ** -0.5
    heads = range(RET_HEADS_PER_STEP)
    lg_f = [_log_decay(af_ref[h]) for h in heads]
    lg_b = [_log_decay(ab_ref[h]) for h in heads]
    gn = gn_ref[...]
    hcols = [slice(h * HEAD_DIM, (h + 1) * HEAD_DIM) for h in heads]

    def qc_of(h, i):
        return qc_ref[i * BLOCK:(i + 1) * BLOCK, hcols[h]].astype(F32)

    def kc_of(h, i):
        return kc_ref[i * BLOCK:(i + 1) * BLOCK, hcols[h]].astype(F32) * scale

    zero = [jnp.zeros((HEAD_DIM, R_DV), F32) for _ in heads]
    s_f, s_b = _retention_scan(qc_of, kc_of, vc_ref, gc_ref, gn, lg_f, lg_b, oc_ref, acc_ref,
                               qc_ref.shape[0] // BLOCK, zero, zero)

    cos, sin = cos_ref[...], sin_ref[...]

    def rot(x):
        return x * cos + pltpu.roll(x, HEAD_DIM // 2, 1) * sin

    for h in heads:
        qr_ref[:, hcols[h]] = rot(q_ref[:, hcols[h]].astype(F32))
        kr_ref[:, hcols[h]] = rot(k_ref[:, hcols[h]].astype(F32)) * scale

    def q_of(h, i):
        return qr_ref[i * BLOCK:(i + 1) * BLOCK, hcols[h]]

    def k_of(h, i):
        return kr_ref[i * BLOCK:(i + 1) * BLOCK, hcols[h]]

    _retention_scan(q_of, k_of, v_ref, g_ref, gn, lg_f, lg_b, o_ref, acc_ref, q_ref.shape[0] // BLOCK, s_f, s_b)


def _retention_call(cfg, p, gn, a_f, a_b, cos_r, sin_r):
    b, t, l, rh = cfg.b, cfg.t, cfg.l, cfg.rh
    assert l <= t
    ctx_row0 = (b * t) // l
    hp = RET_HEADS_PER_STEP
    kw, vw = hp * HEAD_DIM, hp * R_DV
    assert rh % hp == 0 and all(off % kw == 0 for off in (cfg.off_qr, cfg.off_kr))
    assert all(off % vw == 0 for off in (cfg.off_vr, cfg.off_gr))
    qb, kb = cfg.off_qr // kw, cfg.off_kr // kw
    vb, gb = cfg.off_vr // vw, cfg.off_gr // vw

    def head_specs(rows, row_blk):
        return [pl.BlockSpec((rows, kw), lambda bi, h: (row_blk(bi), qb + h)),
                pl.BlockSpec((rows, kw), lambda bi, h: (row_blk(bi), kb + h)),
                pl.BlockSpec((rows, vw), lambda bi, h: (row_blk(bi), vb + h)),
                pl.BlockSpec((rows, vw), lambda bi, h: (row_blk(bi), gb + h))]

    table_spec = pl.BlockSpec((t, HEAD_DIM), lambda bi, h: (0, 0))
    decay_spec = pl.BlockSpec((hp, 1, HEAD_DIM), lambda bi, h: (h, 0, 0))
    in_specs = (head_specs(l, lambda bi: ctx_row0 + bi) + head_specs(t, lambda bi: bi)
                + [pl.BlockSpec((1, vw), lambda bi, h: (0, h)), decay_spec, decay_spec, table_spec, table_spec])
    return pl.pallas_call(
        _retention_kernel,
        grid=(b, rh // hp),
        in_specs=in_specs,
        out_specs=[pl.BlockSpec((l, vw), lambda bi, h: (bi, h)),
                   pl.BlockSpec((t, vw), lambda bi, h: (bi, h))],
        out_shape=[jax.ShapeDtypeStruct((b * l, cfg.r_v), BF16),
                   jax.ShapeDtypeStruct((b * t, cfg.r_v), BF16)],
        scratch_shapes=[pltpu.VMEM((t, vw), F32), pltpu.VMEM((t, kw), F32),
                        pltpu.VMEM((t, kw), F32)],
        compiler_params=_cparams(2),
        name="retention",
    )(*([p] * 8), gn, a_f, a_b, cos_r, sin_r)


def _rotary_tables(t):
    rows = t // GRID_W
    row = jnp.repeat(jnp.arange(rows, dtype=F32), GRID_W)
    col = jnp.tile(jnp.arange(GRID_W, dtype=F32), rows)
    nf = HEAD_DIM // 4
    inv = jnp.power(ROPE_BASE, -jnp.arange(nf, dtype=F32) / nf)
    ang_row, ang_col = row[:, None] * inv, col[:, None] * inv
    cos_a = jnp.concatenate([jnp.cos(ang_row)] * 2 + [jnp.cos(ang_col)] * 2, axis=1)
    sin_a = jnp.concatenate([-jnp.sin(ang_row), jnp.sin(ang_row), -jnp.sin(ang_col), jnp.sin(ang_col)], axis=1)
    nf = HEAD_DIM // 2
    inv = jnp.power(ROPE_BASE, -jnp.arange(nf, dtype=F32) / nf)
    ang = jnp.arange(t, dtype=F32)[:, None] * inv
    cos_r = jnp.concatenate([jnp.cos(ang)] * 2, axis=1)
    sin_r = jnp.concatenate([-jnp.sin(ang), jnp.sin(ang)], axis=1)
    return cos_a, sin_a, cos_r, sin_r


class _Config:
    def __init__(self, x, ctx, w_in, w_gate_d):
        self.b, self.t, self.d = x.shape
        self.l = ctx.shape[1]
        self.ntok = self.b * (self.t + self.l)
        self.ha = self.d // 256
        self.kvh = self.ha // Q_PER_KV
        self.rh = self.d // 256
        self.a_q = self.ha * HEAD_DIM
        self.a_kv = self.kvh * HEAD_DIM
        self.r_qk = self.rh * HEAD_DIM
        self.r_v = self.rh * R_DV
        self.off_ka = 0
        self.off_va = self.a_kv
        self.off_kr = 2 * self.a_kv
        self.off_vr = self.off_kr + self.r_qk
        self.off_qa = self.off_vr + self.r_v
        self.off_qr = self.off_qa + self.a_q
        self.off_gr = self.off_qr + self.r_qk
        self.off_za = self.off_gr + self.r_v
        self.off_zr = self.off_za + self.d
        assert w_in.shape[-1] == self.off_zr + self.d
        self.bm = min(ROW_TILE, self.t)
        self.norm_bm = min(NORM_ROW_TILE, self.t)
        assert self.t % self.bm == 0 and (self.b * self.l) % self.bm == 0
        assert self.t % GRID_W == 0 and self.t % BLOCK == 0 and self.l % BLOCK == 0

    def bidx_for(self, tile):
        assert self.t % tile == 0 and (self.b * self.l) % tile == 0
        tiles_per_batch = self.t // tile
        n_batch = self.b
        return lambda i: jnp.minimum(i // tiles_per_batch, n_batch)


def kernel(x, c, ctx, c_ctx, w_ada, b_ada, g_mix, g_ffn, w_in, attn_sink, ret_a_fwd, ret_a_bwd, ret_gn,
           w_br_attn, w_br_ret, w_out, w_gate_d, w_up_d, w_down_d, w_router, w_gate_e, w_up_e, w_down_e,
           g_final):
    cfg = _Config(x, ctx, w_in, w_gate_d)
    b, t, d, l = cfg.b, cfg.t, cfg.d, cfg.l
    depth = w_ada.shape[0]
    n_lat = b * t

    x_all = (x.reshape(n_lat, d), ctx.reshape(b * l, d))
    mod_rows = -(-(b + 1) // 8) * 8
    c_all = jnp.zeros((mod_rows, d), F32).at[:b].set(c).at[b].set(c_ctx)
    mod = _ada_call(c_all, w_ada, b_ada).reshape(depth, mod_rows, 1, 6 * d)

    cos_a, sin_a, cos_r, sin_r = _rotary_tables(t)
    w_br_attn_h, w_br_ret_h, w_out_h, w_down_d_h = (w.astype(BF16) for w in (w_br_attn, w_br_ret, w_out, w_down_d))
    w_gate_e_h, w_up_e_h = w_gate_e.astype(BF16), w_up_e.astype(BF16)
    w_router_h = jnp.pad(w_router, ((0, 0), (0, 0), (0, LANES - N_EXPERTS))).astype(BF16)
    a_f = jnp.broadcast_to(ret_a_fwd.astype(F32)[:, :, None, None], ret_a_fwd.shape + (1, HEAD_DIM))
    a_b = jnp.broadcast_to(ret_a_bwd.astype(F32)[:, :, None, None], ret_a_bwd.shape + (1, HEAD_DIM))

    for layer in range(depth):
        last = layer == depth - 1
        n_rows = n_lat if last else cfg.ntok
        mod_l = mod[layer]
        gn = ret_gn[layer].reshape(1, cfg.r_v)

        h = _norm_mod_call(cfg, x_all, g_mix[layer], mod_l, 0, cfg.ntok)
        p = _in_proj_call(cfg, h, w_in, layer, ctx_kv_only=last)
        y_a = _attn_lat_call(cfg, p, attn_sink[layer], cos_a, sin_a)
        y_r_ctx, y_r = _retention_call(cfg, p, gn, a_f[layer], a_b[layer], cos_r, sin_r)
        if last:
            y_a_ctx, y_r_ctx = y_a, y_r
        else:
            y_a_ctx = _attn_ctx_call(cfg, p, attn_sink[layer])
        m = _merge_call(cfg, y_a, y_r, y_a_ctx, y_r_ctx, w_br_attn_h, w_br_ret_h, layer, p, n_rows)
        x_all = _resid_call(cfg, m, w_out_h, layer, x_all, mod_l, 2, n_rows, bn=1024)

        i = layer // 2
        if layer % 2 == 0:
            hf = _norm_mod_call(cfg, x_all, g_ffn[layer], mod_l, 3, n_rows)
            act = _glu_call(cfg, hf, w_gate_d, w_up_d, i, n_rows)
            x_all = _resid_call(cfg, act, w_down_d_h, i, x_all, mod_l, 5, n_rows, bn=512)
        else:
            x_all = _moe_ffn(cfg, x_all, g_ffn[layer], mod_l, w_router_h, w_gate_e_h, w_up_e_h, w_down_e,
                             i, n_rows)

    return _final_norm_call(cfg, x_all, g_final).reshape(b, t, d)
```
